```python
import jax, jax.numpy as jnp
from jax import lax
import numpy as np

D_MODEL = 1024
BATCH = 8
SEQ = 2048
DEPTH = 4
DEC_BATCH = 128
DEC_SEQ = 4
PAST_LEN = 8192
PAGE_SIZE = 128

N_HEADS = 8
HEAD_DIM = 64
N_KV_HEADS = 2
GROUP = N_HEADS // N_KV_HEADS
ATTN_W = N_HEADS * HEAD_DIM
KV_W = N_KV_HEADS * HEAD_DIM
CONV_W = D_MODEL - ATTN_W
CONV_K = 3
WINDOW = 128
BLOCK = 128
W_BUF = min(WINDOW, PAST_LEN)
ROPE_THETA = 10000.0
PLE_DIM = 256
EPS = 1e-6
SCALE = HEAD_DIM ** -0.5
NEG = -1e30
IN_W = 2 * ATTN_W + 2 * KV_W + 4 * CONV_W
SPLITS = (ATTN_W, ATTN_W + KV_W, ATTN_W + 2 * KV_W, 2 * ATTN_W + 2 * KV_W,
          2 * ATTN_W + 2 * KV_W + CONV_W, 2 * ATTN_W + 2 * KV_W + 2 * CONV_W,
          2 * ATTN_W + 2 * KV_W + 3 * CONV_W)

kernel_name = "hymba_swa_sink_shortconv_ple_step"


def rmsnorm(x, g):
    xf = x.astype(jnp.float32)
    r = lax.rsqrt(jnp.mean(xf * xf, axis=-1, keepdims=True) + EPS)
    return (xf * r).astype(x.dtype) * g


def rope(x, pos):
    inv = ROPE_THETA ** (-jnp.arange(0, HEAD_DIM, 2, dtype=jnp.float32) / HEAD_DIM)
    ang = pos[:, None] * inv[None, :]
    cos = jnp.concatenate([jnp.cos(ang), jnp.cos(ang)], -1)[None, :, None, :].astype(x.dtype)
    sin = jnp.concatenate([jnp.sin(ang), jnp.sin(ang)], -1)[None, :, None, :].astype(x.dtype)
    x1, x2 = jnp.split(x, 2, axis=-1)
    return x * cos + jnp.concatenate([-x2, x1], -1) * sin


def sink_softmax(s, sink):
    m = jnp.maximum(jnp.max(s, axis=-1, keepdims=True), sink)
    e = jnp.exp(s - m)
    return e / (jnp.sum(e, axis=-1, keepdims=True) + jnp.exp(sink - m))


def branch_inputs(x, g_norm, w_in, pos):
    b, t = x.shape[0], x.shape[1]
    z = rmsnorm(x, g_norm) @ w_in
    q, k, v, ga, bg, cg, hc, gc = jnp.split(z, SPLITS, axis=-1)
    q = rope(q.reshape(b, t, N_HEADS, HEAD_DIM), pos)
    k = rope(k.reshape(b, t, N_KV_HEADS, HEAD_DIM), pos)
    v = v.reshape(b, t, N_KV_HEADS, HEAD_DIM)
    return q, k, v, ga, bg, cg * hc, gc


def attn_prompt(q, k, v, sink):
    b, s_len = q.shape[0], q.shape[1]
    nb = s_len // BLOCK
    qb = q.reshape(b, nb, BLOCK, N_KV_HEADS, GROUP, HEAD_DIM)

    def with_prev(t):
        tb = t.reshape(b, nb, BLOCK, N_KV_HEADS, HEAD_DIM)
        prev = jnp.pad(tb[:, :-1], ((0, 0), (1, 0), (0, 0), (0, 0), (0, 0)))
        return jnp.concatenate([prev, tb], axis=2)

    kk, vv = with_prev(k), with_prev(v)
    s = jnp.einsum('bnqhgd,bnkhd->bnhgqk', qb, kk, preferred_element_type=jnp.float32) * SCALE
    qi = jnp.arange(BLOCK)[:, None]
    kj = jnp.arange(2 * BLOCK)[None, :]
    diff = BLOCK + qi - kj
    blk = jnp.arange(nb)[:, None, None]
    valid = (diff >= 0) & (diff < WINDOW) & (blk * BLOCK - BLOCK + kj >= 0)
    s = jnp.where(valid[None, :, None, None], s, NEG)
    p = sink_softmax(s, sink.astype(jnp.float32).reshape(1, 1, N_KV_HEADS, GROUP, 1, 1))
    o = jnp.einsum('bnhgqk,bnkhd->bnqhgd', p.astype(vv.dtype), vv)
    return o.reshape(b, s_len, ATTN_W)


def attn_sample(q, k, v, kbuf, vbuf, sink):
    b, t = q.shape[0], q.shape[1]
    kk = jnp.concatenate([kbuf, k], axis=1)
    vv = jnp.concatenate([vbuf, v], axis=1)
    qpos = PAST_LEN + jnp.arange(t)
    kpos = jnp.concatenate([PAST_LEN - W_BUF + jnp.arange(W_BUF), qpos])
    diff = qpos[:, None] - kpos[None, :]
    valid = (diff >= 0) & (diff < WINDOW)
    qg = q.reshape(b, t, N_KV_HEADS, GROUP, HEAD_DIM)
    s = jnp.einsum('bqhgd,bkhd->bhgqk', qg, kk, preferred_element_type=jnp.float32) * SCALE
    s = jnp.where(valid[None, None, None], s, NEG)
    p = sink_softmax(s, sink.astype(jnp.float32).reshape(1, N_KV_HEADS, GROUP, 1, 1))
    o = jnp.einsum('bhgqk,bkhd->bqhgd', p.astype(vv.dtype), vv)
    return o.reshape(b, t, ATTN_W), kk[:, -W_BUF:], vv[:, -W_BUF:]


def causal_conv(u_pad, w, t):
    y = w[0] * u_pad[:, 0:t]
    for j in range(1, CONV_K):
        y = y + w[j] * u_pad[:, j:j + t]
    return y


def branch_output(x, o_attn, ga, bg, conv_out, gc, w_out, p_i, w_pg, w_pp):
    mix = jnp.concatenate([o_attn * jax.nn.silu(ga), bg * conv_out * jax.nn.silu(gc)], axis=-1)
    x = x + mix @ w_out
    return x + jax.nn.sigmoid(x @ w_pg) * (p_i @ w_pp)


def setup_inputs(seed: int = 0) -> dict:
    key = jax.random.key(seed)
    ks = jax.random.split(key, 18)
    f32 = jnp.float32
    n = lambda k, s, sc: jax.random.normal(k, s, f32) * sc
    return {
        "x_prompt": n(ks[0], (BATCH, SEQ, D_MODEL), 1.0),
        "x_sample": n(ks[1], (DEC_BATCH, DEC_SEQ, D_MODEL), 1.0),
        "cache_k": n(ks[2], (DEPTH, DEC_BATCH, W_BUF, N_KV_HEADS, HEAD_DIM), 1.0),
        "cache_v": n(ks[3], (DEPTH, DEC_BATCH, W_BUF, N_KV_HEADS, HEAD_DIM), 1.0),
        "state_conv": n(ks[4], (DEPTH, DEC_BATCH, CONV_K - 1, CONV_W), 1.0),
        "p_prompt": n(ks[5], (DEPTH, BATCH, SEQ, PLE_DIM), 1.0),
        "p_sample": n(ks[6], (DEPTH, DEC_BATCH, DEC_SEQ, PLE_DIM), 1.0),
        "g_norm": 1.0 + n(ks[7], (DEPTH, D_MODEL), 0.05),
        "w_in": n(ks[8], (DEPTH, D_MODEL, IN_W), D_MODEL ** -0.5),
        "sinks": n(ks[9], (DEPTH, N_HEADS), 0.5),
        "conv_w": n(ks[10], (DEPTH, CONV_K, CONV_W), CONV_K ** -0.5),
        "w_out": n(ks[11], (DEPTH, D_MODEL, D_MODEL), D_MODEL ** -0.5),
        "w_pg": n(ks[12], (DEPTH, D_MODEL, D_MODEL), D_MODEL ** -0.5),
        "w_pp": n(ks[13], (DEPTH, PLE_DIM, D_MODEL), PLE_DIM ** -0.5),
        "g_final": 1.0 + n(ks[14], (D_MODEL,), 0.05),
    }


def reference(x_prompt, x_sample, cache_k, cache_v, state_conv, p_prompt, p_sample,
              g_norm, w_in, sinks, conv_w, w_out, w_pg, w_pp, g_final):
    pos_p = jnp.arange(SEQ, dtype=jnp.float32)
    pos_s = PAST_LEN + jnp.arange(DEC_SEQ, dtype=jnp.float32)
    xp, xs = x_prompt, x_sample
    nkp, nvp, ncp, nks, nvs, ncs = [], [], [], [], [], []
    for i in range(DEPTH):
        q, k, v, ga, bg, u, gc = branch_inputs(xp, g_norm[i], w_in[i], pos_p)
        oa = attn_prompt(q, k, v, sinks[i])
        u_pad = jnp.pad(u, ((0, 0), (CONV_K - 1, 0), (0, 0)))
        yc = causal_conv(u_pad, conv_w[i], SEQ)
        xp = branch_output(xp, oa, ga, bg, yc, gc, w_out[i], p_prompt[i], w_pg[i], w_pp[i])
        nkp.append(k[:, SEQ - W_BUF:])
        nvp.append(v[:, SEQ - W_BUF:])
        ncp.append(u[:, SEQ - (CONV_K - 1):])
        q, k, v, ga, bg, u, gc = branch_inputs(xs, g_norm[i], w_in[i], pos_s)
        oa, kb, vb = attn_sample(q, k, v, cache_k[i], cache_v[i], sinks[i])
        u_pad = jnp.concatenate([state_conv[i], u], axis=1)
        yc = causal_conv(u_pad, conv_w[i], DEC_SEQ)
        xs = branch_output(xs, oa, ga, bg, yc, gc, w_out[i], p_sample[i], w_pg[i], w_pp[i])
        nks.append(kb)
        nvs.append(vb)
        ncs.append(u_pad[:, -(CONV_K - 1):])
    y_prompt = rmsnorm(xp, g_final)
    y_sample = rmsnorm(xs, g_final)
    return (y_prompt, y_sample, jnp.stack(nkp), jnp.stack(nvp), jnp.stack(ncp),
            jnp.stack(nks), jnp.stack(nvs), jnp.stack(ncs))
```

```python
import functools

import jax
import jax.numpy as jnp
from jax import lax
from jax.experimental import pallas as pl
from jax.experimental.pallas import tpu as pltpu

D_MODEL = 1024
DEPTH = 4
N_HEADS = 8
HEAD_DIM = 64
N_KV_HEADS = 2
GROUP = N_HEADS // N_KV_HEADS
ATTN_W = N_HEADS * HEAD_DIM
KV_W = N_KV_HEADS * HEAD_DIM
CONV_W = D_MODEL - ATTN_W
CONV_K = 3
WINDOW = 128
BLOCK = 128
PLE_DIM = 256
PAST_LEN = 8192
ROPE_THETA = 10000.0
EPS = 1e-6
SCALE = HEAD_DIM ** -0.5
NEG = -1e30
IN_W = 2 * ATTN_W + 2 * KV_W + 4 * CONV_W

C_Q = 0
C_K = ATTN_W
C_V = C_K + KV_W
C_GA = C_V + KV_W
C_BG = C_GA + ATTN_W
C_CG = C_BG + CONV_W
C_HC = C_CG + CONV_W
C_GC = C_HC + CONV_W

LANES = 128
SUBLANES = 8
ROPE_HALF = HEAD_DIM // 2

PROMPT_TILE = 512
SAMPLE_GROUP = 8
VMEM_LIMIT = 56 * 1024 * 1024

BF = jnp.bfloat16
F32 = jnp.float32
_NT = (((1,), (1,)), ((), ()))


def _mm(a, w):
    return jnp.dot(a, w, preferred_element_type=F32)


def _sigmoid(x):
    return 1.0 / (1.0 + jnp.exp(-x))


def _silu(x):
    return x * _sigmoid(x)


def _rmsnorm(x, g):
    r = lax.rsqrt(jnp.mean(x * x, axis=-1, keepdims=True) + EPS)
    return (x * r) * g


def _rope(x, cos, sin_signed):
    rows = x.shape[0]
    lane = lax.broadcasted_iota(jnp.int32, (rows, LANES), 1)
    first_half = (lane % HEAD_DIM) < ROPE_HALF
    outs = []
    for c in range(x.shape[1] // LANES):
        xc = x[:, c * LANES:(c + 1) * LANES]
        rot = jnp.where(first_half, pltpu.roll(xc, LANES - ROPE_HALF, 1), pltpu.roll(xc, ROPE_HALF, 1))
        outs.append(xc * cos + rot * sin_signed)
    return outs[0] if len(outs) == 1 else jnp.concatenate(outs, axis=1)


def _conv_gate(u, u1, u2, cw_ref, bg, gc):
    yc = cw_ref[0:1, :] * u2 + cw_ref[1:2, :] * u1 + cw_ref[2:3, :] * u
    return bg * yc * _silu(gc)


def _output_stage(x, mix, p, wout_ref, wpg_ref, wpp_ref, gfin_ref, final):
    x1 = x + _mm(mix, wout_ref[...])
    gate = _sigmoid(_mm(x1.astype(BF), wpg_ref[...]))
    x2 = x1 + gate * _mm(p.astype(BF), wpp_ref[...])
    if final:
        x2 = _rmsnorm(x2, gfin_ref[...])
    return x2


def _prompt_kernel(sink_ref, x_ref, p_ref, cos_ref, sin_ref, gn_ref, win_ref, cw_ref, wout_ref, wpg_ref,
                   wpp_ref, gfin_ref, xo_ref, kt_ref, vt_ref, ut_ref, kbuf, vbuf, ubuf, mix_buf, *, tile, final):
    s = pl.program_id(1)
    last = pl.num_programs(1) - 1

    @pl.when(s == 0)
    def _():
        kbuf[0:BLOCK, :] = jnp.zeros((BLOCK, KV_W), BF)
        vbuf[0:BLOCK, :] = jnp.zeros((BLOCK, KV_W), BF)
        ubuf[0:SUBLANES, :] = jnp.zeros((SUBLANES, CONV_W), F32)

    x = x_ref[0]
    xn = _rmsnorm(x, gn_ref[...]).astype(BF)
    cos = cos_ref[...]
    sin_s = sin_ref[...]

    q = (_rope(_mm(xn, win_ref[:, C_Q:C_K]), cos, sin_s) * SCALE).astype(BF)
    k = _rope(_mm(xn, win_ref[:, C_K:C_V]), cos, sin_s)
    v = _mm(xn, win_ref[:, C_V:C_GA])
    kbuf[BLOCK:BLOCK + tile, :] = k.astype(BF)
    vbuf[BLOCK:BLOCK + tile, :] = v.astype(BF)

    @pl.when(s == last)
    def _():
        kt_ref[0] = k[tile - BLOCK:, :]
        vt_ref[0] = v[tile - BLOCK:, :]

    sga = _silu(_mm(xn, win_ref[:, C_GA:C_BG]))

    row = lax.broadcasted_iota(jnp.int32, (BLOCK, BLOCK), 0)
    col = lax.broadcasted_iota(jnp.int32, (BLOCK, BLOCK), 1)
    own = col <= row
    no_prev = jnp.where(s == 0, NEG, 0.0).astype(F32)

    for j in range(tile // BLOCK):
        heads = []
        for g in range(N_KV_HEADS):
            kcat = kbuf[j * BLOCK:(j + 2) * BLOCK, g * HEAD_DIM:(g + 1) * HEAD_DIM]
            vcat = vbuf[j * BLOCK:(j + 2) * BLOCK, g * HEAD_DIM:(g + 1) * HEAD_DIM]
            qst = jnp.concatenate(
                [q[j * BLOCK:(j + 1) * BLOCK, (GROUP * g + h) * HEAD_DIM:(GROUP * g + h + 1) * HEAD_DIM]
                 for h in range(GROUP)], axis=0)
            sc = lax.dot_general(qst, kcat, _NT, preferred_element_type=F32)
            probs, invs = [], []
            for h in range(GROUP):
                sh = sc[h * BLOCK:(h + 1) * BLOCK]
                s_prev = sh[:, :BLOCK]
                if j == 0:
                    s_prev = s_prev + no_prev
                sm = jnp.where(own, sh[:, BLOCK:], s_prev)
                sink = sink_ref[GROUP * g + h]
                m = jnp.maximum(jnp.max(sm, axis=-1, keepdims=True), sink)
                e = jnp.exp(sm - m)
                den = jnp.sum(e, axis=-1, keepdims=True) + jnp.exp(sink - m)
                invs.append(1.0 / den)
                eb = e.astype(BF)
                zero = jnp.zeros_like(eb)
                probs.append(jnp.concatenate([jnp.where(own, zero, eb), jnp.where(own, eb, zero)], axis=1))
            o = _mm(jnp.concatenate(probs, axis=0), vcat)
            for h in range(GROUP):
                heads.append(o[h * BLOCK:(h + 1) * BLOCK] * invs[h])
        oa = jnp.concatenate(heads, axis=1)
        mix_buf[j * BLOCK:(j + 1) * BLOCK, 0:ATTN_W] = (oa * sga[j * BLOCK:(j + 1) * BLOCK]).astype(BF)

    kbuf[0:BLOCK, :] = kbuf[tile:tile + BLOCK, :]
    vbuf[0:BLOCK, :] = vbuf[tile:tile + BLOCK, :]

    u = _mm(xn, win_ref[:, C_CG:C_HC]) * _mm(xn, win_ref[:, C_HC:C_GC])
    ubuf[SUBLANES:SUBLANES + tile, :] = u
    u1 = ubuf[SUBLANES - 1:SUBLANES - 1 + tile, :]
    u2 = ubuf[SUBLANES - 2:SUBLANES - 2 + tile, :]
    bg = _mm(xn, win_ref[:, C_BG:C_CG])
    gc = _mm(xn, win_ref[:, C_GC:IN_W])
    mix_buf[:, ATTN_W:D_MODEL] = _conv_gate(u, u1, u2, cw_ref, bg, gc).astype(BF)
    ubuf[0:SUBLANES, :] = ubuf[tile:tile + SUBLANES, :]

    @pl.when(s == last)
    def _():
        ut_ref[0] = u[tile - (CONV_K - 1):, :]

    xo_ref[0] = _output_stage(x_ref[0], mix_buf[...], p_ref[0], wout_ref, wpg_ref, wpp_ref, gfin_ref, final)


def _prompt_layer(i, x, p_all, cos, sin_s, sinks, g_norm, w_in, conv_w, w_out, w_pg, w_pp, g_final, final):
    batch, seq, _ = x.shape
    tile = PROMPT_TILE
    const = lambda b, s: (0, 0)
    layer = lambda b, s: (i, 0, 0)
    return pl.pallas_call(
        functools.partial(_prompt_kernel, tile=tile, final=final),
        name=f"prompt_layer{i}",
        grid=(batch, seq // tile),
        in_specs=[
            pl.BlockSpec(memory_space=pltpu.SMEM),
            pl.BlockSpec((1, tile, D_MODEL), lambda b, s: (b, s, 0)),
            pl.BlockSpec((None, 1, tile, PLE_DIM), lambda b, s: (i, b, s, 0)),
            pl.BlockSpec((tile, LANES), lambda b, s: (s, 0)),
            pl.BlockSpec((tile, LANES), lambda b, s: (s, 0)),
            pl.BlockSpec((None, 1, D_MODEL), layer),
            pl.BlockSpec((None, D_MODEL, IN_W), layer),
            pl.BlockSpec((None, CONV_K, CONV_W), layer),
            pl.BlockSpec((None, D_MODEL, D_MODEL), layer),
            pl.BlockSpec((None, D_MODEL, D_MODEL), layer),
            pl.BlockSpec((None, PLE_DIM, D_MODEL), layer),
            pl.BlockSpec((1, D_MODEL), const),
        ],
        out_specs=[
            pl.BlockSpec((1, tile, D_MODEL), lambda b, s: (b, s, 0)),
            pl.BlockSpec((1, BLOCK, KV_W), lambda b, s: (b, 0, 0)),
            pl.BlockSpec((1, BLOCK, KV_W), lambda b, s: (b, 0, 0)),
            pl.BlockSpec((1, CONV_K - 1, CONV_W), lambda b, s: (b, 0, 0)),
        ],
        out_shape=[
            jax.ShapeDtypeStruct((batch, seq, D_MODEL), F32),
            jax.ShapeDtypeStruct((batch, BLOCK, KV_W), F32),
            jax.ShapeDtypeStruct((batch, BLOCK, KV_W), F32),
            jax.ShapeDtypeStruct((batch, CONV_K - 1, CONV_W), F32),
        ],
        scratch_shapes=[
            pltpu.VMEM((tile + BLOCK, KV_W), BF),
            pltpu.VMEM((tile + BLOCK, KV_W), BF),
            pltpu.VMEM((tile + SUBLANES, CONV_W), F32),
            pltpu.VMEM((tile, D_MODEL), BF),
        ],
        compiler_params=pltpu.CompilerParams(
            dimension_semantics=("arbitrary", "arbitrary"), vmem_limit_bytes=VMEM_LIMIT),
    )(sinks[i], x, p_all, cos, sin_s, g_norm, w_in, conv_w, w_out, w_pg, w_pp, g_final)


def _sample_kernel(sink_ref, x_ref, p_ref, cos_ref, sin_ref, s1_ref, s2_ref, gn_ref, win_ref, cw_ref, wout_ref,
                   wpg_ref, wpp_ref, gfin_ref, ck_ref, cv_ref, xo_ref, u_ref, nk_ref, nv_ref,
                   q_s, k_s, v_s, sga_s, mix_s, ubuf, *, rows, t_dec, nb, final):
    grp = pl.program_id(0)
    last = pl.num_programs(0) - 1
    gr = nb * t_dec

    @pl.when(grp == 0)
    def _():
        xn = _rmsnorm(x_ref[...], gn_ref[...]).astype(BF)
        cos = cos_ref[...]
        sin_s = sin_ref[...]
        q_s[...] = (_rope(_mm(xn, win_ref[:, C_Q:C_K]), cos, sin_s) * SCALE).astype(BF)
        k_s[...] = _rope(_mm(xn, win_ref[:, C_K:C_V]), cos, sin_s)
        v_s[...] = _mm(xn, win_ref[:, C_V:C_GA])
        sga_s[...] = _silu(_mm(xn, win_ref[:, C_GA:C_BG]))
        u = _mm(xn, win_ref[:, C_CG:C_HC]) * _mm(xn, win_ref[:, C_HC:C_GC])
        u_ref[...] = u
        ubuf[0:SUBLANES, :] = jnp.zeros((SUBLANES, CONV_W), F32)
        ubuf[SUBLANES:SUBLANES + rows, :] = u
        t = lax.broadcasted_iota(jnp.int32, (rows, CONV_W), 0) % t_dec
        u1 = jnp.where(t >= 1, ubuf[SUBLANES - 1:SUBLANES - 1 + rows, :], 0.0) + s1_ref[...]
        u2 = jnp.where(t >= 2, ubuf[SUBLANES - 2:SUBLANES - 2 + rows, :], 0.0) + s2_ref[...]
        bg = _mm(xn, win_ref[:, C_BG:C_CG])
        gc = _mm(xn, win_ref[:, C_GC:IN_W])
        mix_s[:, ATTN_W:D_MODEL] = _conv_gate(u, u1, u2, cw_ref, bg, gc).astype(BF)

    r0 = pl.multiple_of(grp * gr, gr)
    qg = q_s[pl.ds(r0, gr), :]
    kn = k_s[pl.ds(r0, gr), :]
    vn = v_s[pl.ds(r0, gr), :]
    kc = ck_ref[...].reshape(nb * WINDOW, KV_W)
    vc = cv_ref[...].reshape(nb * WINDOW, KV_W)

    qrow = lax.broadcasted_iota(jnp.int32, (gr, nb * WINDOW), 0)
    ccol = lax.broadcasted_iota(jnp.int32, (gr, nb * WINDOW), 1)
    valid_c = ((ccol // WINDOW) == (qrow // t_dec)) & ((ccol % WINDOW) > (qrow % t_dec))
    qrow_n = lax.broadcasted_iota(jnp.int32, (gr, gr), 0)
    ncol = lax.broadcasted_iota(jnp.int32, (gr, gr), 1)
    valid_n = ((ncol // t_dec) == (qrow_n // t_dec)) & ((ncol % t_dec) <= (qrow_n % t_dec))

    heads = []
    for g in range(N_KV_HEADS):
        lo, hi = g * HEAD_DIM, (g + 1) * HEAD_DIM
        kcg = kc[:, lo:hi].astype(BF)
        vcg = vc[:, lo:hi].astype(BF)
        kng = kn[:, lo:hi].astype(BF)
        vng = vn[:, lo:hi].astype(BF)
        qst = jnp.concatenate(
            [qg[:, (GROUP * g + h) * HEAD_DIM:(GROUP * g + h + 1) * HEAD_DIM] for h in range(GROUP)], axis=0)
        sc = lax.dot_general(qst, kcg, _NT, preferred_element_type=F32)
        sn = lax.dot_general(qst, kng, _NT, preferred_element_type=F32)
        ecs, ens, invs = [], [], []
        for h in range(GROUP):
            scm = jnp.where(valid_c, sc[h * gr:(h + 1) * gr], NEG)
            snm = jnp.where(valid_n, sn[h * gr:(h + 1) * gr], NEG)
            sink = sink_ref[GROUP * g + h]
            m = jnp.maximum(jnp.maximum(jnp.max(scm, axis=-1, keepdims=True),
                                        jnp.max(snm, axis=-1, keepdims=True)), sink)
            ec = jnp.exp(scm - m)
            en = jnp.exp(snm - m)
            den = (jnp.sum(ec, axis=-1, keepdims=True) + jnp.sum(en, axis=-1, keepdims=True)
                   + jnp.exp(sink - m))
            invs.append(1.0 / den)
            ecs.append(ec.astype(BF))
            ens.append(en.astype(BF))
        o = _mm(jnp.concatenate(ecs, axis=0), vcg) + _mm(jnp.concatenate(ens, axis=0), vng)
        for h in range(GROUP):
            heads.append(o[h * gr:(h + 1) * gr] * invs[h])
    oa = jnp.concatenate(heads, axis=1)
    mix_s[pl.ds(r0, gr), 0:ATTN_W] = (oa * sga_s[pl.ds(r0, gr), :]).astype(BF)

    nk_ref[:, 0:WINDOW - t_dec, :] = ck_ref[:, t_dec:WINDOW, :]
    nv_ref[:, 0:WINDOW - t_dec, :] = cv_ref[:, t_dec:WINDOW, :]
    for b in range(nb):
        nk_ref[b, WINDOW - t_dec:WINDOW, :] = kn[b * t_dec:(b + 1) * t_dec, :]
        nv_ref[b, WINDOW - t_dec:WINDOW, :] = vn[b * t_dec:(b + 1) * t_dec, :]

    @pl.when(grp == last)
    def _():
        xo_ref[...] = _output_stage(x_ref[...], mix_s[...], p_ref[...], wout_ref, wpg_ref, wpp_ref, gfin_ref,
                                    final)


def _sample_layer(i, x, p_all, cos, sin_s, s1, s2, sinks, g_norm, w_in, conv_w, w_out, w_pg, w_pp, g_final,
                  cache_k, cache_v, t_dec, final):
    rows = x.shape[0]
    dec_batch = rows // t_dec
    nb = SAMPLE_GROUP
    const = lambda g: (0, 0)
    layer = lambda g: (i, 0, 0)
    full = lambda shape: pl.BlockSpec(shape, const)
    return pl.pallas_call(
        functools.partial(_sample_kernel, rows=rows, t_dec=t_dec, nb=nb, final=final),
        name=f"sample_layer{i}",
        grid=(dec_batch // nb,),
        in_specs=[
            pl.BlockSpec(memory_space=pltpu.SMEM),
            full((rows, D_MODEL)),
            pl.BlockSpec((None, rows, PLE_DIM), layer),
            full((rows, LANES)),
            full((rows, LANES)),
            full((rows, CONV_W)),
            full((rows, CONV_W)),
            pl.BlockSpec((None, 1, D_MODEL), layer),
            pl.BlockSpec((None, D_MODEL, IN_W), layer),
            pl.BlockSpec((None, CONV_K, CONV_W), layer),
            pl.BlockSpec((None, D_MODEL, D_MODEL), layer),
            pl.BlockSpec((None, D_MODEL, D_MODEL), layer),
            pl.BlockSpec((None, PLE_DIM, D_MODEL), layer),
            full((1, D_MODEL)),
            pl.BlockSpec((None, nb, WINDOW, KV_W), lambda g: (i, g, 0, 0)),
            pl.BlockSpec((None, nb, WINDOW, KV_W), lambda g: (i, g, 0, 0)),
        ],
        out_specs=[
            full((rows, D_MODEL)),
            full((rows, CONV_W)),
            pl.BlockSpec((nb, WINDOW, KV_W), lambda g: (g, 0, 0)),
            pl.BlockSpec((nb, WINDOW, KV_W), lambda g: (g, 0, 0)),
        ],
        out_shape=[
            jax.ShapeDtypeStruct((rows, D_MODEL), F32),
            jax.ShapeDtypeStruct((rows, CONV_W), F32),
            jax.ShapeDtypeStruct((dec_batch, WINDOW, KV_W), F32),
            jax.ShapeDtypeStruct((dec_batch, WINDOW, KV_W), F32),
        ],
        scratch_shapes=[
            pltpu.VMEM((rows, ATTN_W), BF),
            pltpu.VMEM((rows, KV_W), F32),
            pltpu.VMEM((rows, KV_W), F32),
            pltpu.VMEM((rows, ATTN_W), F32),
            pltpu.VMEM((rows, D_MODEL), BF),
            pltpu.VMEM((rows + SUBLANES, CONV_W), F32),
        ],
        compiler_params=pltpu.CompilerParams(
            dimension_semantics=("arbitrary",), vmem_limit_bytes=VMEM_LIMIT),
    )(sinks[i], x, p_all, cos, sin_s, s1, s2, g_norm, w_in, conv_w, w_out, w_pg, w_pp, g_final,
      cache_k, cache_v)


def _rope_tables(pos):
    inv = ROPE_THETA ** (-jnp.arange(0, HEAD_DIM, 2, dtype=F32) / HEAD_DIM)
    ang = pos[:, None] * inv[None, :]
    cos = jnp.tile(jnp.cos(ang), (1, 2 * LANES // HEAD_DIM))
    sin = jnp.sin(ang)
    sin_signed = jnp.tile(jnp.concatenate([-sin, sin], axis=-1), (1, LANES // HEAD_DIM))
    return cos, sin_signed


def kernel(x_prompt, x_sample, cache_k, cache_v, state_conv, p_prompt, p_sample, g_norm, w_in, sinks, conv_w,
           w_out, w_pg, w_pp, g_final):
    batch, seq, _ = x_prompt.shape
    dec_batch, t_dec, _ = x_sample.shape
    rows = dec_batch * t_dec

    cos_p, sin_p = _rope_tables(jnp.arange(seq, dtype=F32))
    cos_s, sin_s = _rope_tables(jnp.tile(PAST_LEN +jnp.arange(t_dec, dtype=F32), dec_batch))

    w_in_b, w_out_b, w_pg_b, w_pp_b = (w.astype(BF) for w in (w_in, w_out, w_pg, w_pp))
    g_norm3 = g_norm.reshape(DEPTH, 1, D_MODEL)
    g_fin2 = g_final.reshape(1, D_MODEL)
    ck = cache_k.reshape(DEPTH, dec_batch, WINDOW, KV_W)
    cv = cache_v.reshape(DEPTH, dec_batch, WINDOW, KV_W)
    p_s = p_sample.reshape(DEPTH, rows, PLE_DIM)
    zpad = jnp.zeros((DEPTH, dec_batch, t_dec, CONV_W), F32)
    s1_all = zpad.at[:, :, 0].set(state_conv[:, :, 1]).reshape(DEPTH, rows, CONV_W)
    s2_all = zpad.at[:, :, 0:2].set(state_conv).reshape(DEPTH, rows, CONV_W)

    xp = x_prompt
    xs = x_sample.reshape(rows, D_MODEL)
    nkp, nvp, ncp, nks, nvs, ncs = [], [], [], [], [], []
    for i in range(DEPTH):
        final = i == DEPTH - 1
        xp, kt, vt, ut = _prompt_layer(i, xp, p_prompt, cos_p, sin_p, sinks, g_norm3, w_in_b, conv_w, w_out_b,
                                       w_pg_b, w_pp_b, g_fin2, final)
        nkp.append(kt)
        nvp.append(vt)
        ncp.append(ut)
        xs, us, nk, nv = _sample_layer(i, xs, p_s, cos_s, sin_s, s1_all[i], s2_all[i], sinks, g_norm3, w_in_b,
                                       conv_w, w_out_b, w_pg_b, w_pp_b, g_fin2, ck, cv, t_dec, final)
        nks.append(nk)
        nvs.append(nv)
        ncs.append(us.reshape(dec_batch, t_dec, CONV_W)[:, t_dec - (CONV_K - 1):])

    kv5 = lambda parts, b: jnp.stack(parts).reshape(DEPTH, b, WINDOW, N_KV_HEADS, HEAD_DIM)
    return (xp, xs.reshape(dec_batch, t_dec, D_MODEL), kv5(nkp, batch), kv5(nvp, batch), jnp.stack(ncp),
            kv5(nks, dec_batch), kv5(nvs, dec_batch), jnp.stack(ncs))
```

```python
import functools

import jax
import jax.numpy as jnp
from jax import lax
from jax.experimental import pallas as pl
from jax.experimental.pallas import tpu as pltpu

D_MODEL = 1024
DEPTH = 4
N_HEADS = 8
HEAD_DIM = 64
N_KV_HEADS = 2
GROUP = N_HEADS // N_KV_HEADS
ATTN_W = N_HEADS * HEAD_DIM
KV_W = N_KV_HEADS * HEAD_DIM
CONV_W = D_MODEL - ATTN_W
CONV_K = 3
WINDOW = 128
BLOCK = 128
PLE_DIM = 256
PAST_LEN = 8192
ROPE_THETA = 10000.0
EPS = 1e-6
SCALE = HEAD_DIM ** -0.5
NEG = -1e30
IN_W = 2 * ATTN_W + 2 * KV_W + 4 * CONV_W

C_Q = 0
C_K = ATTN_W
C_V = C_K + KV_W
C_GA = C_V + KV_W
C_BG = C_GA + ATTN_W
C_CG = C_BG + CONV_W
C_HC = C_CG + CONV_W
C_GC = C_HC + CONV_W

LANES = 128
SUBLANES = 8
ROPE_HALF = HEAD_DIM // 2

MXU_N = 256
PROMPT_TILE = 512
SAMPLE_GROUP = 8
VMEM_LIMIT = 56 * 1024 * 1024

BF = jnp.bfloat16
F32 = jnp.float32
_NT = (((1,), (1,)), ((), ()))


def _mm(a, w):
    return jnp.dot(a, w, preferred_element_type=F32)


def _sigmoid(x):
    return 1.0 / (1.0 + jnp.exp(-x))


def _silu(x):
    return x * _sigmoid(x)


def _rmsnorm(x, g):
    r = lax.rsqrt(jnp.mean(x * x, axis=-1, keepdims=True) + EPS)
    return (x * r) * g


def _rope(x, cos, sin_signed):
    rows = x.shape[0]
    lane = lax.broadcasted_iota(jnp.int32, (rows, LANES), 1)
    first_half = (lane % HEAD_DIM) < ROPE_HALF
    outs = []
    for c in range(x.shape[1] // LANES):
        xc = x[:, c * LANES:(c + 1) * LANES]
        rot = jnp.where(first_half, pltpu.roll(xc, LANES - ROPE_HALF, 1), pltpu.roll(xc, ROPE_HALF, 1))
        outs.append(xc * cos + rot * sin_signed)
    return outs[0] if len(outs) == 1 else jnp.concatenate(outs, axis=1)


def _conv_gate(u, u1, u2, cw_ref, bg, gc):
    yc = cw_ref[0:1, :] * u2 + cw_ref[1:2, :] * u1 + cw_ref[2:3, :] * u
    return bg * yc * _silu(gc)


def _output_stage(x, mix, p, wout_ref, wpg_ref, wpp_ref, gfin_ref, final):
    x1 = x + _mm(mix, wout_ref[...])
    gate = _sigmoid(_mm(x1.astype(BF), wpg_ref[...]))
    x2 = x1 + gate * _mm(p.astype(BF), wpp_ref[...])
    if final:
        x2 = _rmsnorm(x2, gfin_ref[...])
    return x2


def _prompt_kernel(sink_ref, x_ref, p_ref, cos_ref, sin_ref, gn_ref, win_ref, cw_ref, wout_ref, wpg_ref,
                   wpp_ref, gfin_ref, xo_ref, kt_ref, vt_ref, ut_ref, kbuf, vbuf, ubuf, mix_buf, xn_buf, zc_buf,
                   pp_buf, *, tile, final):
    s = pl.program_id(1)
    last = pl.num_programs(1) - 1

    @pl.when(s == 0)
    def _():
        kbuf[0:BLOCK, :] = jnp.zeros((BLOCK, KV_W), BF)
        vbuf[0:BLOCK, :] = jnp.zeros((BLOCK, KV_W), BF)
        ubuf[0:SUBLANES, :] = jnp.zeros((SUBLANES, CONV_W), F32)

    pp_buf[...] = _mm(p_ref[0].astype(BF), wpp_ref[...])
    xn_buf[...] = _rmsnorm(x_ref[0], gn_ref[...]).astype(BF)
    cos = cos_ref[...]
    sin_s = sin_ref[...]

    q = (_rope(_mm(xn_buf[...], win_ref[:, C_Q:C_K]), cos, sin_s) * SCALE).astype(BF)
    k = _rope(_mm(xn_buf[...], win_ref[:, C_K:C_V]), cos, sin_s)
    v = _mm(xn_buf[...], win_ref[:, C_V:C_GA])
    kbuf[BLOCK:BLOCK + tile, :] = k.astype(BF)
    vbuf[BLOCK:BLOCK + tile, :] = v.astype(BF)

    kt_ref[0] = k[tile - BLOCK:, :]
    vt_ref[0] = v[tile - BLOCK:, :]

    sga =_silu(_mm(xn_buf[...], win_ref[:, C_GA:C_BG]))

    conv_cols = (C_CG, C_HC, C_BG, C_GC)
    n_slabs = CONV_W // MXU_N

    def dense_unit(n):
        half, plane = divmod(n, len(conv_cols))
        src = conv_cols[plane] + half * MXU_N
        zc_buf[plane, :, half * MXU_N:(half + 1) * MXU_N] = _mm(xn_buf[...], win_ref[:, src:src + MXU_N])

    def conv_unit(half):
        cs = slice(half * MXU_N, (half + 1) * MXU_N)
        u = zc_buf[0, :, cs] * zc_buf[1, :, cs]
        ubuf[SUBLANES:SUBLANES + tile, cs] = u
        u1 = ubuf[SUBLANES - 1:SUBLANES - 1 + tile, cs]
        u2 = ubuf[SUBLANES - 2:SUBLANES - 2 + tile, cs]
        yc = cw_ref[0:1, cs] * u2 + cw_ref[1:2, cs] * u1 + cw_ref[2:3, cs] * u
        mix_buf[:, ATTN_W + half * MXU_N:ATTN_W + (half + 1) * MXU_N] = (
            zc_buf[2, :, cs] * yc * _silu(zc_buf[3, :, cs])).astype(BF)
        ubuf[0:SUBLANES, cs] = ubuf[tile:tile + SUBLANES, cs]
        ut_ref[0, :, cs] = u[tile - (CONV_K - 1):, :]

    row = lax.broadcasted_iota(jnp.int32, (BLOCK, BLOCK), 0)
    col = lax.broadcasted_iota(jnp.int32, (BLOCK, BLOCK), 1)
    own = col <= row
    no_prev = jnp.where(s == 0, NEG, 0.0).astype(F32)

    def scores(n):
        j, g = divmod(n, N_KV_HEADS)
        kcat = kbuf[j * BLOCK:(j + 2) * BLOCK, g * HEAD_DIM:(g + 1) * HEAD_DIM]
        qst = jnp.concatenate(
            [q[j * BLOCK:(j + 1) * BLOCK, (GROUP * g + h) * HEAD_DIM:(GROUP * g + h + 1) * HEAD_DIM]
             for h in range(GROUP)], axis=0)
        return lax.dot_general(qst, kcat, _NT, preferred_element_type=F32)

    def softmax(n, sc):
        j, g = divmod(n, N_KV_HEADS)
        probs, invs = [], []
        for h in range(GROUP):
            sh = sc[h * BLOCK:(h + 1) * BLOCK]
            s_prev = sh[:, :BLOCK]
            if j == 0:
                s_prev = s_prev + no_prev
            sm = jnp.where(own, sh[:, BLOCK:], s_prev)
            sink = sink_ref[GROUP * g + h]
            m = jnp.maximum(jnp.max(sm, axis=-1, keepdims=True), sink)
            e = jnp.exp(sm - m)
            den = jnp.sum(e, axis=-1, keepdims=True) + jnp.exp(sink - m)
            invs.append(1.0 / den)
            eb = e.astype(BF)
            zero = jnp.zeros_like(eb)
            probs.append(jnp.concatenate([jnp.where(own, zero, eb), jnp.where(own, eb, zero)], axis=1))
        return jnp.concatenate(probs, axis=0), invs

    heads = {}

    def values(n, probs, invs):
        j, g = divmod(n, N_KV_HEADS)
        vcat = vbuf[j * BLOCK:(j + 2) * BLOCK, g * HEAD_DIM:(g + 1) * HEAD_DIM]
        o = _mm(probs, vcat)
        heads.setdefault(j, []).extend(o[h * BLOCK:(h + 1) * BLOCK] * invs[h] for h in range(GROUP))
        if g == N_KV_HEADS - 1:
            oa = jnp.concatenate(heads.pop(j), axis=1)
            mix_buf[j * BLOCK:(j + 1) * BLOCK, 0:ATTN_W] = (oa * sga[j * BLOCK:(j + 1) * BLOCK]).astype(BF)

    n_planes = len(conv_cols)
    n_dense = n_slabs * n_planes
    n_units = (tile // BLOCK) * N_KV_HEADS
    conv_done = 0
    sc_q, sm_q = {}, {}
    for t in range(n_units + 2):
        if t < n_units:
            sc_q[t] = scores(t)
        if t < n_dense:
            dense_unit(t)
        if 0 <= t - 2 < n_units:
            values(t - 2, *sm_q.pop(t - 2))
        if 0 <= t - 1 < n_units:
            sm_q[t - 1] = softmax(t - 1, sc_q.pop(t - 1))
        if conv_done < n_slabs and t == (conv_done + 1) * n_planes + 1:
            conv_unit(conv_done)
            conv_done += 1

    kbuf[0:BLOCK, :] = kbuf[tile:tile + BLOCK, :]
    vbuf[0:BLOCK, :] = vbuf[tile:tile + BLOCK, :]

    for n in range(n_units + 2, n_dense):
        dense_unit(n)
    x1 = x_ref[0] + _mm(mix_buf[:, 0:ATTN_W], wout_ref[0:ATTN_W, :])
    for half in range(conv_done, n_slabs):
        conv_unit(half)
    x1 = x1 + _mm(mix_buf[:, ATTN_W:D_MODEL], wout_ref[ATTN_W:D_MODEL, :])
    gate = _sigmoid(_mm(x1.astype(BF), wpg_ref[...]))
    x2 = x1 + gate * pp_buf[...]
    if final:
        x2 = _rmsnorm(x2, gfin_ref[...])
    xo_ref[0] = x2


def _prompt_layer(i, x, p_all, cos, sin_s, sinks, g_norm, w_in, conv_w, w_out, w_pg, w_pp, g_final, final):
    batch, seq, _ = x.shape
    tile = PROMPT_TILE
    const = lambda b, s: (0, 0)
    layer = lambda b, s: (i, 0, 0)
    return pl.pallas_call(
        functools.partial(_prompt_kernel, tile=tile, final=final),
        name=f"prompt_layer{i}",
        grid=(batch, seq // tile),
        in_specs=[
            pl.BlockSpec(memory_space=pltpu.SMEM),
            pl.BlockSpec((1, tile, D_MODEL), lambda b, s: (b, s, 0)),
            pl.BlockSpec((None, 1, tile, PLE_DIM), lambda b, s: (i, b, s, 0)),
            pl.BlockSpec((tile, LANES), lambda b, s: (s, 0)),
            pl.BlockSpec((tile, LANES), lambda b, s: (s, 0)),
            pl.BlockSpec((None, 1, D_MODEL), layer),
            pl.BlockSpec((None, D_MODEL, IN_W), layer),
            pl.BlockSpec((None, CONV_K, CONV_W), layer),
            pl.BlockSpec((None, D_MODEL, D_MODEL), layer),
            pl.BlockSpec((None, D_MODEL, D_MODEL), layer),
            pl.BlockSpec((None, PLE_DIM, D_MODEL), layer),
            pl.BlockSpec((1, D_MODEL), const),
        ],
        out_specs=[
            pl.BlockSpec((1, tile, D_MODEL), lambda b, s: (b, s, 0)),
            pl.BlockSpec((1, BLOCK, KV_W), lambda b, s: (b, 0, 0)),
            pl.BlockSpec((1, BLOCK, KV_W), lambda b, s: (b, 0, 0)),
            pl.BlockSpec((1, CONV_K - 1, CONV_W), lambda b, s: (b, 0, 0)),
        ],
        out_shape=[
            jax.ShapeDtypeStruct((batch, seq, D_MODEL), F32),
            jax.ShapeDtypeStruct((batch, BLOCK, KV_W), F32),
            jax.ShapeDtypeStruct((batch, BLOCK, KV_W), F32),
            jax.ShapeDtypeStruct((batch, CONV_K - 1, CONV_W), F32),
        ],
        scratch_shapes=[
            pltpu.VMEM((tile + BLOCK, KV_W), BF),
            pltpu.VMEM((tile + BLOCK, KV_W), BF),
            pltpu.VMEM((tile + SUBLANES, CONV_W), F32),
            pltpu.VMEM((tile, D_MODEL), BF),
            pltpu.VMEM((tile, D_MODEL), BF),
            pltpu.VMEM((4, tile, CONV_W), F32),
            pltpu.VMEM((tile, D_MODEL), F32),
        ],
        compiler_params=pltpu.CompilerParams(
            dimension_semantics=("arbitrary", "arbitrary"), vmem_limit_bytes=VMEM_LIMIT),
    )(sinks[i], x, p_all, cos, sin_s, g_norm, w_in, conv_w, w_out, w_pg, w_pp, g_final)


def _sample_kernel(sink_ref, x_ref, p_ref, cos_ref, sin_ref, s1_ref, s2_ref, gn_ref, win_ref, cw_ref, wout_ref,
                   wpg_ref, wpp_ref, gfin_ref, ck_ref, cv_ref, xo_ref, u_ref, nk_ref, nv_ref,
                   q_s, k_s, v_s, sga_s, mix_s, ubuf, *, rows, t_dec, nb, final):
    grp = pl.program_id(0)
    last = pl.num_programs(0) - 1
    gr = nb * t_dec

    @pl.when(grp == 0)
    def _():
        xn = _rmsnorm(x_ref[...], gn_ref[...]).astype(BF)
        cos = cos_ref[...]
        sin_s = sin_ref[...]
        q_s[...] = (_rope(_mm(xn, win_ref[:, C_Q:C_K]), cos, sin_s) * SCALE).astype(BF)
        k_s[...] = _rope(_mm(xn, win_ref[:, C_K:C_V]), cos, sin_s)
        v_s[...] = _mm(xn, win_ref[:, C_V:C_GA])
        sga_s[...] = _silu(_mm(xn, win_ref[:, C_GA:C_BG]))
        u = _mm(xn, win_ref[:, C_CG:C_HC]) * _mm(xn, win_ref[:, C_HC:C_GC])
        u_ref[...] = u
        ubuf[0:SUBLANES, :] = jnp.zeros((SUBLANES, CONV_W), F32)
        ubuf[SUBLANES:SUBLANES + rows, :] = u
        t = lax.broadcasted_iota(jnp.int32, (rows, CONV_W), 0) % t_dec
        u1 = jnp.where(t >= 1, ubuf[SUBLANES - 1:SUBLANES - 1 + rows, :], 0.0) + s1_ref[...]
        u2 = jnp.where(t >= 2, ubuf[SUBLANES - 2:SUBLANES - 2 + rows, :], 0.0) + s2_ref[...]
        bg = _mm(xn, win_ref[:, C_BG:C_CG])
        gc = _mm(xn, win_ref[:, C_GC:IN_W])
        mix_s[:, ATTN_W:D_MODEL] = _conv_gate(u, u1, u2, cw_ref, bg, gc).astype(BF)

    r0 = pl.multiple_of(grp * gr, gr)
    qg = q_s[pl.ds(r0, gr), :]
    kn = k_s[pl.ds(r0, gr), :]
    vn = v_s[pl.ds(r0, gr), :]
    kc = ck_ref[...].reshape(nb * WINDOW, KV_W)
    vc = cv_ref[...].reshape(nb * WINDOW, KV_W)

    qrow = lax.broadcasted_iota(jnp.int32, (gr, nb * WINDOW), 0)
    ccol = lax.broadcasted_iota(jnp.int32, (gr, nb * WINDOW), 1)
    valid_c = ((ccol // WINDOW) == (qrow // t_dec)) & ((ccol % WINDOW) > (qrow % t_dec))
    qrow_n = lax.broadcasted_iota(jnp.int32, (gr, gr), 0)
    ncol = lax.broadcasted_iota(jnp.int32, (gr, gr), 1)
    valid_n = ((ncol // t_dec) == (qrow_n // t_dec)) & ((ncol % t_dec) <= (qrow_n % t_dec))

    heads = []
    for g in range(N_KV_HEADS):
        lo, hi = g * HEAD_DIM, (g + 1) * HEAD_DIM
        kcg = kc[:, lo:hi].astype(BF)
        vcg = vc[:, lo:hi].astype(BF)
        kng = kn[:, lo:hi].astype(BF)
        vng = vn[:, lo:hi].astype(BF)
        qst = jnp.concatenate(
            [qg[:, (GROUP * g + h) * HEAD_DIM:(GROUP * g + h + 1) * HEAD_DIM] for h in range(GROUP)], axis=0)
        sc = lax.dot_general(qst, kcg, _NT, preferred_element_type=F32)
        sn = lax.dot_general(qst, kng, _NT, preferred_element_type=F32)
        ecs, ens, invs = [], [], []
        for h in range(GROUP):
            scm = jnp.where(valid_c, sc[h * gr:(h + 1) * gr], NEG)
            snm = jnp.where(valid_n, sn[h * gr:(h + 1) * gr], NEG)
            sink = sink_ref[GROUP * g + h]
            m = jnp.maximum(jnp.maximum(jnp.max(scm, axis=-1, keepdims=True),
                                        jnp.max(snm, axis=-1, keepdims=True)), sink)
            ec = jnp.exp(scm - m)
            en = jnp.exp(snm - m)
            den = (jnp.sum(ec, axis=-1, keepdims=True) + jnp.sum(en, axis=-1, keepdims=True)
                   + jnp.exp(sink - m))
            invs.append(1.0 / den)
            ecs.append(ec.astype(BF))
            ens.append(en.astype(BF))
        o = _mm(jnp.concatenate(ecs, axis=0), vcg) + _mm(jnp.concatenate(ens, axis=0), vng)
        for h in range(GROUP):
            heads.append(o[h * gr:(h + 1) * gr] * invs[h])
    oa = jnp.concatenate(heads, axis=1)
    mix_s[pl.ds(r0, gr), 0:ATTN_W] = (oa * sga_s[pl.ds(r0, gr), :]).astype(BF)

    nk_ref[:, 0:WINDOW - t_dec, :] = ck_ref[:, t_dec:WINDOW, :]
    nv_ref[:, 0:WINDOW - t_dec, :] = cv_ref[:, t_dec:WINDOW, :]
    for b in range(nb):
        nk_ref[b, WINDOW - t_dec:WINDOW, :] = kn[b * t_dec:(b + 1) * t_dec, :]
        nv_ref[b, WINDOW - t_dec:WINDOW, :] = vn[b * t_dec:(b + 1) * t_dec, :]

    @pl.when(grp == last)
    def _():
        xo_ref[...] = _output_stage(x_ref[...], mix_s[...], p_ref[...], wout_ref, wpg_ref, wpp_ref, gfin_ref,
                                    final)


def _sample_layer(i, x, p_all, cos, sin_s, s1, s2, sinks, g_norm, w_in, conv_w, w_out, w_pg, w_pp, g_final,
                  cache_k, cache_v, t_dec, final):
    rows = x.shape[0]
    dec_batch = rows // t_dec
    nb = SAMPLE_GROUP
    const = lambda g: (0, 0)
    layer = lambda g: (i, 0, 0)
    full = lambda shape: pl.BlockSpec(shape, const)
    return pl.pallas_call(
        functools.partial(_sample_kernel, rows=rows, t_dec=t_dec, nb=nb, final=final),
        name=f"sample_layer{i}",
        grid=(dec_batch // nb,),
        in_specs=[
            pl.BlockSpec(memory_space=pltpu.SMEM),
            full((rows, D_MODEL)),
            pl.BlockSpec((None, rows, PLE_DIM), layer),
            full((rows, LANES)),
            full((rows, LANES)),
            full((rows, CONV_W)),
            full((rows, CONV_W)),
            pl.BlockSpec((None, 1, D_MODEL), layer),
            pl.BlockSpec((None, D_MODEL, IN_W), layer),
            pl.BlockSpec((None, CONV_K, CONV_W), layer),
            pl.BlockSpec((None, D_MODEL, D_MODEL), layer),
            pl.BlockSpec((None, D_MODEL, D_MODEL), layer),
            pl.BlockSpec((None, PLE_DIM, D_MODEL), layer),
            full((1, D_MODEL)),
            pl.BlockSpec((None, nb, WINDOW, KV_W), lambda g: (i, g, 0, 0)),
            pl.BlockSpec((None, nb, WINDOW, KV_W), lambda g: (i, g, 0, 0)),
        ],
        out_specs=[
            full((rows, D_MODEL)),
            full((rows, CONV_W)),
            pl.BlockSpec((nb, WINDOW, KV_W), lambda g: (g, 0, 0)),
            pl.BlockSpec((nb, WINDOW, KV_W), lambda g: (g, 0, 0)),
        ],
        out_shape=[
            jax.ShapeDtypeStruct((rows, D_MODEL), F32),
            jax.ShapeDtypeStruct((rows, CONV_W), F32),
            jax.ShapeDtypeStruct((dec_batch, WINDOW, KV_W), F32),
            jax.ShapeDtypeStruct((dec_batch, WINDOW, KV_W), F32),
        ],
        scratch_shapes=[
            pltpu.VMEM((rows, ATTN_W), BF),
            pltpu.VMEM((rows, KV_W), F32),
            pltpu.VMEM((rows, KV_W), F32),
            pltpu.VMEM((rows, ATTN_W), F32),
            pltpu.VMEM((rows, D_MODEL), BF),
            pltpu.VMEM((rows + SUBLANES, CONV_W), F32),
        ],
        compiler_params=pltpu.CompilerParams(
            dimension_semantics=("arbitrary",), vmem_limit_bytes=VMEM_LIMIT),
    )(sinks[i], x, p_all, cos, sin_s, s1, s2, g_norm, w_in, conv_w, w_out, w_pg, w_pp, g_final,
      cache_k, cache_v)


def _rope_tables(pos):
    inv = ROPE_THETA ** (-jnp.arange(0, HEAD_DIM, 2, dtype=F32) / HEAD_DIM)
    ang = pos[:, None] * inv[None, :]
    cos = jnp.tile(jnp.cos(ang), (1, 2 * LANES // HEAD_DIM))
    sin = jnp.sin(ang)
    sin_signed = jnp.tile(jnp.concatenate([-sin, sin], axis=-1), (1, LANES // HEAD_DIM))
    return cos, sin_signed


def kernel(x_prompt, x_sample, cache_k, cache_v, state_conv, p_prompt, p_sample, g_norm, w_in, sinks, conv_w,
           w_out, w_pg, w_pp, g_final):
    batch, seq, _ = x_prompt.shape
    dec_batch, t_dec, _ = x_sample.shape
    rows = dec_batch * t_dec

    cos_p, sin_p = _rope_tables(jnp.arange(seq, dtype=F32))
    cos_s, sin_s = _rope_tables(jnp.tile(PAST_LEN +jnp.arange(t_dec, dtype=F32), dec_batch))

    w_in_b, w_out_b, w_pg_b, w_pp_b = (w.astype(BF) for w in (w_in, w_out, w_pg, w_pp))
    g_norm3 = g_norm.reshape(DEPTH, 1, D_MODEL)
    g_fin2 = g_final.reshape(1, D_MODEL)
    ck = cache_k.reshape(DEPTH, dec_batch, WINDOW, KV_W)
    cv = cache_v.reshape(DEPTH, dec_batch, WINDOW, KV_W)
    p_s = p_sample.reshape(DEPTH, rows, PLE_DIM)
    zpad = jnp.zeros((DEPTH, dec_batch, t_dec, CONV_W), F32)
    s1_all = zpad.at[:, :, 0].set(state_conv[:, :, 1]).reshape(DEPTH, rows, CONV_W)
    s2_all = zpad.at[:, :, 0:2].set(state_conv).reshape(DEPTH, rows, CONV_W)

    xp = x_prompt
    xs = x_sample.reshape(rows, D_MODEL)
    nkp, nvp, ncp, nks, nvs, ncs = [], [], [], [], [], []
    for i in range(DEPTH):
        final = i == DEPTH - 1
        xp, kt, vt, ut = _prompt_layer(i, xp, p_prompt, cos_p, sin_p, sinks, g_norm3, w_in_b, conv_w, w_out_b,
                                       w_pg_b, w_pp_b, g_fin2, final)
        nkp.append(kt)
        nvp.append(vt)
        ncp.append(ut)
        xs, us, nk, nv = _sample_layer(i, xs, p_s, cos_s, sin_s, s1_all[i], s2_all[i], sinks, g_norm3, w_in_b,
                                       conv_w, w_out_b, w_pg_b, w_pp_b, g_fin2, ck, cv, t_dec, final)
        nks.append(nk)
        nvs.append(nv)
        ncs.append(us.reshape(dec_batch, t_dec, CONV_W)[:, t_dec - (CONV_K - 1):])

    kv5 = lambda parts, b: jnp.stack(parts).reshape(DEPTH, b, WINDOW, N_KV_HEADS, HEAD_DIM)
    return (xp, xs.reshape(dec_batch, t_dec, D_MODEL), kv5(nkp, batch), kv5(nvp, batch), jnp.stack(ncp),
            kv5(nks, dec_batch), kv5(nvs, dec_batch), jnp.stack(ncs))
```

```python
import functools

import jax
import jax.numpy as jnp
from jax import lax
from jax.experimental import pallas as pl
from jax.experimental.pallas import tpu as pltpu

D_MODEL = 1024
DEPTH = 4
N_HEADS = 8
HEAD_DIM = 64
N_KV_HEADS = 2
GROUP = N_HEADS // N_KV_HEADS
ATTN_W = N_HEADS * HEAD_DIM
KV_W = N_KV_HEADS * HEAD_DIM
CONV_W = D_MODEL - ATTN_W
CONV_K = 3
WINDOW = 128
BLOCK = 128
PLE_DIM = 256
PAST_LEN = 8192
ROPE_THETA = 10000.0
EPS = 1e-6
SCALE = HEAD_DIM ** -0.5
NEG = -1e30
IN_W = 2 * ATTN_W + 2 * KV_W + 4 * CONV_W

C_Q = 0
C_K = ATTN_W
C_V = C_K + KV_W
C_GA = C_V + KV_W
C_BG = C_GA + ATTN_W
C_CG = C_BG + CONV_W
C_HC = C_CG + CONV_W
C_GC = C_HC + CONV_W

LANES = 128
SUBLANES = 8
ROPE_HALF = HEAD_DIM // 2

MXU_N = 256
PROMPT_TILE = 512
SAMPLE_GROUP = 8
VMEM_LIMIT = 56 * 1024 * 1024

BF = jnp.bfloat16
F32 = jnp.float32
_NT = (((1,), (1,)), ((), ()))


def _mm(a, w):
    return jnp.dot(a, w, preferred_element_type=F32)


def _sigmoid(x):
    return 1.0 / (1.0 + jnp.exp(-x))


def _silu(x):
    return x * _sigmoid(x)


def _rmsnorm(x, g):
    r = lax.rsqrt(jnp.mean(x * x, axis=-1, keepdims=True) + EPS)
    return (x * r) * g


def _rope(x, cos, sin_signed):
    rows = x.shape[0]
    lane = lax.broadcasted_iota(jnp.int32, (rows, LANES), 1)
    first_half = (lane % HEAD_DIM) < ROPE_HALF
    outs = []
    for c in range(x.shape[1] // LANES):
        xc = x[:, c * LANES:(c + 1) * LANES]
        rot = jnp.where(first_half, pltpu.roll(xc, LANES - ROPE_HALF, 1), pltpu.roll(xc, ROPE_HALF, 1))
        outs.append(xc * cos + rot * sin_signed)
    return outs[0] if len(outs) == 1 else jnp.concatenate(outs, axis=1)


def _conv_gate(u, u1, u2, cw_ref, bg, gc):
    yc = cw_ref[0:1, :] * u2 + cw_ref[1:2, :] * u1 + cw_ref[2:3, :] * u
    return bg * yc * _silu(gc)


def _output_stage(x, mix, p, wout_ref, wpg_ref, wpp_ref, gfin_ref, final):
    x1 = x + _mm(mix, wout_ref[...])
    gate = _sigmoid(_mm(x1.astype(BF), wpg_ref[...]))
    x2 = x1 + gate * _mm(p.astype(BF), wpp_ref[...])
    if final:
        x2 = _rmsnorm(x2, gfin_ref[...])
    return x2


def _prompt_kernel(sink_ref, xa_ref, xb_ref, p_ref, cos_ref, sin_ref, gn_ref, win_ref, cw_ref, wout_ref, wpg_ref,
                   wpp_ref, gfin_ref, xo_ref, kt_ref, vt_ref, ut_ref, kbuf, vbuf, ubuf, mix_buf, xn_buf, zc_buf,
                   q_buf, sga_buf, x1_buf, x1b_buf, *, tile, n_seq, final):
    i = pl.program_id(0)
    n_tiles = pl.num_programs(0) - 1
    first_of_seq = jnp.minimum(i, n_tiles - 1) % n_seq == 0

    @pl.when(i == 0)
    def _():
        mix_buf[...] = jnp.zeros(mix_buf.shape, BF)

    @pl.when(first_of_seq)
    def _():
        kbuf[0:BLOCK, :] = jnp.zeros((BLOCK, KV_W), BF)
        vbuf[0:BLOCK, :] = jnp.zeros((BLOCK, KV_W), BF)
        ubuf[0:SUBLANES, :] = jnp.zeros((SUBLANES, CONV_W), F32)

    n_out = D_MODEL // MXU_N
    rows_per = tile // n_out
    for c in range(n_out):
        cs = slice(c * MXU_N, (c + 1) * MXU_N)
        x1 = (xb_ref[0, :, cs] + _mm(mix_buf[:, 0:ATTN_W], wout_ref[0:ATTN_W, cs])
              + _mm(mix_buf[:, ATTN_W:D_MODEL], wout_ref[ATTN_W:D_MODEL, cs]))
        x1_buf[:, cs] = x1
        x1b_buf[:, cs] = x1.astype(BF)
        rs = slice(c * rows_per, (c + 1) * rows_per)
        xn_buf[rs, :] = _rmsnorm(xa_ref[0, rs, :], gn_ref[...]).astype(BF)
    pb = p_ref[0].astype(BF)

    def gate_unit(c):
        cs = slice(c * MXU_N, (c + 1) * MXU_N)
        gate = _sigmoid(_mm(x1b_buf[...], wpg_ref[:, cs]))
        xo_ref[0, :, cs] = x1_buf[:, cs] + gate * _mm(pb, wpp_ref[:, cs])

    cos = cos_ref[...]
    sin_s = sin_ref[...]

    def proj(c0, width=MXU_N):
        return _mm(xn_buf[...], win_ref[:, c0:c0 + width])

    kv = proj(C_K, 2 * KV_W)
    k = _rope(kv[:, 0:KV_W], cos, sin_s)
    kbuf[BLOCK:BLOCK + tile, :] = k.astype(BF)
    v = kv[:, KV_W:2 * KV_W]
    vbuf[BLOCK:BLOCK + tile, :] = v.astype(BF)
    kt_ref[0] = k[tile - BLOCK:, :]
    vt_ref[0] = v[tile - BLOCK:, :]
    for c in range(ATTN_W // MXU_N):
        cs = slice(c * MXU_N, (c + 1) * MXU_N)
        q_buf[:, cs] = (_rope(proj(C_Q + c * MXU_N), cos, sin_s) * SCALE).astype(BF)

    conv_cols = (C_CG, C_HC, C_BG, C_GC)
    n_planes = len(conv_cols)
    n_slabs = CONV_W // MXU_N
    n_gate = ATTN_W // MXU_N
    n_dense = n_gate + n_slabs * n_planes

    def dense_unit(n):
        if n < n_gate:
            sga_buf[:, n * MXU_N:(n + 1) * MXU_N] = _silu(proj(C_GA + n * MXU_N))
        else:
            half, plane = divmod(n - n_gate, n_planes)
            zc_buf[plane, :, half * MXU_N:(half + 1) * MXU_N] = proj(conv_cols[plane] + half * MXU_N)

    def conv_unit(half):
        cs = slice(half * MXU_N, (half + 1) * MXU_N)
        u = zc_buf[0, :, cs] * zc_buf[1, :, cs]
        ubuf[SUBLANES:SUBLANES + tile, cs] = u
        u1 = ubuf[SUBLANES - 1:SUBLANES - 1 + tile, cs]
        u2 = ubuf[SUBLANES - 2:SUBLANES - 2 + tile, cs]
        yc = cw_ref[0:1, cs] * u2 + cw_ref[1:2, cs] * u1 + cw_ref[2:3, cs] * u
        mix_buf[:, ATTN_W + half * MXU_N:ATTN_W + (half + 1) * MXU_N] = (
            zc_buf[2, :, cs] * yc * _silu(zc_buf[3, :, cs])).astype(BF)
        ubuf[0:SUBLANES, cs] = ubuf[tile:tile + SUBLANES, cs]
        ut_ref[0, :, cs] = u[tile - (CONV_K - 1):, :]

    row = lax.broadcasted_iota(jnp.int32, (BLOCK, BLOCK), 0)
    col = lax.broadcasted_iota(jnp.int32, (BLOCK, BLOCK), 1)
    own = col <= row
    no_prev = jnp.where(first_of_seq, NEG, 0.0).astype(F32)

    def scores(n):
        j, g = divmod(n, N_KV_HEADS)
        kcat = kbuf[j * BLOCK:(j + 2) * BLOCK, g * HEAD_DIM:(g + 1) * HEAD_DIM]
        qst = jnp.concatenate(
            [q_buf[j * BLOCK:(j + 1) * BLOCK, (GROUP * g + h) * HEAD_DIM:(GROUP * g + h + 1) * HEAD_DIM]
             for h in range(GROUP)], axis=0)
        return lax.dot_general(qst, kcat, _NT, preferred_element_type=F32)

    def softmax(n, sc):
        j, g = divmod(n, N_KV_HEADS)
        probs, invs = [], []
        for h in range(GROUP):
            sh = sc[h * BLOCK:(h + 1) * BLOCK]
            s_prev = sh[:, :BLOCK]
            if j == 0:
                s_prev = s_prev + no_prev
            sm = jnp.where(own, sh[:, BLOCK:], s_prev)
            sink = sink_ref[GROUP * g + h]
            m = jnp.maximum(jnp.max(sm, axis=-1, keepdims=True), sink)
            e = jnp.exp(sm - m)
            den = jnp.sum(e, axis=-1, keepdims=True) + jnp.exp(sink - m)
            invs.append(1.0 / den)
            eb = e.astype(BF)
            zero = jnp.zeros_like(eb)
            probs.append(jnp.concatenate([jnp.where(own, zero, eb), jnp.where(own, eb, zero)], axis=1))
        return jnp.concatenate(probs, axis=0), invs

    heads = {}

    def values(n, probs, invs):
        j, g = divmod(n, N_KV_HEADS)
        vcat = vbuf[j * BLOCK:(j + 2) * BLOCK, g * HEAD_DIM:(g + 1) * HEAD_DIM]
        o = _mm(probs, vcat)
        heads.setdefault(j, []).extend(o[h * BLOCK:(h + 1) * BLOCK] * invs[h] for h in range(GROUP))
        if g == N_KV_HEADS - 1:
            rows = slice(j * BLOCK, (j + 1) * BLOCK)
            oa = jnp.concatenate(heads.pop(j), axis=1)
            mix_buf[rows, 0:ATTN_W] = (oa * sga_buf[rows, :]).astype(BF)

    n_units = (tile // BLOCK) * N_KV_HEADS
    conv_at = {n_gate + (h + 1) * n_planes: h for h in range(n_slabs)}
    n_steps = max(n_units + 2, max(conv_at) + 1)
    gate_at = {2 * c + 1: c for c in range(n_out)}
    sc_q, sm_q = {}, {}
    for t in range(n_steps):
        if t < n_units:
            sc_q[t] = scores(t)
        if t < n_dense:
            dense_unit(t)
        if 0 <= t - 2 < n_units:
            values(t - 2, *sm_q.pop(t - 2))
        if t in gate_at:
            gate_unit(gate_at[t])
        if 0 <= t - 1 < n_units:
            sm_q[t - 1] = softmax(t - 1, sc_q.pop(t - 1))
        if t in conv_at:
            conv_unit(conv_at[t])

    kbuf[0:BLOCK, :] = kbuf[tile:tile + BLOCK, :]
    vbuf[0:BLOCK, :] = vbuf[tile:tile + BLOCK, :]

    if final:
        xo_ref[0] = _rmsnorm(xo_ref[0], gfin_ref[...])


def _prompt_layer(i, x, p_all, cos, sin_s, sinks, g_norm, w_in, conv_w, w_out, w_pg, w_pp, g_final, final):
    batch, seq, _ = x.shape
    tile = PROMPT_TILE
    n_seq = seq // tile
    n_tiles = batch * n_seq
    in_tile = lambda t: jnp.minimum(t, n_tiles - 1)
    out_tile = lambda t: jnp.maximum(t - 1, 0)
    const = lambda t: (0, 0)
    layer = lambda t: (i, 0, 0)
    resident = dict(pipeline_mode=pl.Buffered(1))
    return pl.pallas_call(
        functools.partial(_prompt_kernel, tile=tile, n_seq=n_seq, final=final),
        name=f"prompt_layer{i}",
        grid=(n_tiles + 1,),
        in_specs=[
            pl.BlockSpec(memory_space=pltpu.SMEM),
            pl.BlockSpec((1, tile, D_MODEL), lambda t: (in_tile(t) // n_seq, in_tile(t) % n_seq, 0)),
            pl.BlockSpec((1, tile, D_MODEL), lambda t: (out_tile(t) // n_seq, out_tile(t) % n_seq, 0)),
            pl.BlockSpec((None, 1, tile, PLE_DIM),
                         lambda t: (i, out_tile(t) // n_seq, out_tile(t) % n_seq, 0)),
            pl.BlockSpec((tile, LANES), lambda t: (in_tile(t) % n_seq, 0)),
            pl.BlockSpec((tile, LANES), lambda t: (in_tile(t) % n_seq, 0)),
            pl.BlockSpec((None, 1, D_MODEL), layer, **resident),
            pl.BlockSpec((None, D_MODEL, IN_W), layer, **resident),
            pl.BlockSpec((None, CONV_K, CONV_W), layer, **resident),
            pl.BlockSpec((None, D_MODEL, D_MODEL), layer, **resident),
            pl.BlockSpec((None, D_MODEL, D_MODEL), layer, **resident),
            pl.BlockSpec((None, PLE_DIM, D_MODEL), layer, **resident),
            pl.BlockSpec((1, D_MODEL), const, **resident),
        ],
        out_specs=[
            pl.BlockSpec((1, tile, D_MODEL), lambda t: (out_tile(t) // n_seq, out_tile(t) % n_seq, 0)),
            pl.BlockSpec((1, BLOCK, KV_W), lambda t: (in_tile(t) // n_seq, 0, 0)),
            pl.BlockSpec((1, BLOCK, KV_W), lambda t: (in_tile(t) // n_seq, 0, 0)),
            pl.BlockSpec((1, CONV_K - 1, CONV_W), lambda t: (in_tile(t) // n_seq, 0, 0)),
        ],
        out_shape=[
            jax.ShapeDtypeStruct((batch, seq, D_MODEL), F32),
            jax.ShapeDtypeStruct((batch, BLOCK, KV_W), F32),
            jax.ShapeDtypeStruct((batch, BLOCK, KV_W), F32),
            jax.ShapeDtypeStruct((batch, CONV_K - 1, CONV_W), F32),
        ],
        scratch_shapes=[
            pltpu.VMEM((tile + BLOCK, KV_W), BF),
            pltpu.VMEM((tile + BLOCK, KV_W), BF),
            pltpu.VMEM((tile + SUBLANES, CONV_W), F32),
            pltpu.VMEM((tile, D_MODEL), BF),
            pltpu.VMEM((tile, D_MODEL), BF),
            pltpu.VMEM((4, tile, CONV_W), F32),
            pltpu.VMEM((tile, ATTN_W), BF),
            pltpu.VMEM((tile, ATTN_W), F32),
            pltpu.VMEM((tile, D_MODEL), F32),
            pltpu.VMEM((tile, D_MODEL), BF),
        ],
        compiler_params=pltpu.CompilerParams(dimension_semantics=("arbitrary",), vmem_limit_bytes=VMEM_LIMIT),
    )(sinks[i], x, x, p_all, cos, sin_s, g_norm, w_in, conv_w, w_out, w_pg, w_pp, g_final)


def _sample_kernel(sink_ref, x_ref, p_ref, cos_ref, sin_ref, s1_ref, s2_ref, gn_ref, win_ref, cw_ref, wout_ref,
                   wpg_ref, wpp_ref, gfin_ref, ck_ref, cv_ref, xo_ref, u_ref, nk_ref, nv_ref,
                   q_s, k_s, v_s, sga_s, mix_s, ubuf, *, rows, t_dec, nb, final):
    grp = pl.program_id(0)
    last = pl.num_programs(0) - 1
    gr = nb * t_dec

    @pl.when(grp == 0)
    def _():
        xn = _rmsnorm(x_ref[...], gn_ref[...]).astype(BF)
        cos = cos_ref[...]
        sin_s = sin_ref[...]
        q_s[...] = (_rope(_mm(xn, win_ref[:, C_Q:C_K]), cos, sin_s) * SCALE).astype(BF)
        k_s[...] = _rope(_mm(xn, win_ref[:, C_K:C_V]), cos, sin_s)
        v_s[...] = _mm(xn, win_ref[:, C_V:C_GA])
        sga_s[...] = _silu(_mm(xn, win_ref[:, C_GA:C_BG]))
        u = _mm(xn, win_ref[:, C_CG:C_HC]) * _mm(xn, win_ref[:, C_HC:C_GC])
        u_ref[...] = u
        ubuf[0:SUBLANES, :] = jnp.zeros((SUBLANES, CONV_W), F32)
        ubuf[SUBLANES:SUBLANES + rows, :] = u
        t = lax.broadcasted_iota(jnp.int32, (rows, CONV_W), 0) % t_dec
        u1 = jnp.where(t >= 1, ubuf[SUBLANES - 1:SUBLANES - 1 + rows, :], 0.0) + s1_ref[...]
        u2 = jnp.where(t >= 2, ubuf[SUBLANES - 2:SUBLANES - 2 + rows, :], 0.0) + s2_ref[...]
        bg = _mm(xn, win_ref[:, C_BG:C_CG])
        gc = _mm(xn, win_ref[:, C_GC:IN_W])
        mix_s[:, ATTN_W:D_MODEL] = _conv_gate(u, u1, u2, cw_ref, bg, gc).astype(BF)

    r0 = pl.multiple_of(grp * gr, gr)
    qg = q_s[pl.ds(r0, gr), :]
    kn = k_s[pl.ds(r0, gr), :]
    vn = v_s[pl.ds(r0, gr), :]
    kc = ck_ref[...].reshape(nb * WINDOW, KV_W)
    vc = cv_ref[...].reshape(nb * WINDOW, KV_W)

    qrow = lax.broadcasted_iota(jnp.int32, (gr, nb * WINDOW), 0)
    ccol = lax.broadcasted_iota(jnp.int32, (gr, nb * WINDOW), 1)
    valid_c = ((ccol // WINDOW) == (qrow // t_dec)) & ((ccol % WINDOW) > (qrow % t_dec))
    qrow_n = lax.broadcasted_iota(jnp.int32, (gr, gr), 0)
    ncol = lax.broadcasted_iota(jnp.int32, (gr, gr), 1)
    valid_n = ((ncol // t_dec) == (qrow_n // t_dec)) & ((ncol % t_dec) <= (qrow_n % t_dec))

    heads = []
    for g in range(N_KV_HEADS):
        lo, hi = g * HEAD_DIM, (g + 1) * HEAD_DIM
        kcg = kc[:, lo:hi].astype(BF)
        vcg = vc[:, lo:hi].astype(BF)
        kng = kn[:, lo:hi].astype(BF)
        vng = vn[:, lo:hi].astype(BF)
        qst = jnp.concatenate(
            [qg[:, (GROUP * g + h) * HEAD_DIM:(GROUP * g + h + 1) * HEAD_DIM] for h in range(GROUP)], axis=0)
        sc = lax.dot_general(qst, kcg, _NT, preferred_element_type=F32)
        sn = lax.dot_general(qst, kng, _NT, preferred_element_type=F32)
        ecs, ens, invs = [], [], []
        for h in range(GROUP):
            scm = jnp.where(valid_c, sc[h * gr:(h + 1) * gr], NEG)
            snm = jnp.where(valid_n, sn[h * gr:(h + 1) * gr], NEG)
            sink = sink_ref[GROUP * g + h]
            m = jnp.maximum(jnp.maximum(jnp.max(scm, axis=-1, keepdims=True),
                                        jnp.max(snm, axis=-1, keepdims=True)), sink)
            ec = jnp.exp(scm - m)
            en = jnp.exp(snm - m)
            den = (jnp.sum(ec, axis=-1, keepdims=True) + jnp.sum(en, axis=-1, keepdims=True)
                   + jnp.exp(sink - m))
            invs.append(1.0 / den)
            ecs.append(ec.astype(BF))
            ens.append(en.astype(BF))
        o = _mm(jnp.concatenate(ecs, axis=0), vcg) + _mm(jnp.concatenate(ens, axis=0), vng)
        for h in range(GROUP):
            heads.append(o[h * gr:(h + 1) * gr] * invs[h])
    oa = jnp.concatenate(heads, axis=1)
    mix_s[pl.ds(r0, gr), 0:ATTN_W] = (oa * sga_s[pl.ds(r0, gr), :]).astype(BF)

    nk_ref[:, 0:WINDOW - t_dec, :] = ck_ref[:, t_dec:WINDOW, :]
    nv_ref[:, 0:WINDOW - t_dec, :] = cv_ref[:, t_dec:WINDOW, :]
    for b in range(nb):
        nk_ref[b, WINDOW - t_dec:WINDOW, :] = kn[b * t_dec:(b + 1) * t_dec, :]
        nv_ref[b, WINDOW - t_dec:WINDOW, :] = vn[b * t_dec:(b + 1) * t_dec, :]

    @pl.when(grp == last)
    def _():
        xo_ref[...] = _output_stage(x_ref[...], mix_s[...], p_ref[...], wout_ref, wpg_ref, wpp_ref, gfin_ref,
                                    final)


def _sample_layer(i, x, p_all, cos, sin_s, s1, s2, sinks, g_norm, w_in, conv_w, w_out, w_pg, w_pp, g_final,
                  cache_k, cache_v, t_dec, final):
    rows = x.shape[0]
    dec_batch = rows // t_dec
    nb = SAMPLE_GROUP
    const = lambda g: (0, 0)
    layer = lambda g: (i, 0, 0)
    full = lambda shape: pl.BlockSpec(shape, const)
    return pl.pallas_call(
        functools.partial(_sample_kernel, rows=rows, t_dec=t_dec, nb=nb, final=final),
        name=f"sample_layer{i}",
        grid=(dec_batch // nb,),
        in_specs=[
            pl.BlockSpec(memory_space=pltpu.SMEM),
            full((rows, D_MODEL)),
            pl.BlockSpec((None, rows, PLE_DIM), layer),
            full((rows, LANES)),
            full((rows, LANES)),
            full((rows, CONV_W)),
            full((rows, CONV_W)),
            pl.BlockSpec((None, 1, D_MODEL), layer),
            pl.BlockSpec((None, D_MODEL, IN_W), layer),
            pl.BlockSpec((None, CONV_K, CONV_W), layer),
            pl.BlockSpec((None, D_MODEL, D_MODEL), layer),
            pl.BlockSpec((None, D_MODEL, D_MODEL), layer),
            pl.BlockSpec((None, PLE_DIM, D_MODEL), layer),
            full((1, D_MODEL)),
            pl.BlockSpec((None, nb, WINDOW, KV_W), lambda g: (i, g, 0, 0)),
            pl.BlockSpec((None, nb, WINDOW, KV_W), lambda g: (i, g, 0, 0)),
        ],
        out_specs=[
            full((rows, D_MODEL)),
            full((rows, CONV_W)),
            pl.BlockSpec((nb, WINDOW, KV_W), lambda g: (g, 0, 0)),
            pl.BlockSpec((nb, WINDOW, KV_W), lambda g: (g, 0, 0)),
        ],
        out_shape=[
            jax.ShapeDtypeStruct((rows, D_MODEL), F32),
            jax.ShapeDtypeStruct((rows, CONV_W), F32),
            jax.ShapeDtypeStruct((dec_batch, WINDOW, KV_W), F32),
            jax.ShapeDtypeStruct((dec_batch, WINDOW, KV_W), F32),
        ],
        scratch_shapes=[
            pltpu.VMEM((rows, ATTN_W), BF),
            pltpu.VMEM((rows, KV_W), F32),
            pltpu.VMEM((rows, KV_W), F32),
            pltpu.VMEM((rows, ATTN_W), F32),
            pltpu.VMEM((rows, D_MODEL), BF),
            pltpu.VMEM((rows + SUBLANES, CONV_W), F32),
        ],
        compiler_params=pltpu.CompilerParams(
            dimension_semantics=("arbitrary",), vmem_limit_bytes=VMEM_LIMIT),
    )(sinks[i], x, p_all, cos, sin_s, s1, s2, g_norm, w_in, conv_w, w_out, w_pg, w_pp, g_final,
      cache_k, cache_v)


def _rope_tables(pos):
    inv = ROPE_THETA ** (-jnp.arange(0, HEAD_DIM, 2, dtype=F32) / HEAD_DIM)
    ang = pos[:, None] * inv[None, :]
    cos = jnp.tile(jnp.cos(ang), (1, 2 * LANES // HEAD_DIM))
    sin = jnp.sin(ang)
    sin_signed = jnp.tile(jnp.concatenate([-sin, sin], axis=-1), (1, LANES // HEAD_DIM))
    return cos, sin_signed


def kernel(x_prompt, x_sample, cache_k, cache_v, state_conv, p_prompt, p_sample, g_norm, w_in, sinks, conv_w,
           w_out, w_pg, w_pp, g_final):
    batch, seq, _ = x_prompt.shape
    dec_batch, t_dec, _ = x_sample.shape
    rows = dec_batch * t_dec

    cos_p, sin_p = _rope_tables(jnp.arange(seq, dtype=F32))
    cos_s, sin_s = _rope_tables(jnp.tile(PAST_LEN + jnp.arange(t_dec, dtype=F32), dec_batch))

    w_in_b, w_out_b, w_pg_b, w_pp_b = (w.astype(BF) for w in (w_in, w_out, w_pg, w_pp))
    g_norm3 = g_norm.reshape(DEPTH, 1, D_MODEL)
    g_fin2 = g_final.reshape(1, D_MODEL)
    ck = cache_k.reshape(DEPTH, dec_batch, WINDOW, KV_W)
    cv = cache_v.reshape(DEPTH, dec_batch, WINDOW, KV_W)
    p_s = p_sample.reshape(DEPTH, rows, PLE_DIM)
    zpad = jnp.zeros((DEPTH, dec_batch, t_dec, CONV_W), F32)
    s1_all = zpad.at[:, :, 0].set(state_conv[:, :, 1]).reshape(DEPTH, rows, CONV_W)
    s2_all = zpad.at[:, :, 0:2].set(state_conv).reshape(DEPTH, rows, CONV_W)

    xp = x_prompt
    xs = x_sample.reshape(rows, D_MODEL)
    nkp, nvp, ncp, nks, nvs, ncs = [], [], [], [], [], []
    for i in range(DEPTH):
        final = i == DEPTH - 1
        xp, kt, vt, ut = _prompt_layer(i, xp, p_prompt, cos_p, sin_p, sinks, g_norm3, w_in_b, conv_w, w_out_b,
                                       w_pg_b, w_pp_b, g_fin2, final)
        nkp.append(kt)
        nvp.append(vt)
        ncp.append(ut)
        xs, us, nk, nv = _sample_layer(i, xs, p_s, cos_s, sin_s, s1_all[i], s2_all[i], sinks, g_norm3, w_in_b,
                                       conv_w, w_out_b, w_pg_b, w_pp_b, g_fin2, ck, cv, t_dec, final)
        nks.append(nk)
        nvs.append(nv)
        ncs.append(us.reshape(dec_batch, t_dec, CONV_W)[:, t_dec - (CONV_K - 1):])

    kv5 = lambda parts, b: jnp.stack(parts).reshape(DEPTH, b, WINDOW, N_KV_HEADS, HEAD_DIM)
    return (xp, xs.reshape(dec_batch, t_dec, D_MODEL), kv5(nkp, batch), kv5(nvp, batch), jnp.stack(ncp),
            kv5(nks, dec_batch), kv5(nvs, dec_batch), jnp.stack(ncs))
```

```python
import functools

import jax
import jax.numpy as jnp
from jax import lax
from jax.experimental import pallas as pl
from jax.experimental.pallas import tpu as pltpu

D_MODEL = 1024
DEPTH = 4
N_HEADS = 8
HEAD_DIM = 64
N_KV_HEADS = 2
GROUP = N_HEADS // N_KV_HEADS
ATTN_W = N_HEADS * HEAD_DIM
KV_W = N_KV_HEADS * HEAD_DIM
CONV_W = D_MODEL - ATTN_W
CONV_K = 3
WINDOW = 128
BLOCK = 128
PLE_DIM = 256
PAST_LEN = 8192
ROPE_THETA = 10000.0
EPS = 1e-6
SCALE = HEAD_DIM ** -0.5
NEG = -1e30
IN_W = 2 * ATTN_W + 2 * KV_W + 4 * CONV_W

C_Q = 0
C_K = ATTN_W
C_V = C_K + KV_W
C_GA = C_V + KV_W
C_BG = C_GA + ATTN_W
C_CG = C_BG + CONV_W
C_HC = C_CG + CONV_W
C_GC = C_HC + CONV_W

LANES = 128
SUBLANES = 8
ROPE_HALF = HEAD_DIM // 2

MXU_N = 256
PROMPT_TILE = 512
SAMPLE_GROUP = 8
VMEM_LIMIT = 56 * 1024 * 1024

BF = jnp.bfloat16
F32 = jnp.float32
_NT = (((1,), (1,)), ((), ()))


def _mm(a, w):
    return jnp.dot(a, w, preferred_element_type=F32)


def _sigmoid(x):
    return 1.0 / (1.0 + jnp.exp(-x))


def _silu(x):
    return x * _sigmoid(x)


def _rmsnorm(x, g):
    r = lax.rsqrt(jnp.mean(x * x, axis=-1, keepdims=True) + EPS)
    return (x * r) * g


def _rope(x, cos, sin_signed):
    rows = x.shape[0]
    lane = lax.broadcasted_iota(jnp.int32, (rows, LANES), 1)
    first_half = (lane % HEAD_DIM) < ROPE_HALF
    outs = []
    for c in range(x.shape[1] // LANES):
        xc = x[:, c * LANES:(c + 1) * LANES]
        rot = jnp.where(first_half, pltpu.roll(xc, LANES - ROPE_HALF, 1), pltpu.roll(xc, ROPE_HALF, 1))
        outs.append(xc * cos + rot * sin_signed)
    return outs[0] if len(outs) == 1 else jnp.concatenate(outs, axis=1)


def _conv_gate(u, u1, u2, cw_ref, bg, gc):
    yc = cw_ref[0:1, :] * u2 + cw_ref[1:2, :] * u1 + cw_ref[2:3, :] * u
    return bg * yc * _silu(gc)


def _prompt_kernel(sink_ref, xa_ref, xb_ref, p_ref, cos_ref, sin_ref, gn_ref, win_ref, cw_ref, wout_ref, wpg_ref,
                   wpp_ref, gfin_ref, xo_ref, kt_ref, vt_ref, ut_ref, kbuf, vbuf, ubuf, mix_buf, xn_buf, zc_buf,
                   q_buf, sga_buf, x1_buf, x1b_buf, *, tile, n_seq, final):
    i = pl.program_id(0)
    n_tiles = pl.num_programs(0) - 1
    first_of_seq = jnp.minimum(i, n_tiles - 1) % n_seq == 0

    @pl.when(i == 0)
    def _():
        mix_buf[...] = jnp.zeros(mix_buf.shape, BF)

    @pl.when(first_of_seq)
    def _():
        kbuf[0:BLOCK, :] = jnp.zeros((BLOCK, KV_W), BF)
        vbuf[0:BLOCK, :] = jnp.zeros((BLOCK, KV_W), BF)
        ubuf[0:SUBLANES, :] = jnp.zeros((SUBLANES, CONV_W), F32)

    n_out = D_MODEL // MXU_N
    rows_per = tile // n_out
    for c in range(n_out):
        cs = slice(c * MXU_N, (c + 1) * MXU_N)
        x1 = (xb_ref[0, :, cs] + _mm(mix_buf[:, 0:ATTN_W], wout_ref[0:ATTN_W, cs])
              + _mm(mix_buf[:, ATTN_W:D_MODEL], wout_ref[ATTN_W:D_MODEL, cs]))
        x1_buf[:, cs] = x1
        x1b_buf[:, cs] = x1.astype(BF)
        rs = slice(c * rows_per, (c + 1) * rows_per)
        xn_buf[rs, :] = _rmsnorm(xa_ref[0, rs, :], gn_ref[...]).astype(BF)
    pb = p_ref[0].astype(BF)

    def gate_unit(c):
        cs = slice(c * MXU_N, (c + 1) * MXU_N)
        gate = _sigmoid(_mm(x1b_buf[...], wpg_ref[:, cs]))
        xo_ref[0, :, cs] = x1_buf[:, cs] + gate * _mm(pb, wpp_ref[:, cs])

    cos = cos_ref[...]
    sin_s = sin_ref[...]

    def proj(c0, width=MXU_N):
        return _mm(xn_buf[...], win_ref[:, c0:c0 + width])

    kv = proj(C_K, 2 * KV_W)
    k = _rope(kv[:, 0:KV_W], cos, sin_s)
    kbuf[BLOCK:BLOCK + tile, :] = k.astype(BF)
    v = kv[:, KV_W:2 * KV_W]
    vbuf[BLOCK:BLOCK + tile, :] = v.astype(BF)
    kt_ref[0] = k[tile - BLOCK:, :].T
    vt_ref[0] = v[tile - BLOCK:, :].T
    for c in range(ATTN_W // MXU_N):
        cs = slice(c * MXU_N, (c + 1) * MXU_N)
        q_buf[:, cs] = (_rope(proj(C_Q + c * MXU_N), cos, sin_s) * SCALE).astype(BF)

    conv_cols = (C_CG, C_HC, C_BG, C_GC)
    n_planes = len(conv_cols)
    n_slabs = CONV_W // MXU_N
    n_gate = ATTN_W // MXU_N
    n_dense = n_gate + n_slabs * n_planes

    def dense_unit(n):
        if n < n_gate:
            sga_buf[:, n * MXU_N:(n + 1) * MXU_N] = _silu(proj(C_GA + n * MXU_N))
        else:
            half, plane = divmod(n - n_gate, n_planes)
            zc_buf[plane, :, half * MXU_N:(half + 1) * MXU_N] = proj(conv_cols[plane] + half * MXU_N)

    def conv_unit(half):
        cs = slice(half * MXU_N, (half + 1) * MXU_N)
        u = zc_buf[0, :, cs] * zc_buf[1, :, cs]
        ubuf[SUBLANES:SUBLANES + tile, cs] = u
        u1 = ubuf[SUBLANES - 1:SUBLANES - 1 + tile, cs]
        u2 = ubuf[SUBLANES - 2:SUBLANES - 2 + tile, cs]
        yc = cw_ref[0:1, cs] * u2 + cw_ref[1:2, cs] * u1 + cw_ref[2:3, cs] * u
        mix_buf[:, ATTN_W + half * MXU_N:ATTN_W + (half + 1) * MXU_N] = (
            zc_buf[2, :, cs] * yc * _silu(zc_buf[3, :, cs])).astype(BF)
        ubuf[0:SUBLANES, cs] = ubuf[tile:tile + SUBLANES, cs]
        ut_ref[0, :, cs] = u[tile - (CONV_K - 1):, :]

    row = lax.broadcasted_iota(jnp.int32, (BLOCK, BLOCK), 0)
    col = lax.broadcasted_iota(jnp.int32, (BLOCK, BLOCK), 1)
    own = col <= row
    no_prev = jnp.where(first_of_seq, NEG, 0.0).astype(F32)

    def scores(n):
        j, g = divmod(n, N_KV_HEADS)
        kcat = kbuf[j * BLOCK:(j + 2) * BLOCK, g * HEAD_DIM:(g + 1) * HEAD_DIM]
        qst = jnp.concatenate(
            [q_buf[j * BLOCK:(j + 1) * BLOCK, (GROUP * g + h) * HEAD_DIM:(GROUP * g + h + 1) * HEAD_DIM]
             for h in range(GROUP)], axis=0)
        return lax.dot_general(qst, kcat, _NT, preferred_element_type=F32)

    def softmax(n, sc):
        j, g = divmod(n, N_KV_HEADS)
        probs, invs = [], []
        for h in range(GROUP):
            sh = sc[h * BLOCK:(h + 1) * BLOCK]
            s_prev = sh[:, :BLOCK]
            if j == 0:
                s_prev = s_prev + no_prev
            sm = jnp.where(own, sh[:, BLOCK:], s_prev)
            sink = sink_ref[GROUP * g + h]
            m = jnp.maximum(jnp.max(sm, axis=-1, keepdims=True), sink)
            e = jnp.exp(sm - m)
            den = jnp.sum(e, axis=-1, keepdims=True) + jnp.exp(sink - m)
            invs.append(1.0 / den)
            eb = e.astype(BF)
            zero = jnp.zeros_like(eb)
            probs.append(jnp.concatenate([jnp.where(own, zero, eb), jnp.where(own, eb, zero)], axis=1))
        return jnp.concatenate(probs, axis=0), invs

    heads = {}

    def values(n, probs, invs):
        j, g = divmod(n, N_KV_HEADS)
        vcat = vbuf[j * BLOCK:(j + 2) * BLOCK, g * HEAD_DIM:(g + 1) * HEAD_DIM]
        o = _mm(probs, vcat)
        heads.setdefault(j, []).extend(o[h * BLOCK:(h + 1) * BLOCK] * invs[h] for h in range(GROUP))
        if g == N_KV_HEADS - 1:
            rows = slice(j * BLOCK, (j + 1) * BLOCK)
            oa = jnp.concatenate(heads.pop(j), axis=1)
            mix_buf[rows, 0:ATTN_W] = (oa * sga_buf[rows, :]).astype(BF)

    n_units = (tile // BLOCK) * N_KV_HEADS
    conv_at = {n_gate + (h + 1) * n_planes: h for h in range(n_slabs)}
    n_steps = max(n_units + 2, max(conv_at) + 1)
    gate_at = {2 * c + 1: c for c in range(n_out)}
    sc_q, sm_q = {}, {}
    for t in range(n_steps):
        if t < n_units:
            sc_q[t] = scores(t)
        if t < n_dense:
            dense_unit(t)
        if 0 <= t - 2 < n_units:
            values(t - 2, *sm_q.pop(t - 2))
        if t in gate_at:
            gate_unit(gate_at[t])
        if 0 <= t - 1 < n_units:
            sm_q[t - 1] = softmax(t - 1, sc_q.pop(t - 1))
        if t in conv_at:
            conv_unit(conv_at[t])

    kbuf[0:BLOCK, :] = kbuf[tile:tile + BLOCK, :]
    vbuf[0:BLOCK, :] = vbuf[tile:tile + BLOCK, :]

    if final:
        xo_ref[0] = _rmsnorm(xo_ref[0], gfin_ref[...])


def _prompt_layer(i, x, p_all, cos, sin_s, sinks, g_norm, w_in, conv_w, w_out, w_pg, w_pp, g_final, final):
    batch, seq, _ = x.shape
    tile = PROMPT_TILE
    n_seq = seq // tile
    n_tiles = batch * n_seq
    in_tile = lambda t: jnp.minimum(t, n_tiles - 1)
    out_tile = lambda t: jnp.maximum(t - 1, 0)
    const = lambda t: (0, 0)
    layer = lambda t: (i, 0, 0)
    resident = dict(pipeline_mode=pl.Buffered(1))
    return pl.pallas_call(
        functools.partial(_prompt_kernel, tile=tile, n_seq=n_seq, final=final),
        name=f"prompt_layer{i}",
        grid=(n_tiles + 1,),
        in_specs=[
            pl.BlockSpec(memory_space=pltpu.SMEM),
            pl.BlockSpec((1, tile, D_MODEL), lambda t: (in_tile(t) // n_seq, in_tile(t) % n_seq, 0)),
            pl.BlockSpec((1, tile, D_MODEL), lambda t: (out_tile(t) // n_seq, out_tile(t) % n_seq, 0)),
            pl.BlockSpec((None, 1, tile, PLE_DIM),
                         lambda t: (i, out_tile(t) // n_seq, out_tile(t) % n_seq, 0)),
            pl.BlockSpec((tile, LANES), lambda t: (in_tile(t) % n_seq, 0)),
            pl.BlockSpec((tile, LANES), lambda t: (in_tile(t) % n_seq, 0)),
            pl.BlockSpec((None, 1, D_MODEL), layer, **resident),
            pl.BlockSpec((None, D_MODEL, IN_W), layer, **resident),
            pl.BlockSpec((None, CONV_K, CONV_W), layer, **resident),
            pl.BlockSpec((None, D_MODEL, D_MODEL), layer, **resident),
            pl.BlockSpec((None, D_MODEL, D_MODEL), layer, **resident),
            pl.BlockSpec((None, PLE_DIM, D_MODEL), layer, **resident),
            pl.BlockSpec((1, D_MODEL), const, **resident),
        ],
        out_specs=[
            pl.BlockSpec((1, tile, D_MODEL), lambda t: (out_tile(t) // n_seq, out_tile(t) % n_seq, 0)),
            pl.BlockSpec((1, BLOCK, KV_W), lambda t: (in_tile(t) // n_seq, 0, 0)),
            pl.BlockSpec((1, BLOCK, KV_W), lambda t: (in_tile(t) // n_seq, 0, 0)),
            pl.BlockSpec((1, CONV_K - 1, CONV_W), lambda t: (in_tile(t) // n_seq, 0, 0)),
        ],
        out_shape=[
            jax.ShapeDtypeStruct((batch, seq, D_MODEL), F32),
            jax.ShapeDtypeStruct((batch, BLOCK, KV_W), F32),
            jax.ShapeDtypeStruct((batch, BLOCK, KV_W), F32),
            jax.ShapeDtypeStruct((batch, CONV_K - 1, CONV_W), F32),
        ],
        scratch_shapes=[
            pltpu.VMEM((tile + BLOCK, KV_W), BF),
            pltpu.VMEM((tile + BLOCK, KV_W), BF),
            pltpu.VMEM((tile + SUBLANES, CONV_W), F32),
            pltpu.VMEM((tile, D_MODEL), BF),
            pltpu.VMEM((tile, D_MODEL), BF),
            pltpu.VMEM((4, tile, CONV_W), F32),
            pltpu.VMEM((tile, ATTN_W), BF),
            pltpu.VMEM((tile, ATTN_W), F32),
            pltpu.VMEM((tile, D_MODEL), F32),
            pltpu.VMEM((tile, D_MODEL), BF),
        ],
        compiler_params=pltpu.CompilerParams(dimension_semantics=("arbitrary",), vmem_limit_bytes=VMEM_LIMIT),
    )(sinks[i], x, x, p_all, cos, sin_s, g_norm, w_in, conv_w, w_out, w_pg, w_pp, g_final)


def _rope_t(x, cos_t, sin_t):
    parts = [x[r:r + ROPE_HALF] for r in range(0, x.shape[0], ROPE_HALF)]
    swapped = jnp.concatenate([parts[n ^ 1] for n in range(len(parts))], axis=0)
    return x * cos_t + swapped * sin_t


def _sample_kernel(sink_ref, x_ref, p_ref, cos_ref, sin_ref, cost_ref, sint_ref, st0_ref, st1_ref, gn_ref, win_ref,
                   wkvt_ref, cw_ref, wout_ref, wpg_ref, wpp_ref, gfin_ref, ck_ref, cv_ref,
                   xo_ref, nc0_ref, nc1_ref, nk_ref, nv_ref,
                   xs_buf, q_s, kt_s, vt_s, sga_s, mix_s, ubuf, p1_buf, p2_buf, *, rows, t_dec, chunk, nb):
    layer = pl.program_id(0)
    step = pl.program_id(1)
    last_layer = pl.num_programs(0) - 1
    last_step = pl.num_programs(1) - 1
    dec_batch = rows // t_dec
    gr = nb * t_dec
    cols = chunk * t_dec

    @pl.when((layer == 0) & (step == 0))
    def _():
        xs_buf[...] = x_ref[...]

    @pl.when(step == 0)
    def _():
        xn = _rmsnorm(xs_buf[...], gn_ref[...]).astype(BF)
        q_s[...] = (_rope(_mm(xn, win_ref[:, C_Q:C_K]), cos_ref[...], sin_ref[...]) * SCALE).astype(BF)
        kvt = lax.dot_general(wkvt_ref[...], xn, _NT, preferred_element_type=F32)
        kt_s[...] = _rope_t(kvt[0:KV_W], cost_ref[...], sint_ref[...])
        vt_s[...] = kvt[KV_W:2 * KV_W]
        sga_s[...] = _silu(_mm(xn, win_ref[:, C_GA:C_BG]))
        u = _mm(xn, win_ref[:, C_CG:C_HC]) * _mm(xn, win_ref[:, C_HC:C_GC])
        per_batch = lambda first: pl.ds(first, dec_batch, stride=t_dec)
        for c in range(CONV_W // LANES):
            cs = slice(c * LANES, (c + 1) * LANES)
            ubuf[c, 0:SUBLANES, :] = jnp.zeros((SUBLANES, LANES), F32)
            ubuf[c, SUBLANES:SUBLANES + rows, :] = u[:, cs]
            p1_buf[c] = ubuf[c, SUBLANES - 1:SUBLANES - 1 + rows, :]
            p2_buf[c] = ubuf[c, SUBLANES - 2:SUBLANES - 2 + rows, :]
            p1_buf[c, per_batch(0), :] = st1_ref[:, cs]
            p2_buf[c, per_batch(0), :] = st0_ref[:, cs]
            p2_buf[c, per_batch(1), :] = st1_ref[:, cs]
            nc0_ref[:, cs] = ubuf[c, per_batch(SUBLANES + t_dec - 2), :]
            nc1_ref[:, cs] = ubuf[c, per_batch(SUBLANES + t_dec - 1), :]
        u1 = jnp.concatenate([p1_buf[c] for c in range(CONV_W // LANES)], axis=1)
        u2 = jnp.concatenate([p2_buf[c] for c in range(CONV_W // LANES)], axis=1)
        bg = _mm(xn, win_ref[:, C_BG:C_CG])
        gc = _mm(xn, win_ref[:, C_GC:IN_W])
        mix_s[:, ATTN_W:D_MODEL] = _conv_gate(u, u1, u2, cw_ref, bg, gc).astype(BF)

    c0 = pl.multiple_of(step * cols, cols)
    kt_new = kt_s[:, pl.ds(c0, cols)]
    vt_new = vt_s[:, pl.ds(c0, cols)]

    qrow = lax.broadcasted_iota(jnp.int32, (gr, nb * WINDOW), 0)
    ccol = lax.broadcasted_iota(jnp.int32, (gr, nb * WINDOW), 1)
    valid_c = ((ccol // WINDOW) == (qrow // t_dec)) & ((ccol % WINDOW) > (qrow % t_dec))
    qrow_n = lax.broadcasted_iota(jnp.int32, (gr, gr), 0)
    ncol = lax.broadcasted_iota(jnp.int32, (gr, gr), 1)
    valid_n = ((ncol // t_dec) == (qrow_n // t_dec)) & ((ncol % t_dec) <= (qrow_n % t_dec))
    lane = lax.broadcasted_iota(jnp.int32, (KV_W, WINDOW), 1)
    kept = lane < WINDOW - t_dec

    for sg in range(chunk // nb):
        r0 = pl.multiple_of(step * cols + sg * gr, gr)
        qg = q_s[pl.ds(r0, gr), :]
        heads = []
        for g in range(N_KV_HEADS):
            hd = slice(g * HEAD_DIM, (g + 1) * HEAD_DIM)
            ktc = jnp.concatenate([ck_ref[sg * nb + b, hd, :] for b in range(nb)], axis=1).astype(BF)
            vtc = jnp.concatenate([cv_ref[sg * nb + b, hd, :] for b in range(nb)], axis=1).astype(BF)
            ktn = kt_new[hd, sg * gr:(sg + 1) * gr].astype(BF)
            vtn = vt_new[hd, sg * gr:(sg + 1) * gr].astype(BF)
            qst = jnp.concatenate(
                [qg[:, (GROUP * g + h) * HEAD_DIM:(GROUP * g + h + 1) * HEAD_DIM] for h in range(GROUP)], axis=0)
            sc = _mm(qst, ktc)
            sn = _mm(qst, ktn)
            ecs, ens, invs = [], [], []
            for h in range(GROUP):
                scm = jnp.where(valid_c, sc[h * gr:(h + 1) * gr], NEG)
                snm = jnp.where(valid_n, sn[h * gr:(h + 1) * gr], NEG)
                sink = sink_ref[layer, GROUP * g + h]
                m = jnp.maximum(jnp.maximum(jnp.max(scm, axis=-1, keepdims=True),
                                            jnp.max(snm, axis=-1, keepdims=True)), sink)
                ec = jnp.exp(scm - m)
                en = jnp.exp(snm - m)
                den = (jnp.sum(ec, axis=-1, keepdims=True) + jnp.sum(en, axis=-1, keepdims=True)
                       + jnp.exp(sink - m))
                invs.append(1.0 / den)
                ecs.append(ec.astype(BF))
                ens.append(en.astype(BF))
            o = (lax.dot_general(jnp.concatenate(ecs, axis=0), vtc, _NT, preferred_element_type=F32)
                 + lax.dot_general(jnp.concatenate(ens, axis=0), vtn, _NT, preferred_element_type=F32))
            for h in range(GROUP):
                heads.append(o[h * gr:(h + 1) * gr] * invs[h])
        oa = jnp.concatenate(heads, axis=1)
        mix_s[pl.ds(r0, gr), 0:ATTN_W] = (oa * sga_s[pl.ds(r0, gr), :]).astype(BF)

        for b in range(sg * nb, (sg + 1) * nb):
            place = (WINDOW - t_dec - t_dec * b) % cols
            k_in = pltpu.roll(kt_new, place, 1) if place else kt_new
            v_in = pltpu.roll(vt_new, place, 1) if place else vt_new
            nk_ref[b] = jnp.where(kept, pltpu.roll(ck_ref[b], WINDOW - t_dec, 1), k_in)
            nv_ref[b] = jnp.where(kept, pltpu.roll(cv_ref[b], WINDOW - t_dec, 1), v_in)

    @pl.when(step == last_step)
    def _():
        x1 = xs_buf[...] + _mm(mix_s[...], wout_ref[...])
        gate = _sigmoid(_mm(x1.astype(BF), wpg_ref[...]))
        x2 = x1 + gate * _mm(p_ref[...].astype(BF), wpp_ref[...])
        xs_buf[...] = x2

        @pl.when(layer == last_layer)
        def _():
            xo_ref[...] = _rmsnorm(x2, gfin_ref[...])


def _sample_trunk(x, p_all, cos, sin_s, cos_t, sin_t, st0, st1, sinks, g_norm, w_in, w_kvt, conv_w, w_out, w_pg, w_pp,
                  g_final, cache_kt, cache_vt, t_dec):
    rows = x.shape[0]
    dec_batch = rows // t_dec
    chunk = LANES // t_dec
    nb = SAMPLE_GROUP
    const = lambda l, s: (0, 0)
    layer = lambda l, s: (l, 0, 0)
    once = dict(pipeline_mode=pl.Buffered(1))
    full = lambda shape: pl.BlockSpec(shape, const, **once)
    per_layer = lambda *shape: pl.BlockSpec((None,) + shape, layer, **once)
    cache = pl.BlockSpec((None, chunk, KV_W, WINDOW), lambda l, s: (l, s, 0, 0))
    return pl.pallas_call(
        functools.partial(_sample_kernel, rows=rows, t_dec=t_dec, chunk=chunk, nb=nb),
        name="sample_trunk",
        grid=(DEPTH, dec_batch // chunk),
        in_specs=[
            pl.BlockSpec(memory_space=pltpu.SMEM),
            full((rows, D_MODEL)),
            per_layer(rows, PLE_DIM),
            full((rows, LANES)),
            full((rows, LANES)),
            full((KV_W, rows)),
            full((KV_W, rows)),
            per_layer(dec_batch, CONV_W),
            per_layer(dec_batch, CONV_W),
            per_layer(1, D_MODEL),
            per_layer(D_MODEL, IN_W),
            per_layer(2 * KV_W, D_MODEL),
            per_layer(CONV_K, CONV_W),
            per_layer(D_MODEL, D_MODEL),
            per_layer(D_MODEL, D_MODEL),
            per_layer(PLE_DIM, D_MODEL),
            full((1, D_MODEL)),
            cache,
            cache,
        ],
        out_specs=[
            pl.BlockSpec((rows, D_MODEL), const),
            pl.BlockSpec((None, dec_batch, CONV_W), layer),
            pl.BlockSpec((None, dec_batch, CONV_W), layer),
            cache,
            cache,
        ],
        out_shape=[
            jax.ShapeDtypeStruct((rows, D_MODEL), F32),
            jax.ShapeDtypeStruct((DEPTH, dec_batch, CONV_W), F32),
            jax.ShapeDtypeStruct((DEPTH, dec_batch, CONV_W), F32),
            jax.ShapeDtypeStruct((DEPTH, dec_batch, KV_W, WINDOW), F32),
            jax.ShapeDtypeStruct((DEPTH, dec_batch, KV_W, WINDOW), F32),
        ],
        scratch_shapes=[
            pltpu.VMEM((rows, D_MODEL), F32),
            pltpu.VMEM((rows, ATTN_W), BF),
            pltpu.VMEM((KV_W, rows), F32),
            pltpu.VMEM((KV_W, rows), F32),
            pltpu.VMEM((rows, ATTN_W), F32),
            pltpu.VMEM((rows, D_MODEL), BF),
            pltpu.VMEM((CONV_W // LANES, rows + SUBLANES, LANES), F32),
            pltpu.VMEM((CONV_W // LANES, rows, LANES), F32),
            pltpu.VMEM((CONV_W // LANES, rows, LANES), F32),
        ],
        compiler_params=pltpu.CompilerParams(
            dimension_semantics=("arbitrary", "arbitrary"), vmem_limit_bytes=VMEM_LIMIT),
    )(sinks, x, p_all, cos, sin_s, cos_t, sin_t, st0, st1, g_norm, w_in, w_kvt, conv_w, w_out, w_pg, w_pp, g_final,
      cache_kt, cache_vt)


def _rope_angles(pos):
    inv = ROPE_THETA ** (-jnp.arange(0, HEAD_DIM, 2, dtype=F32) / HEAD_DIM)
    return pos[:, None] * inv[None, :]


def _rope_tables(pos):
    ang = _rope_angles(pos)
    cos = jnp.tile(jnp.cos(ang), (1, 2 * LANES // HEAD_DIM))
    sin = jnp.sin(ang)
    sin_signed = jnp.tile(jnp.concatenate([-sin, sin], axis=-1), (1, LANES // HEAD_DIM))
    return cos, sin_signed


def _rope_tables_t(pos):
    ang = _rope_angles(pos).T
    cos = jnp.tile(jnp.cos(ang), (2 * KV_W // HEAD_DIM, 1))
    sin = jnp.sin(ang)
    sin_signed = jnp.tile(jnp.concatenate([-sin, sin], axis=0), (KV_W // HEAD_DIM, 1))
    return cos, sin_signed


def kernel(x_prompt, x_sample, cache_k, cache_v, state_conv, p_prompt, p_sample, g_norm, w_in, sinks, conv_w,
           w_out, w_pg, w_pp, g_final):
    batch, seq, _ = x_prompt.shape
    dec_batch, t_dec, _ = x_sample.shape
    rows = dec_batch * t_dec

    cos_p, sin_p = _rope_tables(jnp.arange(seq, dtype=F32))
    pos_s = jnp.tile(PAST_LEN + jnp.arange(t_dec, dtype=F32), dec_batch)
    cos_s, sin_s = _rope_tables(pos_s)
    cos_st, sin_st = _rope_tables_t(pos_s)

    w_in_b, w_out_b, w_pg_b, w_pp_b = (w.astype(BF) for w in (w_in, w_out, w_pg, w_pp))
    w_kvt = jnp.swapaxes(w_in[:, :, C_K:C_GA], 1, 2).astype(BF)
    g_norm3 = g_norm.reshape(DEPTH, 1, D_MODEL)
    g_fin2 = g_final.reshape(1, D_MODEL)
    to_t = lambda c: jnp.transpose(c, (0, 1, 3, 4, 2)).reshape(DEPTH, -1, KV_W, WINDOW)
    from_t = lambda c: jnp.transpose(c.reshape(DEPTH, -1, N_KV_HEADS, HEAD_DIM, WINDOW), (0, 1, 4, 2, 3))

    xp = x_prompt
    nkp, nvp, ncp = [], [], []
    for i in range(DEPTH):
        xp, kt, vt, ut = _prompt_layer(i, xp, p_prompt, cos_p, sin_p, sinks, g_norm3, w_in_b, conv_w, w_out_b,
                                       w_pg_b, w_pp_b, g_fin2, i == DEPTH - 1)
        nkp.append(kt)
        nvp.append(vt)
        ncp.append(ut)

    xs, nc0, nc1, nks, nvs = _sample_trunk(
        x_sample.reshape(rows, D_MODEL), p_sample.reshape(DEPTH, rows, PLE_DIM), cos_s, sin_s, cos_st, sin_st,
        state_conv[:, :, 0], state_conv[:, :, 1], sinks, g_norm3, w_in_b, w_kvt, conv_w, w_out_b, w_pg_b, w_pp_b,
        g_fin2, to_t(cache_k), to_t(cache_v), t_dec)

    return (xp, xs.reshape(dec_batch, t_dec, D_MODEL), from_t(jnp.stack(nkp)), from_t(jnp.stack(nvp)),
            jnp.stack(ncp), from_t(nks), from_t(nvs), jnp.stack([nc0, nc1], axis=2))
```

```python
import functools

import jax
import jax.numpy as jnp
from jax import lax
from jax.experimental import pallas as pl
from jax.experimental.pallas import tpu as pltpu

D_MODEL = 1024
DEPTH = 4
N_HEADS = 8
HEAD_DIM = 64
N_KV_HEADS = 2
GROUP = N_HEADS // N_KV_HEADS
ATTN_W = N_HEADS * HEAD_DIM
KV_W = N_KV_HEADS * HEAD_DIM
CONV_W = D_MODEL - ATTN_W
CONV_K = 3
WINDOW = 128
BLOCK = 128
PLE_DIM = 256
PAST_LEN = 8192
ROPE_THETA = 10000.0
EPS = 1e-6
SCALE = HEAD_DIM ** -0.5
NEG = -1e30
IN_W = 2 * ATTN_W + 2 * KV_W + 4 * CONV_W

C_Q = 0
C_K = ATTN_W
C_V = C_K + KV_W
C_GA = C_V + KV_W
C_BG = C_GA + ATTN_W
C_CG = C_BG + CONV_W
C_HC = C_CG + CONV_W
C_GC = C_HC + CONV_W

LANES = 128
SUBLANES = 8
ROPE_HALF = HEAD_DIM // 2

MXU_N = 256
PROMPT_TILE = 512
SAMPLE_GROUP = 8
VMEM_LIMIT = 56 * 1024 * 1024

BF = jnp.bfloat16
F32 = jnp.float32
_NT = (((1,), (1,)), ((), ()))


def _mm(a, w):
    return jnp.dot(a, w, preferred_element_type=F32)


def _sigmoid(x):
    return 1.0 / (1.0 + jnp.exp(-x))


def _silu(x):
    return x * _sigmoid(x)


def _rmsnorm(x, g):
    r = lax.rsqrt(jnp.mean(x * x, axis=-1, keepdims=True) + EPS)
    return (x * r) * g


def _rope(x, cos, sin_signed):
    rows = x.shape[0]
    lane = lax.broadcasted_iota(jnp.int32, (rows, LANES), 1)
    first_half = (lane % HEAD_DIM) < ROPE_HALF
    outs = []
    for c in range(x.shape[1] // LANES):
        xc = x[:, c * LANES:(c + 1) * LANES]
        rot = jnp.where(first_half, pltpu.roll(xc, LANES - ROPE_HALF, 1), pltpu.roll(xc, ROPE_HALF, 1))
        outs.append(xc * cos + rot * sin_signed)
    return outs[0] if len(outs) == 1 else jnp.concatenate(outs, axis=1)


def _conv_gate(u, u1, u2, cw_ref, bg, gc):
    yc = cw_ref[0:1, :] * u2 + cw_ref[1:2, :] * u1 + cw_ref[2:3, :] * u
    return bg * yc * _silu(gc)


def _prompt_kernel(sink_ref, xa_ref, xb_ref, p_ref, cos_ref, sin_ref, gn_ref, win_ref, cw_ref, wout_ref, wpg_ref,
                   wpp_ref, gfin_ref, xo_ref, kt_ref, vt_ref, ut_ref, kbuf, vbuf, ubuf, mix_buf, xn_buf, zc_buf,
                   q_buf, sga_buf, x1_buf, x1b_buf, *, tile, n_seq, final):
    i = pl.program_id(0)
    n_tiles = pl.num_programs(0) - 1
    first_of_seq = jnp.minimum(i, n_tiles - 1) % n_seq == 0

    @pl.when(i == 0)
    def _():
        mix_buf[...] = jnp.zeros(mix_buf.shape, BF)

    @pl.when(first_of_seq)
    def _():
        kbuf[0:BLOCK, :] = jnp.zeros((BLOCK, KV_W), BF)
        vbuf[0:BLOCK, :] = jnp.zeros((BLOCK, KV_W), BF)
        ubuf[0:SUBLANES, :] = jnp.zeros((SUBLANES, CONV_W), F32)

    n_out = D_MODEL // MXU_N
    rows_per = tile // n_out
    for c in range(n_out):
        cs = slice(c * MXU_N, (c + 1) * MXU_N)
        x1 = (xb_ref[0, :, cs] + _mm(mix_buf[:, 0:ATTN_W], wout_ref[0:ATTN_W, cs])
              + _mm(mix_buf[:, ATTN_W:D_MODEL], wout_ref[ATTN_W:D_MODEL, cs]))
        x1_buf[:, cs] = x1
        x1b_buf[:, cs] = x1.astype(BF)
        rs = slice(c * rows_per, (c + 1) * rows_per)
        xn_buf[rs, :] = _rmsnorm(xa_ref[0, rs, :], gn_ref[...]).astype(BF)
    pb = p_ref[0].astype(BF)

    def gate_unit(c):
        cs = slice(c * MXU_N, (c + 1) * MXU_N)
        gate = _sigmoid(_mm(x1b_buf[...], wpg_ref[:, cs]))
        xo_ref[0, :, cs] = x1_buf[:, cs] + gate * _mm(pb, wpp_ref[:, cs])

    cos = cos_ref[...]
    sin_s = sin_ref[...]

    def proj(c0, width=MXU_N):
        return _mm(xn_buf[...], win_ref[:, c0:c0 + width])

    kv = proj(C_K, 2 * KV_W)
    k = _rope(kv[:, 0:KV_W], cos, sin_s)
    kbuf[BLOCK:BLOCK + tile, :] = k.astype(BF)
    v = kv[:, KV_W:2 * KV_W]
    vbuf[BLOCK:BLOCK + tile, :] = v.astype(BF)
    kt_ref[0] = k[tile - BLOCK:, :].T
    vt_ref[0] = v[tile - BLOCK:, :].T
    for c in range(ATTN_W // MXU_N):
        cs = slice(c * MXU_N, (c + 1) * MXU_N)
        q_buf[:, cs] = (_rope(proj(C_Q + c * MXU_N), cos, sin_s) * SCALE).astype(BF)

    conv_cols = (C_CG, C_HC, C_BG, C_GC)
    n_planes = len(conv_cols)
    n_slabs = CONV_W // MXU_N
    n_gate = ATTN_W // MXU_N
    n_dense = n_gate + n_slabs * n_planes

    def dense_unit(n):
        if n < n_gate:
            sga_buf[:, n * MXU_N:(n + 1) * MXU_N] = _silu(proj(C_GA + n * MXU_N))
        else:
            half, plane = divmod(n - n_gate, n_planes)
            zc_buf[plane, :, half * MXU_N:(half + 1) * MXU_N] = proj(conv_cols[plane] + half * MXU_N)

    def conv_unit(half):
        cs = slice(half * MXU_N, (half + 1) * MXU_N)
        u = zc_buf[0, :, cs] * zc_buf[1, :, cs]
        ubuf[SUBLANES:SUBLANES + tile, cs] = u
        u1 = ubuf[SUBLANES - 1:SUBLANES - 1 + tile, cs]
        u2 = ubuf[SUBLANES - 2:SUBLANES - 2 + tile, cs]
        yc = cw_ref[0:1, cs] * u2 + cw_ref[1:2, cs] * u1 + cw_ref[2:3, cs] * u
        mix_buf[:, ATTN_W + half * MXU_N:ATTN_W + (half + 1) * MXU_N] = (
            zc_buf[2, :, cs] * yc * _silu(zc_buf[3, :, cs])).astype(BF)
        ubuf[0:SUBLANES, cs] = ubuf[tile:tile + SUBLANES, cs]
        ut_ref[0, :, cs] = u[tile - (CONV_K - 1):, :]

    row = lax.broadcasted_iota(jnp.int32, (BLOCK, BLOCK), 0)
    col = lax.broadcasted_iota(jnp.int32, (BLOCK, BLOCK), 1)
    own = col <= row
    no_prev = jnp.where(first_of_seq, NEG, 0.0).astype(F32)

    def scores(n):
        j, g = divmod(n, N_KV_HEADS)
        kcat = kbuf[j * BLOCK:(j + 2) * BLOCK, g * HEAD_DIM:(g + 1) * HEAD_DIM]
        qst = jnp.concatenate(
            [q_buf[j * BLOCK:(j + 1) * BLOCK, (GROUP * g + h) * HEAD_DIM:(GROUP * g + h + 1) * HEAD_DIM]
             for h in range(GROUP)], axis=0)
        return lax.dot_general(qst, kcat, _NT, preferred_element_type=F32)

    def softmax(n, sc):
        j, g = divmod(n, N_KV_HEADS)
        probs, invs = [], []
        for h in range(GROUP):
            sh = sc[h * BLOCK:(h + 1) * BLOCK]
            s_prev = sh[:, :BLOCK]
            if j == 0:
                s_prev = s_prev + no_prev
            sm = jnp.where(own, sh[:, BLOCK:], s_prev)
            sink = sink_ref[GROUP * g + h]
            m = jnp.maximum(jnp.max(sm, axis=-1, keepdims=True), sink)
            e = jnp.exp(sm - m)
            den = jnp.sum(e, axis=-1, keepdims=True) + jnp.exp(sink - m)
            invs.append(1.0 / den)
            eb = e.astype(BF)
            zero = jnp.zeros_like(eb)
            probs.append(jnp.concatenate([jnp.where(own, zero, eb), jnp.where(own, eb, zero)], axis=1))
        return jnp.concatenate(probs, axis=0), invs

    heads = {}

    def values(n, probs, invs):
        j, g = divmod(n, N_KV_HEADS)
        vcat = vbuf[j * BLOCK:(j + 2) * BLOCK, g * HEAD_DIM:(g + 1) * HEAD_DIM]
        o = _mm(probs, vcat)
        heads.setdefault(j, []).extend(o[h * BLOCK:(h + 1) * BLOCK] * invs[h] for h in range(GROUP))
        if g == N_KV_HEADS - 1:
            rows = slice(j * BLOCK, (j + 1) * BLOCK)
            oa = jnp.concatenate(heads.pop(j), axis=1)
            mix_buf[rows, 0:ATTN_W] = (oa * sga_buf[rows, :]).astype(BF)

    n_units = (tile // BLOCK) * N_KV_HEADS
    conv_at = {n_gate + (h + 1) * n_planes: h for h in range(n_slabs)}
    n_steps = max(n_units + 2, max(conv_at) + 1)
    gate_at = {2 * c + 1: c for c in range(n_out)}
    sc_q, sm_q = {}, {}
    for t in range(n_steps):
        if t < n_units:
            sc_q[t] = scores(t)
        if t < n_dense:
            dense_unit(t)
        if 0 <= t - 2 < n_units:
            values(t - 2, *sm_q.pop(t - 2))
        if t in gate_at:
            gate_unit(gate_at[t])
        if 0 <= t - 1 < n_units:
            sm_q[t - 1] = softmax(t - 1, sc_q.pop(t - 1))
        if t in conv_at:
            conv_unit(conv_at[t])

    kbuf[0:BLOCK, :] = kbuf[tile:tile + BLOCK, :]
    vbuf[0:BLOCK, :] = vbuf[tile:tile + BLOCK, :]

    if final:
        xo_ref[0] = _rmsnorm(xo_ref[0], gfin_ref[...])


def _prompt_layer(i, x, p_all, cos, sin_s, sinks, g_norm, w_in, conv_w, w_out, w_pg, w_pp, g_final, final):
    batch, seq, _ = x.shape
    tile = PROMPT_TILE
    n_seq = seq // tile
    n_tiles = batch * n_seq
    in_tile = lambda t: jnp.minimum(t, n_tiles - 1)
    out_tile = lambda t: jnp.maximum(t - 1, 0)
    const = lambda t: (0, 0)
    layer = lambda t: (i, 0, 0)
    resident = dict(pipeline_mode=pl.Buffered(1))
    return pl.pallas_call(
        functools.partial(_prompt_kernel, tile=tile, n_seq=n_seq, final=final),
        name=f"prompt_layer{i}",
        grid=(n_tiles + 1,),
        in_specs=[
            pl.BlockSpec(memory_space=pltpu.SMEM),
            pl.BlockSpec((1, tile, D_MODEL), lambda t: (in_tile(t) // n_seq, in_tile(t) % n_seq, 0)),
            pl.BlockSpec((1, tile, D_MODEL), lambda t: (out_tile(t) // n_seq, out_tile(t) % n_seq, 0)),
            pl.BlockSpec((None, 1, tile, PLE_DIM),
                         lambda t: (i, out_tile(t) // n_seq, out_tile(t) % n_seq, 0)),
            pl.BlockSpec((tile, LANES), lambda t: (in_tile(t) % n_seq, 0)),
            pl.BlockSpec((tile, LANES), lambda t: (in_tile(t) % n_seq, 0)),
            pl.BlockSpec((None, 1, D_MODEL), layer, **resident),
            pl.BlockSpec((None, D_MODEL, IN_W), layer, **resident),
            pl.BlockSpec((None, CONV_K, CONV_W), layer, **resident),
            pl.BlockSpec((None, D_MODEL, D_MODEL), layer, **resident),
            pl.BlockSpec((None, D_MODEL, D_MODEL), layer, **resident),
            pl.BlockSpec((None, PLE_DIM, D_MODEL), layer, **resident),
            pl.BlockSpec((1, D_MODEL), const, **resident),
        ],
        out_specs=[
            pl.BlockSpec((1, tile, D_MODEL), lambda t: (out_tile(t) // n_seq, out_tile(t) % n_seq, 0)),
            pl.BlockSpec((1, BLOCK, KV_W), lambda t: (in_tile(t) // n_seq, 0, 0)),
            pl.BlockSpec((1, BLOCK, KV_W), lambda t: (in_tile(t) // n_seq, 0, 0)),
            pl.BlockSpec((1, CONV_K - 1, CONV_W), lambda t: (in_tile(t) // n_seq, 0, 0)),
        ],
        out_shape=[
            jax.ShapeDtypeStruct((batch, seq, D_MODEL), F32),
            jax.ShapeDtypeStruct((batch, BLOCK, KV_W), F32),
            jax.ShapeDtypeStruct((batch, BLOCK, KV_W), F32),
            jax.ShapeDtypeStruct((batch, CONV_K - 1, CONV_W), F32),
        ],
        scratch_shapes=[
            pltpu.VMEM((tile + BLOCK, KV_W), BF),
            pltpu.VMEM((tile + BLOCK, KV_W), BF),
            pltpu.VMEM((tile + SUBLANES, CONV_W), F32),
            pltpu.VMEM((tile, D_MODEL), BF),
            pltpu.VMEM((tile, D_MODEL), BF),
            pltpu.VMEM((4, tile, CONV_W), F32),
            pltpu.VMEM((tile, ATTN_W), BF),
            pltpu.VMEM((tile, ATTN_W), F32),
            pltpu.VMEM((tile, D_MODEL), F32),
            pltpu.VMEM((tile, D_MODEL), BF),
        ],
        compiler_params=pltpu.CompilerParams(dimension_semantics=("arbitrary",), vmem_limit_bytes=VMEM_LIMIT),
    )(sinks[i], x, x, p_all, cos, sin_s, g_norm, w_in, conv_w, w_out, w_pg, w_pp, g_final)


def _rope_t(x, cos_t, sin_t):
    parts = [x[r:r + ROPE_HALF] for r in range(0, x.shape[0], ROPE_HALF)]
    swapped = jnp.concatenate([parts[n ^ 1] for n in range(len(parts))], axis=0)
    return x * cos_t + swapped * sin_t


def _sample_kernel(sink_ref, x_ref, p_ref, cos_ref, sin_ref, cost_ref, sint_ref, st0_ref, st1_ref, gn_ref, win_ref,
                   cw_ref, wout_ref, wpg_ref, wpp_ref, gfin_ref, ck_ref, cv_ref,
                   xo_ref, nc0_ref, nc1_ref, nk_ref, nv_ref,
                   xs_buf, q_s, kt_s, vt_s, sga_s, mix_s, ubuf, p1_buf, p2_buf, *, rows, t_dec, chunk, nb):
    layer = pl.program_id(0)
    step = pl.program_id(1)
    last_layer = pl.num_programs(0) - 1
    last_step = pl.num_programs(1) - 1
    dec_batch = rows // t_dec
    gr = nb * t_dec
    cols = chunk * t_dec

    @pl.when((layer == 0) & (step == 0))
    def _():
        xs_buf[...] = x_ref[...]

    @pl.when(step == 0)
    def _():
        xn = _rmsnorm(xs_buf[...], gn_ref[...]).astype(BF)
        q_s[...] = (_rope(_mm(xn, win_ref[:, C_Q:C_K]), cos_ref[...], sin_ref[...]) * SCALE).astype(BF)
        kvt = lax.dot_general(win_ref[:, C_K:C_GA], xn, (((0,), (1,)), ((), ())),
                              preferred_element_type=F32)
        kt_s[...] = _rope_t(kvt[0:KV_W], cost_ref[...], sint_ref[...])
        vt_s[...] = kvt[KV_W:2 * KV_W]
        sga_s[...] = _silu(_mm(xn, win_ref[:, C_GA:C_BG]))
        u = _mm(xn, win_ref[:, C_CG:C_HC]) * _mm(xn, win_ref[:, C_HC:C_GC])
        per_batch = lambda first: pl.ds(first, dec_batch, stride=t_dec)
        for c in range(CONV_W // LANES):
            cs = slice(c * LANES, (c + 1) * LANES)
            ubuf[c, 0:SUBLANES, :] = jnp.zeros((SUBLANES, LANES), F32)
            ubuf[c, SUBLANES:SUBLANES + rows, :] = u[:, cs]
            p1_buf[c] = ubuf[c, SUBLANES - 1:SUBLANES - 1 + rows, :]
            p2_buf[c] = ubuf[c, SUBLANES - 2:SUBLANES - 2 + rows, :]
            p1_buf[c, per_batch(0), :] = st1_ref[:, cs]
            p2_buf[c, per_batch(0), :] = st0_ref[:, cs]
            p2_buf[c, per_batch(1), :] = st1_ref[:, cs]
            nc0_ref[:, cs] = ubuf[c, per_batch(SUBLANES + t_dec - 2), :]
            nc1_ref[:, cs] = ubuf[c, per_batch(SUBLANES + t_dec - 1), :]
        u1 = jnp.concatenate([p1_buf[c] for c in range(CONV_W // LANES)], axis=1)
        u2 = jnp.concatenate([p2_buf[c] for c in range(CONV_W // LANES)], axis=1)
        bg = _mm(xn, win_ref[:, C_BG:C_CG])
        gc = _mm(xn, win_ref[:, C_GC:IN_W])
        mix_s[:, ATTN_W:D_MODEL] = _conv_gate(u, u1, u2, cw_ref, bg, gc).astype(BF)

    c0 = pl.multiple_of(step * cols, cols)
    kt_new = kt_s[:, pl.ds(c0, cols)]
    vt_new = vt_s[:, pl.ds(c0, cols)]

    qrow = lax.broadcasted_iota(jnp.int32, (gr, nb * WINDOW), 0)
    ccol = lax.broadcasted_iota(jnp.int32, (gr, nb * WINDOW), 1)
    valid_c = ((ccol // WINDOW) == (qrow // t_dec)) & ((ccol % WINDOW) > (qrow % t_dec))
    qrow_n = lax.broadcasted_iota(jnp.int32, (gr, gr), 0)
    ncol = lax.broadcasted_iota(jnp.int32, (gr, gr), 1)
    valid_n = ((ncol // t_dec) == (qrow_n // t_dec)) & ((ncol % t_dec) <= (qrow_n % t_dec))
    lane = lax.broadcasted_iota(jnp.int32, (KV_W, WINDOW), 1)
    kept = lane < WINDOW - t_dec

    for sg in range(chunk // nb):
        r0 = pl.multiple_of(step * cols + sg * gr, gr)
        qg = q_s[pl.ds(r0, gr), :]
        heads = []
        for g in range(N_KV_HEADS):
            hd = slice(g * HEAD_DIM, (g + 1) * HEAD_DIM)
            ktc = jnp.concatenate([ck_ref[sg * nb + b, hd, :] for b in range(nb)], axis=1).astype(BF)
            vtc = jnp.concatenate([cv_ref[sg * nb + b, hd, :] for b in range(nb)], axis=1).astype(BF)
            ktn = kt_new[hd, sg * gr:(sg + 1) * gr].astype(BF)
            vtn = vt_new[hd, sg * gr:(sg + 1) * gr].astype(BF)
            qst = jnp.concatenate(
                [qg[:, (GROUP * g + h) * HEAD_DIM:(GROUP * g + h + 1) * HEAD_DIM] for h in range(GROUP)], axis=0)
            sc = _mm(qst, ktc)
            sn = _mm(qst, ktn)
            ecs, ens, invs = [], [], []
            for h in range(GROUP):
                scm = jnp.where(valid_c, sc[h * gr:(h + 1) * gr], NEG)
                snm = jnp.where(valid_n, sn[h * gr:(h + 1) * gr], NEG)
                sink = sink_ref[layer, GROUP * g + h]
                m = jnp.maximum(jnp.maximum(jnp.max(scm, axis=-1, keepdims=True),
                                            jnp.max(snm, axis=-1, keepdims=True)), sink)
                ec = jnp.exp(scm - m)
                en = jnp.exp(snm - m)
                den = (jnp.sum(ec, axis=-1, keepdims=True) + jnp.sum(en, axis=-1, keepdims=True)
                       + jnp.exp(sink - m))
                invs.append(1.0 / den)
                ecs.append(ec.astype(BF))
                ens.append(en.astype(BF))
            o = (lax.dot_general(jnp.concatenate(ecs, axis=0), vtc, _NT, preferred_element_type=F32)
                 + lax.dot_general(jnp.concatenate(ens, axis=0), vtn, _NT, preferred_element_type=F32))
            for h in range(GROUP):
                heads.append(o[h * gr:(h + 1) * gr] * invs[h])
        oa = jnp.concatenate(heads, axis=1)
        mix_s[pl.ds(r0, gr), 0:ATTN_W] = (oa * sga_s[pl.ds(r0, gr), :]).astype(BF)

        for b in range(sg * nb, (sg + 1) * nb):
            place = (WINDOW - t_dec - t_dec * b) % cols
            k_in = pltpu.roll(kt_new, place, 1) if place else kt_new
            v_in = pltpu.roll(vt_new, place, 1) if place else vt_new
            nk_ref[b] = jnp.where(kept, pltpu.roll(ck_ref[b], WINDOW - t_dec, 1), k_in)
            nv_ref[b] = jnp.where(kept, pltpu.roll(cv_ref[b], WINDOW - t_dec, 1), v_in)

    @pl.when(step == last_step)
    def _():
        x1 = xs_buf[...] + _mm(mix_s[...], wout_ref[...])
        gate = _sigmoid(_mm(x1.astype(BF), wpg_ref[...]))
        x2 = x1 + gate * _mm(p_ref[...].astype(BF), wpp_ref[...])
        xs_buf[...] = x2

        @pl.when(layer == last_layer)
        def _():
            xo_ref[...] = _rmsnorm(x2, gfin_ref[...])


def _sample_trunk(x, p_all, cos, sin_s, cos_t, sin_t, st0, st1, sinks, g_norm, w_in, conv_w, w_out, w_pg, w_pp,
                  g_final, cache_kt, cache_vt, t_dec):
    rows = x.shape[0]
    dec_batch = rows // t_dec
    chunk = LANES // t_dec
    nb = SAMPLE_GROUP
    const = lambda l, s: (0, 0)
    layer = lambda l, s: (l, 0, 0)
    once = dict(pipeline_mode=pl.Buffered(1))
    full = lambda shape: pl.BlockSpec(shape, const, **once)
    per_layer = lambda *shape: pl.BlockSpec((None,) + shape, layer, **once)
    cache = pl.BlockSpec((None, chunk, KV_W, WINDOW), lambda l, s: (l, s, 0, 0))
    return pl.pallas_call(
        functools.partial(_sample_kernel, rows=rows, t_dec=t_dec, chunk=chunk, nb=nb),
        name="sample_trunk",
        grid=(DEPTH, dec_batch // chunk),
        in_specs=[
            pl.BlockSpec(memory_space=pltpu.SMEM),
            full((rows, D_MODEL)),
            per_layer(rows, PLE_DIM),
            full((rows, LANES)),
            full((rows, LANES)),
            full((KV_W, rows)),
            full((KV_W, rows)),
            per_layer(dec_batch, CONV_W),
            per_layer(dec_batch, CONV_W),
            per_layer(1, D_MODEL),
            per_layer(D_MODEL, IN_W),
            per_layer(CONV_K, CONV_W),
            per_layer(D_MODEL, D_MODEL),
            per_layer(D_MODEL, D_MODEL),
            per_layer(PLE_DIM, D_MODEL),
            full((1, D_MODEL)),
            cache,
            cache,
        ],
        out_specs=[
            pl.BlockSpec((rows, D_MODEL), const),
            pl.BlockSpec((None, dec_batch, CONV_W), layer),
            pl.BlockSpec((None, dec_batch, CONV_W), layer),
            cache,
            cache,
        ],
        out_shape=[
            jax.ShapeDtypeStruct((rows, D_MODEL), F32),
            jax.ShapeDtypeStruct((DEPTH, dec_batch, CONV_W), F32),
            jax.ShapeDtypeStruct((DEPTH, dec_batch, CONV_W), F32),
            jax.ShapeDtypeStruct((DEPTH, dec_batch, KV_W, WINDOW), F32),
            jax.ShapeDtypeStruct((DEPTH, dec_batch, KV_W, WINDOW), F32),
        ],
        scratch_shapes=[
            pltpu.VMEM((rows, D_MODEL), F32),
            pltpu.VMEM((rows, ATTN_W), BF),
            pltpu.VMEM((KV_W, rows), F32),
            pltpu.VMEM((KV_W, rows), F32),
            pltpu.VMEM((rows, ATTN_W), F32),
            pltpu.VMEM((rows, D_MODEL), BF),
            pltpu.VMEM((CONV_W // LANES, rows + SUBLANES, LANES), F32),
            pltpu.VMEM((CONV_W // LANES, rows, LANES), F32),
            pltpu.VMEM((CONV_W // LANES, rows, LANES), F32),
        ],
        compiler_params=pltpu.CompilerParams(
            dimension_semantics=("arbitrary", "arbitrary"), vmem_limit_bytes=VMEM_LIMIT),
    )(sinks, x, p_all, cos, sin_s, cos_t, sin_t, st0, st1, g_norm, w_in, conv_w, w_out, w_pg, w_pp, g_final,
      cache_kt, cache_vt)


def _rope_angles(pos):
    inv = ROPE_THETA ** (-jnp.arange(0, HEAD_DIM, 2, dtype=F32) / HEAD_DIM)
    return pos[:, None] * inv[None, :]


def _rope_tables(pos):
    ang = _rope_angles(pos)
    cos = jnp.tile(jnp.cos(ang), (1, 2 * LANES // HEAD_DIM))
    sin = jnp.sin(ang)
    sin_signed = jnp.tile(jnp.concatenate([-sin, sin], axis=-1), (1, LANES // HEAD_DIM))
    return cos, sin_signed


def _rope_tables_t(pos):
    ang = _rope_angles(pos).T
    cos = jnp.tile(jnp.cos(ang), (2 * KV_W // HEAD_DIM, 1))
    sin = jnp.sin(ang)
    sin_signed = jnp.tile(jnp.concatenate([-sin, sin], axis=0), (KV_W // HEAD_DIM, 1))
    return cos, sin_signed


def kernel(x_prompt, x_sample, cache_k, cache_v, state_conv, p_prompt, p_sample, g_norm, w_in, sinks, conv_w,
           w_out, w_pg, w_pp, g_final):
    batch, seq, _ = x_prompt.shape
    dec_batch, t_dec, _ = x_sample.shape
    rows = dec_batch * t_dec

    cos_p, sin_p = _rope_tables(jnp.arange(seq, dtype=F32))
    pos_s = jnp.tile(PAST_LEN + jnp.arange(t_dec, dtype=F32), dec_batch)
    cos_s, sin_s = _rope_tables(pos_s)
    cos_st, sin_st = _rope_tables_t(pos_s)

    w_in_b, w_out_b, w_pg_b, w_pp_b = (w.astype(BF) for w in (w_in, w_out, w_pg, w_pp))
    g_norm3 = g_norm.reshape(DEPTH, 1, D_MODEL)
    g_fin2 = g_final.reshape(1, D_MODEL)
    to_t = lambda c: jnp.transpose(c, (0, 1, 3, 4, 2)).reshape(DEPTH, -1, KV_W, WINDOW)
    from_t = lambda c: jnp.transpose(c.reshape(DEPTH, -1, N_KV_HEADS, HEAD_DIM, WINDOW), (0, 1, 4, 2, 3))

    xp = x_prompt
    nkp, nvp, ncp = [], [], []
    for i in range(DEPTH):
        xp, kt, vt, ut = _prompt_layer(i, xp, p_prompt, cos_p, sin_p, sinks, g_norm3, w_in_b, conv_w, w_out_b,
                                       w_pg_b, w_pp_b, g_fin2, i == DEPTH - 1)
        nkp.append(kt)
        nvp.append(vt)
        ncp.append(ut)

    xs, nc0, nc1, nks, nvs = _sample_trunk(
        x_sample.reshape(rows, D_MODEL), p_sample.reshape(DEPTH, rows, PLE_DIM), cos_s, sin_s, cos_st, sin_st,
        state_conv[:, :, 0], state_conv[:, :, 1], sinks, g_norm3, w_in_b, conv_w, w_out_b, w_pg_b, w_pp_b,
        g_fin2, to_t(cache_k), to_t(cache_v), t_dec)

    return (xp, xs.reshape(dec_batch, t_dec, D_MODEL), from_t(jnp.stack(nkp)), from_t(jnp.stack(nvp)),
            jnp.stack(ncp), from_t(nks), from_t(nvs), jnp.stack([nc0, nc1], axis=2))
```

```python
import functools

import jax
import jax.numpy as jnp
from jax import lax
from jax.experimental import pallas as pl
from jax.experimental.pallas import tpu as pltpu

D_MODEL = 1024
DEPTH = 4
N_HEADS = 8
HEAD_DIM = 64
N_KV_HEADS = 2
GROUP = N_HEADS // N_KV_HEADS
ATTN_W = N_HEADS * HEAD_DIM
KV_W = N_KV_HEADS * HEAD_DIM
CONV_W = D_MODEL - ATTN_W
CONV_K = 3
WINDOW = 128
BLOCK = 128
PLE_DIM = 256
PAST_LEN = 8192
ROPE_THETA = 10000.0
EPS = 1e-6
SCALE = HEAD_DIM ** -0.5
NEG = -1e30
IN_W = 2 * ATTN_W + 2 * KV_W + 4 * CONV_W

C_Q = 0
C_K = ATTN_W
C_V = C_K + KV_W
C_GA = C_V + KV_W
C_BG = C_GA + ATTN_W
C_CG = C_BG + CONV_W
C_HC = C_CG + CONV_W
C_GC = C_HC + CONV_W

LANES = 128
SUBLANES = 8
ROPE_HALF = HEAD_DIM // 2

MXU_N = 256
PROMPT_TILE = 512
DENSE_PER_STEP = 2
SAMPLE_GROUP = 8
VMEM_LIMIT = 56 * 1024 * 1024

BF = jnp.bfloat16
F32 = jnp.float32
_NT = (((1,), (1,)), ((), ()))


def _mm(a, w):
    return jnp.dot(a, w, preferred_element_type=F32)


def _sigmoid(x):
    return 1.0 / (1.0 + jnp.exp(-x))


def _silu(x):
    return x * _sigmoid(x)


def _rmsnorm(x, g):
    r = lax.rsqrt(jnp.mean(x * x, axis=-1, keepdims=True) + EPS)
    return (x * r) * g


def _rope(x, cos, sin_signed):
    rows = x.shape[0]
    lane = lax.broadcasted_iota(jnp.int32, (rows, LANES), 1)
    first_half = (lane % HEAD_DIM) < ROPE_HALF
    outs = []
    for c in range(x.shape[1] // LANES):
        xc = x[:, c * LANES:(c + 1) * LANES]
        rot = jnp.where(first_half, pltpu.roll(xc, LANES - ROPE_HALF, 1), pltpu.roll(xc, ROPE_HALF, 1))
        outs.append(xc * cos + rot * sin_signed)
    return outs[0] if len(outs) == 1 else jnp.concatenate(outs, axis=1)


def _conv_gate(u, u1, u2, cw_ref, bg, gc):
    yc = cw_ref[0:1, :] * u2 + cw_ref[1:2, :] * u1 + cw_ref[2:3, :] * u
    return bg * yc * _silu(gc)


def _prompt_kernel(sink_ref, xa_ref, xb_ref, p_ref, cos_ref, sin_ref, gn_ref, win_ref, cw_ref, wout_ref, wpg_ref,
                   wpp_ref, gfin_ref, xo_ref, kt_ref, vt_ref, ut_ref, kbuf, vbuf, ubuf, mix_buf, xn_buf, zc_buf,
                   q_buf, sga_buf, x1_buf, x1b_buf, *, tile, n_seq, final):
    i = pl.program_id(0)
    n_tiles = pl.num_programs(0) - 1
    first_of_seq = jnp.minimum(i, n_tiles - 1) % n_seq == 0

    @pl.when(i == 0)
    def _():
        mix_buf[...] = jnp.zeros(mix_buf.shape, BF)

    @pl.when(first_of_seq)
    def _():
        kbuf[0:BLOCK, :] = jnp.zeros((BLOCK, KV_W), BF)
        vbuf[:, 0:BLOCK] = jnp.zeros((KV_W, BLOCK), BF)
        ubuf[0:SUBLANES, :] = jnp.zeros((SUBLANES, CONV_W), F32)

    n_out = D_MODEL // MXU_N
    rows_per = tile // n_out
    for c in range(n_out):
        cs = slice(c * MXU_N, (c + 1) * MXU_N)
        x1 = (xb_ref[0, :, cs] + _mm(mix_buf[:, 0:ATTN_W], wout_ref[0:ATTN_W, cs])
              + _mm(mix_buf[:, ATTN_W:D_MODEL], wout_ref[ATTN_W:D_MODEL, cs]))
        x1_buf[:, cs] = x1
        x1b_buf[:, cs] = x1.astype(BF)
        rs = slice(c * rows_per, (c + 1) * rows_per)
        xn_buf[rs, :] = _rmsnorm(xa_ref[0, rs, :], gn_ref[...]).astype(BF)
    pb = p_ref[0].astype(BF)

    def gate_unit(c):
        cs = slice(c * MXU_N, (c + 1) * MXU_N)
        gate = _sigmoid(_mm(x1b_buf[...], wpg_ref[:, cs]))
        xo_ref[0, :, cs] = x1_buf[:, cs] + gate * _mm(pb, wpp_ref[:, cs])

    cos = cos_ref[...]
    sin_s = sin_ref[...]

    def proj(c0, width=MXU_N):
        return _mm(xn_buf[...], win_ref[:, c0:c0 + width])

    kv = proj(C_K, 2 * KV_W)
    k = _rope(kv[:, 0:KV_W], cos, sin_s)
    kbuf[BLOCK:BLOCK + tile, :] = k.astype(BF)
    vt = kv[:, KV_W:2 * KV_W].T
    vbuf[:, BLOCK:BLOCK + tile] = vt.astype(BF)
    kt_ref[0] = k[tile - BLOCK:, :].T
    vt_ref[0] = vt[:, tile - BLOCK:]
    for c in range(ATTN_W // MXU_N):
        cs = slice(c * MXU_N, (c + 1) * MXU_N)
        q_buf[:, cs] = (_rope(proj(C_Q + c * MXU_N), cos, sin_s) * SCALE).astype(BF)

    conv_cols = (C_CG, C_HC, C_BG, C_GC)
    n_planes = len(conv_cols)
    n_slabs = CONV_W // MXU_N
    n_gate = ATTN_W // MXU_N
    n_dense = n_gate + n_slabs * n_planes

    def dense_unit(n):
        if n < n_gate:
            sga_buf[:, n * MXU_N:(n + 1) * MXU_N] = _silu(proj(C_GA + n * MXU_N))
        else:
            half, plane = divmod(n - n_gate, n_planes)
            zc_buf[plane, :, half * MXU_N:(half + 1) * MXU_N] = proj(conv_cols[plane] + half * MXU_N)

    def conv_unit(half):
        cs = slice(half * MXU_N, (half + 1) * MXU_N)
        u = zc_buf[0, :, cs] * zc_buf[1, :, cs]
        ubuf[SUBLANES:SUBLANES + tile, cs] = u
        u1 = ubuf[SUBLANES - 1:SUBLANES - 1 + tile, cs]
        u2 = ubuf[SUBLANES - 2:SUBLANES - 2 + tile, cs]
        yc = cw_ref[0:1, cs] * u2 + cw_ref[1:2, cs] * u1 + cw_ref[2:3, cs] * u
        mix_buf[:, ATTN_W + half * MXU_N:ATTN_W + (half + 1) * MXU_N] = (
            zc_buf[2, :, cs] * yc * _silu(zc_buf[3, :, cs])).astype(BF)
        ubuf[0:SUBLANES, cs] = ubuf[tile:tile + SUBLANES, cs]
        ut_ref[0, :, cs] = u[tile - (CONV_K - 1):, :]

    row = lax.broadcasted_iota(jnp.int32, (BLOCK, BLOCK), 0)
    col = lax.broadcasted_iota(jnp.int32, (BLOCK, BLOCK), 1)
    own = row <= col
    no_prev = jnp.where(first_of_seq, NEG, 0.0).astype(F32)

    def scores(n):
        j, g = divmod(n, N_KV_HEADS)
        kcat = kbuf[j * BLOCK:(j + 2) * BLOCK, g * HEAD_DIM:(g + 1) * HEAD_DIM]
        qst = jnp.concatenate(
            [q_buf[j * BLOCK:(j + 1) * BLOCK, (GROUP * g + h) * HEAD_DIM:(GROUP * g + h + 1) * HEAD_DIM]
             for h in range(GROUP)], axis=0)
        return lax.dot_general(kcat, qst, _NT, preferred_element_type=F32)

    def softmax(n, sc):
        j, g = divmod(n, N_KV_HEADS)
        probs, invs = [], []
        for h in range(GROUP):
            sh = sc[:, h * BLOCK:(h + 1) * BLOCK]
            s_prev = sh[:BLOCK]
            if j == 0:
                s_prev = s_prev + no_prev
            sm = jnp.where(own, sh[BLOCK:], s_prev)
            sink = sink_ref[GROUP * g + h]
            m = jnp.maximum(jnp.max(sm, axis=0, keepdims=True), sink)
            e = jnp.exp(sm - m)
            den = jnp.sum(e, axis=0, keepdims=True) + jnp.exp(sink - m)
            invs.append(1.0 / den)
            eb = e.astype(BF)
            zero = jnp.zeros_like(eb)
            probs.append(jnp.concatenate([jnp.where(own, zero, eb), jnp.where(own, eb, zero)], axis=0))
        return jnp.concatenate(probs, axis=1), jnp.concatenate(invs, axis=1)

    heads = {}

    def values(n, probs, inv):
        j, g = divmod(n, N_KV_HEADS)
        vcat = vbuf[g * HEAD_DIM:(g + 1) * HEAD_DIM, j * BLOCK:(j + 2) * BLOCK]
        o = _mm(vcat, probs) * inv
        heads.setdefault(j, []).extend(o[:, h * BLOCK:(h + 1) * BLOCK] for h in range(GROUP))
        if g == N_KV_HEADS - 1:
            rows = slice(j * BLOCK, (j + 1) * BLOCK)
            oa = jnp.concatenate(heads.pop(j), axis=0).T
            mix_buf[rows, 0:ATTN_W] = (oa * sga_buf[rows, :]).astype(BF)

    n_units = (tile // BLOCK) * N_KV_HEADS
    dense_at = {n: n // DENSE_PER_STEP for n in range(n_dense)}
    first_free = -(-n_dense // DENSE_PER_STEP)
    gate_at = {c: first_free + c for c in range(n_out)}
    conv_at = {h: dense_at[n_gate + (h + 1) * n_planes - 1] + 1 for h in range(n_slabs)}
    n_steps = max(n_units + 2, max(gate_at.values()) + 1, max(conv_at.values()) + 1)
    sc_q, sm_q = {}, {}
    for t in range(n_steps):
        if t < n_units:
            sc_q[t] = scores(t)
        for n in range(n_dense):
            if dense_at[n] == t:
                dense_unit(n)
        if 0 <= t - 2 < n_units:
            values(t - 2, *sm_q.pop(t - 2))
        for c in range(n_out):
            if gate_at[c] == t:
                gate_unit(c)
        if 0 <= t - 1 < n_units:
            sm_q[t - 1] = softmax(t - 1, sc_q.pop(t - 1))
        for h in range(n_slabs):
            if conv_at[h] == t:
                conv_unit(h)

    kbuf[0:BLOCK, :] = kbuf[tile:tile + BLOCK, :]
    vbuf[:, 0:BLOCK] = vbuf[:, tile:tile + BLOCK]

    if final:
        xo_ref[0] = _rmsnorm(xo_ref[0], gfin_ref[...])


def _prompt_layer(i, x, p_all, cos, sin_s, sinks, g_norm, w_in, conv_w, w_out, w_pg, w_pp, g_final, final):
    batch, seq, _ = x.shape
    tile = PROMPT_TILE
    n_seq = seq // tile
    n_tiles = batch * n_seq
    in_tile = lambda t: jnp.minimum(t, n_tiles - 1)
    out_tile = lambda t: jnp.maximum(t - 1, 0)
    const = lambda t: (0, 0)
    layer = lambda t: (i, 0, 0)
    resident = dict(pipeline_mode=pl.Buffered(1))
    return pl.pallas_call(
        functools.partial(_prompt_kernel, tile=tile, n_seq=n_seq, final=final),
        name=f"prompt_layer{i}",
        grid=(n_tiles + 1,),
        in_specs=[
            pl.BlockSpec(memory_space=pltpu.SMEM),
            pl.BlockSpec((1, tile, D_MODEL), lambda t: (in_tile(t) // n_seq, in_tile(t) % n_seq, 0)),
            pl.BlockSpec((1, tile, D_MODEL), lambda t: (out_tile(t) // n_seq, out_tile(t) % n_seq, 0)),
            pl.BlockSpec((None, 1, tile, PLE_DIM),
                         lambda t: (i, out_tile(t) // n_seq, out_tile(t) % n_seq, 0)),
            pl.BlockSpec((tile, LANES), lambda t: (in_tile(t) % n_seq, 0)),
            pl.BlockSpec((tile, LANES), lambda t: (in_tile(t) % n_seq, 0)),
            pl.BlockSpec((None, 1, D_MODEL), layer, **resident),
            pl.BlockSpec((None, D_MODEL, IN_W), layer, **resident),
            pl.BlockSpec((None, CONV_K, CONV_W), layer, **resident),
            pl.BlockSpec((None, D_MODEL, D_MODEL), layer, **resident),
            pl.BlockSpec((None, D_MODEL, D_MODEL), layer, **resident),
            pl.BlockSpec((None, PLE_DIM, D_MODEL), layer, **resident),
            pl.BlockSpec((1, D_MODEL), const, **resident),
        ],
        out_specs=[
            pl.BlockSpec((1, tile, D_MODEL), lambda t: (out_tile(t) // n_seq, out_tile(t) % n_seq, 0)),
            pl.BlockSpec((1, BLOCK, KV_W), lambda t: (in_tile(t) // n_seq, 0, 0)),
            pl.BlockSpec((1, BLOCK, KV_W), lambda t: (in_tile(t) // n_seq, 0, 0)),
            pl.BlockSpec((1, CONV_K - 1, CONV_W), lambda t: (in_tile(t) // n_seq, 0, 0)),
        ],
        out_shape=[
            jax.ShapeDtypeStruct((batch, seq, D_MODEL), F32),
            jax.ShapeDtypeStruct((batch, BLOCK, KV_W), F32),
            jax.ShapeDtypeStruct((batch, BLOCK, KV_W), F32),
            jax.ShapeDtypeStruct((batch, CONV_K - 1, CONV_W), F32),
        ],
        scratch_shapes=[
            pltpu.VMEM((tile + BLOCK, KV_W), BF),
            pltpu.VMEM((KV_W, tile + BLOCK), BF),
            pltpu.VMEM((tile + SUBLANES, CONV_W), F32),
            pltpu.VMEM((tile, D_MODEL), BF),
            pltpu.VMEM((tile, D_MODEL), BF),
            pltpu.VMEM((4, tile, CONV_W), F32),
            pltpu.VMEM((tile, ATTN_W), BF),
            pltpu.VMEM((tile, ATTN_W), F32),
            pltpu.VMEM((tile, D_MODEL), F32),
            pltpu.VMEM((tile, D_MODEL), BF),
        ],
        compiler_params=pltpu.CompilerParams(dimension_semantics=("arbitrary",), vmem_limit_bytes=VMEM_LIMIT),
    )(sinks[i], x, x, p_all, cos, sin_s, g_norm, w_in, conv_w, w_out, w_pg, w_pp, g_final)


def _rope_t(x, cos_t, sin_t):
    parts = [x[r:r + ROPE_HALF] for r in range(0, x.shape[0], ROPE_HALF)]
    swapped = jnp.concatenate([parts[n ^ 1] for n in range(len(parts))], axis=0)
    return x * cos_t + swapped * sin_t


def _sample_kernel(sink_ref, x_ref, p_ref, cos_ref, sin_ref, cost_ref, sint_ref, st0_ref, st1_ref, gn_ref, win_ref,
                   cw_ref, wout_ref, wpg_ref, wpp_ref, gfin_ref, ck_ref, cv_ref,
                   xo_ref, nc0_ref, nc1_ref, nk_ref, nv_ref,
                   xs_buf, q_s, kt_s, vt_s, sga_s, mix_s, ubuf, p1_buf, p2_buf, *, rows, t_dec, chunk, nb):
    layer = pl.program_id(0)
    step = pl.program_id(1)
    last_layer = pl.num_programs(0) - 1
    last_step = pl.num_programs(1) - 1
    dec_batch = rows // t_dec
    gr = nb * t_dec
    cols = chunk * t_dec

    @pl.when((layer == 0) & (step == 0))
    def _():
        xs_buf[...] = x_ref[...]

    @pl.when(step == 0)
    def _():
        xn = _rmsnorm(xs_buf[...], gn_ref[...]).astype(BF)
        q_s[...] = (_rope(_mm(xn, win_ref[:, C_Q:C_K]), cos_ref[...], sin_ref[...]) * SCALE).astype(BF)
        kvt = lax.dot_general(win_ref[:, C_K:C_GA], xn, (((0,), (1,)), ((), ())),
                              preferred_element_type=F32)
        kt_s[...] = _rope_t(kvt[0:KV_W], cost_ref[...], sint_ref[...])
        vt_s[...] = kvt[KV_W:2 * KV_W]
        sga_s[...] = _silu(_mm(xn, win_ref[:, C_GA:C_BG]))
        u = _mm(xn, win_ref[:, C_CG:C_HC]) * _mm(xn, win_ref[:, C_HC:C_GC])
        per_batch = lambda first: pl.ds(first, dec_batch, stride=t_dec)
        for c in range(CONV_W // LANES):
            cs = slice(c * LANES, (c + 1) * LANES)
            ubuf[c, 0:SUBLANES, :] = jnp.zeros((SUBLANES, LANES), F32)
            ubuf[c, SUBLANES:SUBLANES + rows, :] = u[:, cs]
            p1_buf[c] = ubuf[c, SUBLANES - 1:SUBLANES - 1 + rows, :]
            p2_buf[c] = ubuf[c, SUBLANES - 2:SUBLANES - 2 + rows, :]
            p1_buf[c, per_batch(0), :] = st1_ref[:, cs]
            p2_buf[c, per_batch(0), :] = st0_ref[:, cs]
            p2_buf[c, per_batch(1), :] = st1_ref[:, cs]
            nc0_ref[:, cs] = ubuf[c, per_batch(SUBLANES + t_dec - 2), :]
            nc1_ref[:, cs] = ubuf[c, per_batch(SUBLANES + t_dec - 1), :]
        u1 = jnp.concatenate([p1_buf[c] for c in range(CONV_W // LANES)], axis=1)
        u2 = jnp.concatenate([p2_buf[c] for c in range(CONV_W // LANES)], axis=1)
        bg = _mm(xn, win_ref[:, C_BG:C_CG])
        gc = _mm(xn, win_ref[:, C_GC:IN_W])
        mix_s[:, ATTN_W:D_MODEL] = _conv_gate(u, u1, u2, cw_ref, bg, gc).astype(BF)

    c0 = pl.multiple_of(step * cols, cols)
    kt_new = kt_s[:, pl.ds(c0, cols)]
    vt_new = vt_s[:, pl.ds(c0, cols)]

    qrow = lax.broadcasted_iota(jnp.int32, (gr, nb * WINDOW), 0)
    ccol = lax.broadcasted_iota(jnp.int32, (gr, nb * WINDOW), 1)
    valid_c = ((ccol // WINDOW) == (qrow // t_dec)) & ((ccol % WINDOW) > (qrow % t_dec))
    qrow_n = lax.broadcasted_iota(jnp.int32, (gr, gr), 0)
    ncol = lax.broadcasted_iota(jnp.int32, (gr, gr), 1)
    valid_n = ((ncol // t_dec) == (qrow_n // t_dec)) & ((ncol % t_dec) <= (qrow_n % t_dec))
    lane = lax.broadcasted_iota(jnp.int32, (KV_W, WINDOW), 1)
    kept = lane < WINDOW - t_dec

    for sg in range(chunk // nb):
        r0 = pl.multiple_of(step * cols + sg * gr, gr)
        qg = q_s[pl.ds(r0, gr), :]
        heads = []
        for g in range(N_KV_HEADS):
            hd = slice(g * HEAD_DIM, (g + 1) * HEAD_DIM)
            ktc = jnp.concatenate([ck_ref[sg * nb + b, hd, :] for b in range(nb)], axis=1).astype(BF)
            vtc = jnp.concatenate([cv_ref[sg * nb + b, hd, :] for b in range(nb)], axis=1).astype(BF)
            ktn = kt_new[hd, sg * gr:(sg + 1) * gr].astype(BF)
            vtn = vt_new[hd, sg * gr:(sg + 1) * gr].astype(BF)
            qst = jnp.concatenate(
                [qg[:, (GROUP * g + h) * HEAD_DIM:(GROUP * g + h + 1) * HEAD_DIM] for h in range(GROUP)], axis=0)
            sc = _mm(qst, ktc)
            sn = _mm(qst, ktn)
            ecs, ens, invs = [], [], []
            for h in range(GROUP):
                scm = jnp.where(valid_c, sc[h * gr:(h + 1) * gr], NEG)
                snm = jnp.where(valid_n, sn[h * gr:(h + 1) * gr], NEG)
                sink = sink_ref[layer, GROUP * g + h]
                m = jnp.maximum(jnp.maximum(jnp.max(scm, axis=-1, keepdims=True),
                                            jnp.max(snm, axis=-1, keepdims=True)), sink)
                ec = jnp.exp(scm - m)
                en = jnp.exp(snm - m)
                den = (jnp.sum(ec, axis=-1, keepdims=True) + jnp.sum(en, axis=-1, keepdims=True)
                       + jnp.exp(sink - m))
                invs.append(1.0 / den)
                ecs.append(ec.astype(BF))
                ens.append(en.astype(BF))
            o = (lax.dot_general(jnp.concatenate(ecs, axis=0), vtc, _NT, preferred_element_type=F32)
                 + lax.dot_general(jnp.concatenate(ens, axis=0), vtn, _NT, preferred_element_type=F32))
            for h in range(GROUP):
                heads.append(o[h * gr:(h + 1) * gr] * invs[h])
        oa = jnp.concatenate(heads, axis=1)
        mix_s[pl.ds(r0, gr), 0:ATTN_W] = (oa * sga_s[pl.ds(r0, gr), :]).astype(BF)

        for b in range(sg * nb, (sg + 1) * nb):
            place = (WINDOW - t_dec - t_dec * b) % cols
            k_in = pltpu.roll(kt_new, place, 1) if place else kt_new
            v_in = pltpu.roll(vt_new, place, 1) if place else vt_new
            nk_ref[b] = jnp.where(kept, pltpu.roll(ck_ref[b], WINDOW - t_dec, 1), k_in)
            nv_ref[b] = jnp.where(kept, pltpu.roll(cv_ref[b], WINDOW - t_dec, 1), v_in)

    @pl.when(step == last_step)
    def _():
        x1 = xs_buf[...] + _mm(mix_s[...], wout_ref[...])
        gate = _sigmoid(_mm(x1.astype(BF), wpg_ref[...]))
        x2 = x1 + gate * _mm(p_ref[...].astype(BF), wpp_ref[...])
        xs_buf[...] = x2

        @pl.when(layer == last_layer)
        def _():
            xo_ref[...] = _rmsnorm(x2, gfin_ref[...])


def _sample_trunk(x, p_all, cos, sin_s, cos_t, sin_t, st0, st1, sinks, g_norm, w_in, conv_w, w_out, w_pg, w_pp,
                  g_final, cache_kt, cache_vt, t_dec):
    rows = x.shape[0]
    dec_batch = rows // t_dec
    chunk = LANES // t_dec
    nb = SAMPLE_GROUP
    const = lambda l, s: (0, 0)
    layer = lambda l, s: (l, 0, 0)
    once = dict(pipeline_mode=pl.Buffered(1))
    full = lambda shape: pl.BlockSpec(shape, const, **once)
    per_layer = lambda *shape: pl.BlockSpec((None,) + shape, layer, **once)
    cache = pl.BlockSpec((None, chunk, KV_W, WINDOW), lambda l, s: (l, s, 0, 0))
    return pl.pallas_call(
        functools.partial(_sample_kernel, rows=rows, t_dec=t_dec, chunk=chunk, nb=nb),
        name="sample_trunk",
        grid=(DEPTH, dec_batch // chunk),
        in_specs=[
            pl.BlockSpec(memory_space=pltpu.SMEM),
            full((rows, D_MODEL)),
            per_layer(rows, PLE_DIM),
            full((rows, LANES)),
            full((rows, LANES)),
            full((KV_W, rows)),
            full((KV_W, rows)),
            per_layer(dec_batch, CONV_W),
            per_layer(dec_batch, CONV_W),
            per_layer(1, D_MODEL),
            per_layer(D_MODEL, IN_W),
            per_layer(CONV_K, CONV_W),
            per_layer(D_MODEL, D_MODEL),
            per_layer(D_MODEL, D_MODEL),
            per_layer(PLE_DIM, D_MODEL),
            full((1, D_MODEL)),
            cache,
            cache,
        ],
        out_specs=[
            pl.BlockSpec((rows, D_MODEL), const),
            pl.BlockSpec((None, dec_batch, CONV_W), layer),
            pl.BlockSpec((None, dec_batch, CONV_W), layer),
            cache,
            cache,
        ],
        out_shape=[
            jax.ShapeDtypeStruct((rows, D_MODEL), F32),
            jax.ShapeDtypeStruct((DEPTH, dec_batch, CONV_W), F32),
            jax.ShapeDtypeStruct((DEPTH, dec_batch, CONV_W), F32),
            jax.ShapeDtypeStruct((DEPTH, dec_batch, KV_W, WINDOW), F32),
            jax.ShapeDtypeStruct((DEPTH, dec_batch, KV_W, WINDOW), F32),
        ],
        scratch_shapes=[
            pltpu.VMEM((rows, D_MODEL), F32),
            pltpu.VMEM((rows, ATTN_W), BF),
            pltpu.VMEM((KV_W, rows), F32),
            pltpu.VMEM((KV_W, rows), F32),
            pltpu.VMEM((rows, ATTN_W), F32),
            pltpu.VMEM((rows, D_MODEL), BF),
            pltpu.VMEM((CONV_W // LANES, rows + SUBLANES, LANES), F32),
            pltpu.VMEM((CONV_W // LANES, rows, LANES), F32),
            pltpu.VMEM((CONV_W // LANES, rows, LANES), F32),
        ],
        compiler_params=pltpu.CompilerParams(
            dimension_semantics=("arbitrary", "arbitrary"), vmem_limit_bytes=VMEM_LIMIT),
    )(sinks, x, p_all, cos, sin_s, cos_t, sin_t, st0, st1, g_norm, w_in, conv_w, w_out, w_pg, w_pp, g_final,
      cache_kt, cache_vt)


def _rope_angles(pos):
    inv = ROPE_THETA ** (-jnp.arange(0, HEAD_DIM, 2, dtype=F32) / HEAD_DIM)
    return pos[:, None] * inv[None, :]


def _rope_tables(pos):
    ang = _rope_angles(pos)
    cos = jnp.tile(jnp.cos(ang), (1, 2 * LANES // HEAD_DIM))
    sin = jnp.sin(ang)
    sin_signed = jnp.tile(jnp.concatenate([-sin, sin], axis=-1), (1, LANES // HEAD_DIM))
    return cos, sin_signed


def _rope_tables_t(pos):
    ang = _rope_angles(pos).T
    cos = jnp.tile(jnp.cos(ang), (2 * KV_W // HEAD_DIM, 1))
    sin = jnp.sin(ang)
    sin_signed = jnp.tile(jnp.concatenate([-sin, sin], axis=0), (KV_W // HEAD_DIM, 1))
    return cos, sin_signed


def kernel(x_prompt, x_sample, cache_k, cache_v, state_conv, p_prompt, p_sample, g_norm, w_in, sinks, conv_w,
           w_out, w_pg, w_pp, g_final):
    batch, seq, _ = x_prompt.shape
    dec_batch, t_dec, _ = x_sample.shape
    rows = dec_batch * t_dec

    cos_p, sin_p = _rope_tables(jnp.arange(seq, dtype=F32))
    pos_s = jnp.tile(PAST_LEN + jnp.arange(t_dec, dtype=F32), dec_batch)
    cos_s, sin_s = _rope_tables(pos_s)
    cos_st, sin_st = _rope_tables_t(pos_s)

    w_in_b, w_out_b, w_pg_b, w_pp_b = (w.astype(BF) for w in (w_in, w_out, w_pg, w_pp))
    g_norm3 = g_norm.reshape(DEPTH, 1, D_MODEL)
    g_fin2 = g_final.reshape(1, D_MODEL)
    to_t = lambda c: jnp.transpose(c, (0, 1, 3, 4, 2)).reshape(DEPTH, -1, KV_W, WINDOW)
    from_t = lambda c: jnp.transpose(c.reshape(DEPTH, -1, N_KV_HEADS, HEAD_DIM, WINDOW), (0, 1, 4, 2, 3))

    xp = x_prompt
    nkp, nvp, ncp = [], [], []
    for i in range(DEPTH):
        xp, kt, vt, ut = _prompt_layer(i, xp, p_prompt, cos_p, sin_p, sinks, g_norm3, w_in_b, conv_w, w_out_b,
                                       w_pg_b, w_pp_b, g_fin2, i == DEPTH - 1)
        nkp.append(kt)
        nvp.append(vt)
        ncp.append(ut)

    xs, nc0, nc1, nks, nvs = _sample_trunk(
        x_sample.reshape(rows, D_MODEL), p_sample.reshape(DEPTH, rows, PLE_DIM), cos_s, sin_s, cos_st, sin_st,
        state_conv[:, :, 0], state_conv[:, :, 1], sinks, g_norm3, w_in_b, conv_w, w_out_b, w_pg_b, w_pp_b,
        g_fin2, to_t(cache_k), to_t(cache_v), t_dec)

    return (xp, xs.reshape(dec_batch, t_dec, D_MODEL), from_t(jnp.stack(nkp)), from_t(jnp.stack(nvp)),
            jnp.stack(ncp), from_t(nks), from_t(nvs), jnp.stack([nc0, nc1], axis=2))
```

```python
import functools

import jax
import jax.numpy as jnp
from jax import lax
from jax.experimental import pallas as pl
from jax.experimental.pallas import tpu as pltpu

D_MODEL = 1024
DEPTH = 4
N_HEADS = 8
HEAD_DIM = 64
N_KV_HEADS = 2
GROUP = N_HEADS // N_KV_HEADS
ATTN_W = N_HEADS * HEAD_DIM
KV_W = N_KV_HEADS * HEAD_DIM
CONV_W = D_MODEL - ATTN_W
CONV_K = 3
WINDOW = 128
BLOCK = 128
PLE_DIM = 256
PAST_LEN = 8192
ROPE_THETA = 10000.0
EPS = 1e-6
SCALE = HEAD_DIM ** -0.5
NEG = -1e30
IN_W = 2 * ATTN_W + 2 * KV_W + 4 * CONV_W

C_Q = 0
C_K = ATTN_W
C_V = C_K + KV_W
C_GA = C_V + KV_W
C_BG = C_GA + ATTN_W
C_CG = C_BG + CONV_W
C_HC = C_CG + CONV_W
C_GC = C_HC + CONV_W

LANES = 128
SUBLANES = 8
ROPE_HALF = HEAD_DIM // 2

MXU_N = 256
PROMPT_TILE = 512
DENSE_PER_STEP = 2
SAMPLE_GROUP = 8
VMEM_LIMIT = 56 * 1024 * 1024

BF = jnp.bfloat16
F32 = jnp.float32
_NT = (((1,), (1,)), ((), ()))


def _mm(a, w):
    return jnp.dot(a, w, preferred_element_type=F32)


def _sigmoid(x):
    return 1.0 / (1.0 + jnp.exp(-x))


def _silu(x):
    return x * _sigmoid(x)


def _rmsnorm(x, g):
    r = lax.rsqrt(jnp.mean(x * x, axis=-1, keepdims=True) + EPS)
    return (x * r) * g


def _rope(x, cos, sin_signed):
    rows = x.shape[0]
    lane = lax.broadcasted_iota(jnp.int32, (rows, LANES), 1)
    first_half = (lane % HEAD_DIM) < ROPE_HALF
    outs = []
    for c in range(x.shape[1] // LANES):
        xc = x[:, c * LANES:(c + 1) * LANES]
        rot = jnp.where(first_half, pltpu.roll(xc, LANES - ROPE_HALF, 1), pltpu.roll(xc, ROPE_HALF, 1))
        outs.append(xc * cos + rot * sin_signed)
    return outs[0] if len(outs) == 1 else jnp.concatenate(outs, axis=1)


def _conv_gate(u, u1, u2, cw_ref, bg, gc):
    yc = cw_ref[0:1, :] * u2 + cw_ref[1:2, :] * u1 + cw_ref[2:3, :] * u
    return bg * yc * _silu(gc)


def _prompt_kernel(*refs, **static):
    i = pl.program_id(0)
    n_tiles = pl.num_programs(0) - 1

    @pl.when(i == 0)
    def _():
        _prompt_step(*refs, do_in=True, do_out=False, **static)

    @pl.when((i > 0) & (i < n_tiles))
    def _():
        _prompt_step(*refs, do_in=True, do_out=True, **static)

    @pl.when(i == n_tiles)
    def _():
        _prompt_step(*refs, do_in=False, do_out=True, **static)


def _prompt_step(sink_ref, xa_ref, xb_ref, p_ref, cos_ref, sin_ref, gn_ref, win_ref, cw_ref, wout_ref, wpg_ref,
                 wpp_ref, gfin_ref, xo_ref, kt_ref, vt_ref, ut_ref, kbuf, vbuf, ubuf, mix_buf, xn_buf, zc_buf,
                 q_buf, sga_buf, x1_buf, x1b_buf, *, tile, n_seq, final, do_in, do_out):
    first_of_seq = pl.program_id(0) % n_seq == 0

    if do_in:
        @pl.when(first_of_seq)
        def _():
            kbuf[0:BLOCK, :] = jnp.zeros((BLOCK, KV_W), BF)
            vbuf[:, 0:BLOCK] = jnp.zeros((KV_W, BLOCK), BF)
            ubuf[0:SUBLANES, :] = jnp.zeros((SUBLANES, CONV_W), F32)

    n_out = D_MODEL // MXU_N
    rows_per = tile // n_out
    for c in range(n_out):
        if do_out:
            cs = slice(c * MXU_N, (c + 1) * MXU_N)
            x1 = (xb_ref[0, :, cs] + _mm(mix_buf[:, 0:ATTN_W], wout_ref[0:ATTN_W, cs])
                  + _mm(mix_buf[:, ATTN_W:D_MODEL], wout_ref[ATTN_W:D_MODEL, cs]))
            x1_buf[:, cs] = x1
            x1b_buf[:, cs] = x1.astype(BF)
        if do_in:
            rs = slice(c * rows_per, (c + 1) * rows_per)
            xn_buf[rs, :] = _rmsnorm(xa_ref[0, rs, :], gn_ref[...]).astype(BF)

    pb = p_ref[0].astype(BF) if do_out else None

    def gate_unit(c):
        cs = slice(c * MXU_N, (c + 1) * MXU_N)
        gate = _sigmoid(_mm(x1b_buf[...], wpg_ref[:, cs]))
        xo_ref[0, :, cs] = x1_buf[:, cs] + gate * _mm(pb, wpp_ref[:, cs])

    def finish_out():
        if final:
            xo_ref[0] = _rmsnorm(xo_ref[0], gfin_ref[...])

    if not do_in:
        for c in range(n_out):
            gate_unit(c)
        finish_out()
        return

    cos = cos_ref[...]
    sin_s = sin_ref[...]

    def proj(c0, width=MXU_N):
        return _mm(xn_buf[...], win_ref[:, c0:c0 + width])

    kv = proj(C_K, 2 * KV_W)
    k = _rope(kv[:, 0:KV_W], cos, sin_s)
    kbuf[BLOCK:BLOCK + tile, :] = k.astype(BF)
    vt = kv[:, KV_W:2 * KV_W].T
    vbuf[:, BLOCK:BLOCK + tile] = vt.astype(BF)
    kt_ref[0] = k[tile - BLOCK:, :].T
    vt_ref[0] = vt[:, tile - BLOCK:]
    for c in range(ATTN_W // MXU_N):
        cs = slice(c * MXU_N, (c + 1) * MXU_N)
        q_buf[:, cs] = (_rope(proj(C_Q + c * MXU_N), cos, sin_s) * SCALE).astype(BF)

    conv_cols = (C_CG, C_HC, C_BG, C_GC)
    n_planes = len(conv_cols)
    n_slabs = CONV_W // MXU_N
    n_gate = ATTN_W // MXU_N
    n_dense = n_gate + n_slabs * n_planes

    def dense_unit(n):
        if n < n_gate:
            sga_buf[:, n * MXU_N:(n + 1) * MXU_N] = _silu(proj(C_GA + n * MXU_N))
        else:
            half, plane = divmod(n - n_gate, n_planes)
            zc_buf[plane, :, half * MXU_N:(half + 1) * MXU_N] = proj(conv_cols[plane] + half * MXU_N)

    def conv_unit(half):
        cs = slice(half * MXU_N, (half + 1) * MXU_N)
        u = zc_buf[0, :, cs] * zc_buf[1, :, cs]
        ubuf[SUBLANES:SUBLANES + tile, cs] = u
        u1 = ubuf[SUBLANES - 1:SUBLANES - 1 + tile, cs]
        u2 = ubuf[SUBLANES - 2:SUBLANES - 2 + tile, cs]
        yc = cw_ref[0:1, cs] * u2 + cw_ref[1:2, cs] * u1 + cw_ref[2:3, cs] * u
        mix_buf[:, ATTN_W + half * MXU_N:ATTN_W + (half + 1) * MXU_N] = (
            zc_buf[2, :, cs] * yc * _silu(zc_buf[3, :, cs])).astype(BF)
        ubuf[0:SUBLANES, cs] = ubuf[tile:tile + SUBLANES, cs]
        ut_ref[0, :, cs] = u[tile - (CONV_K - 1):, :]

    row = lax.broadcasted_iota(jnp.int32, (BLOCK, BLOCK), 0)
    col = lax.broadcasted_iota(jnp.int32, (BLOCK, BLOCK), 1)
    own = row <= col
    no_prev = jnp.where(first_of_seq, NEG, 0.0).astype(F32)

    def scores(n):
        j, g = divmod(n, N_KV_HEADS)
        kcat = kbuf[j * BLOCK:(j + 2) * BLOCK, g * HEAD_DIM:(g + 1) * HEAD_DIM]
        qst = jnp.concatenate(
            [q_buf[j * BLOCK:(j + 1) * BLOCK, (GROUP * g + h) * HEAD_DIM:(GROUP * g + h + 1) * HEAD_DIM]
             for h in range(GROUP)], axis=0)
        return lax.dot_general(kcat, qst, _NT, preferred_element_type=F32)

    def softmax(n, sc):
        j, g = divmod(n, N_KV_HEADS)
        probs, invs = [], []
        for h in range(GROUP):
            sh = sc[:, h * BLOCK:(h + 1) * BLOCK]
            s_prev = sh[:BLOCK]
            if j == 0:
                s_prev = s_prev + no_prev
            sm = jnp.where(own, sh[BLOCK:], s_prev)
            sink = sink_ref[GROUP * g + h]
            m = jnp.maximum(jnp.max(sm, axis=0, keepdims=True), sink)
            e = jnp.exp(sm - m)
            den = jnp.sum(e, axis=0, keepdims=True) + jnp.exp(sink - m)
            invs.append(1.0 / den)
            eb = e.astype(BF)
            zero = jnp.zeros_like(eb)
            probs.append(jnp.concatenate([jnp.where(own, zero, eb), jnp.where(own, eb, zero)], axis=0))
        return jnp.concatenate(probs, axis=1), jnp.concatenate(invs, axis=1)

    heads = {}

    def values(n, probs, inv):
        j, g = divmod(n, N_KV_HEADS)
        vcat = vbuf[g * HEAD_DIM:(g + 1) * HEAD_DIM, j * BLOCK:(j + 2) * BLOCK]
        o = _mm(vcat, probs) * inv
        heads.setdefault(j, []).extend(o[:, h * BLOCK:(h + 1) * BLOCK] for h in range(GROUP))
        if g == N_KV_HEADS - 1:
            rows = slice(j * BLOCK, (j + 1) * BLOCK)
            oa = jnp.concatenate(heads.pop(j), axis=0).T
            mix_buf[rows, 0:ATTN_W] = (oa * sga_buf[rows, :]).astype(BF)

    n_units = (tile // BLOCK) * N_KV_HEADS
    dense_at = {n: n // DENSE_PER_STEP for n in range(n_dense)}
    first_free = -(-n_dense // DENSE_PER_STEP)
    gate_at = {c: first_free + c for c in range(n_out)}
    conv_at = {h: dense_at[n_gate + (h + 1) * n_planes - 1] + 1 for h in range(n_slabs)}
    n_steps = max(n_units + 2, max(gate_at.values()) + 1, max(conv_at.values()) + 1)
    sc_q, sm_q = {}, {}
    for t in range(n_steps):
        if t < n_units:
            sc_q[t] = scores(t)
        for n in range(n_dense):
            if dense_at[n] == t:
                dense_unit(n)
        if 0 <= t - 2 < n_units:
            values(t - 2, *sm_q.pop(t - 2))
        for c in range(n_out):
            if do_out and gate_at[c] == t:
                gate_unit(c)
        if 0 <= t - 1 < n_units:
            sm_q[t - 1] = softmax(t - 1, sc_q.pop(t - 1))
        for h in range(n_slabs):
            if conv_at[h] == t:
                conv_unit(h)

    kbuf[0:BLOCK, :] = kbuf[tile:tile + BLOCK, :]
    vbuf[:, 0:BLOCK] = vbuf[:, tile:tile + BLOCK]

    if do_out:
        finish_out()


def _prompt_layer(i, x, p_all, cos, sin_s, sinks, g_norm, w_in, conv_w, w_out, w_pg, w_pp, g_final, final):
    batch, seq, _ = x.shape
    tile = PROMPT_TILE
    n_seq = seq // tile
    n_tiles = batch * n_seq
    in_tile = lambda t: jnp.minimum(t, n_tiles - 1)
    out_tile = lambda t: jnp.maximum(t - 1, 0)
    const = lambda t: (0, 0)
    layer = lambda t: (i, 0, 0)
    resident = dict(pipeline_mode=pl.Buffered(1))
    return pl.pallas_call(
        functools.partial(_prompt_kernel, tile=tile, n_seq=n_seq, final=final),
        name=f"prompt_layer{i}",
        grid=(n_tiles + 1,),
        in_specs=[
            pl.BlockSpec(memory_space=pltpu.SMEM),
            pl.BlockSpec((1, tile, D_MODEL), lambda t: (in_tile(t) // n_seq, in_tile(t) % n_seq, 0)),
            pl.BlockSpec((1, tile, D_MODEL), lambda t: (out_tile(t) // n_seq, out_tile(t) % n_seq, 0)),
            pl.BlockSpec((None, 1, tile, PLE_DIM),
                         lambda t: (i, out_tile(t) // n_seq, out_tile(t) % n_seq, 0)),
            pl.BlockSpec((tile, LANES), lambda t: (in_tile(t) % n_seq, 0)),
            pl.BlockSpec((tile, LANES), lambda t: (in_tile(t) % n_seq, 0)),
            pl.BlockSpec((None, 1, D_MODEL), layer, **resident),
            pl.BlockSpec((None, D_MODEL, IN_W), layer, **resident),
            pl.BlockSpec((None, CONV_K, CONV_W), layer, **resident),
            pl.BlockSpec((None, D_MODEL, D_MODEL), layer, **resident),
            pl.BlockSpec((None, D_MODEL, D_MODEL), layer, **resident),
            pl.BlockSpec((None, PLE_DIM, D_MODEL), layer, **resident),
            pl.BlockSpec((1, D_MODEL), const, **resident),
        ],
        out_specs=[
            pl.BlockSpec((1, tile, D_MODEL), lambda t: (out_tile(t) // n_seq, out_tile(t) % n_seq, 0)),
            pl.BlockSpec((1, BLOCK, KV_W), lambda t: (in_tile(t) // n_seq, 0, 0)),
            pl.BlockSpec((1, BLOCK, KV_W), lambda t: (in_tile(t) // n_seq, 0, 0)),
            pl.BlockSpec((1, CONV_K - 1, CONV_W), lambda t: (in_tile(t) // n_seq, 0, 0)),
        ],
        out_shape=[
            jax.ShapeDtypeStruct((batch, seq, D_MODEL), F32),
            jax.ShapeDtypeStruct((batch, BLOCK, KV_W), F32),
            jax.ShapeDtypeStruct((batch, BLOCK, KV_W), F32),
            jax.ShapeDtypeStruct((batch, CONV_K - 1, CONV_W), F32),
        ],
        scratch_shapes=[
            pltpu.VMEM((tile + BLOCK, KV_W), BF),
            pltpu.VMEM((KV_W, tile + BLOCK), BF),
            pltpu.VMEM((tile + SUBLANES, CONV_W), F32),
            pltpu.VMEM((tile, D_MODEL), BF),
            pltpu.VMEM((tile, D_MODEL), BF),
            pltpu.VMEM((4, tile, CONV_W), F32),
            pltpu.VMEM((tile, ATTN_W), BF),
            pltpu.VMEM((tile, ATTN_W), F32),
            pltpu.VMEM((tile, D_MODEL), F32),
            pltpu.VMEM((tile, D_MODEL), BF),
        ],
        compiler_params=pltpu.CompilerParams(dimension_semantics=("arbitrary",), vmem_limit_bytes=VMEM_LIMIT),
    )(sinks[i], x, x, p_all, cos, sin_s, g_norm, w_in, conv_w, w_out, w_pg, w_pp, g_final)


def _rope_t(x, cos_t, sin_t):
    parts = [x[r:r + ROPE_HALF] for r in range(0, x.shape[0], ROPE_HALF)]
    swapped = jnp.concatenate([parts[n ^ 1] for n in range(len(parts))], axis=0)
    return x * cos_t + swapped * sin_t


def _sample_kernel(sink_ref, x_ref, p_ref, cos_ref, sin_ref, cost_ref, sint_ref, st0_ref, st1_ref, gn_ref, win_ref,
                   cw_ref, wout_ref, wpg_ref, wpp_ref, gfin_ref, ck_ref, cv_ref,
                   xo_ref, nc0_ref, nc1_ref, nk_ref, nv_ref,
                   xs_buf, q_s, kt_s, vt_s, sga_s, mix_s, ubuf, p1_buf, p2_buf, *, rows, t_dec, chunk, nb):
    layer = pl.program_id(0)
    step = pl.program_id(1)
    last_layer = pl.num_programs(0) - 1
    last_step = pl.num_programs(1) - 1
    dec_batch = rows // t_dec
    gr = nb * t_dec
    cols = chunk * t_dec

    @pl.when((layer == 0) & (step == 0))
    def _():
        xs_buf[...] = x_ref[...]

    @pl.when(step == 0)
    def _():
        xn = _rmsnorm(xs_buf[...], gn_ref[...]).astype(BF)
        q_s[...] = (_rope(_mm(xn, win_ref[:, C_Q:C_K]), cos_ref[...], sin_ref[...]) * SCALE).astype(BF)
        kvt = lax.dot_general(win_ref[:, C_K:C_GA], xn, (((0,), (1,)), ((), ())),
                              preferred_element_type=F32)
        kt_s[...] = _rope_t(kvt[0:KV_W], cost_ref[...], sint_ref[...])
        vt_s[...] = kvt[KV_W:2 * KV_W]
        sga_s[...] = _silu(_mm(xn, win_ref[:, C_GA:C_BG]))
        u = _mm(xn, win_ref[:, C_CG:C_HC]) * _mm(xn, win_ref[:, C_HC:C_GC])
        per_batch = lambda first: pl.ds(first, dec_batch, stride=t_dec)
        for c in range(CONV_W // LANES):
            cs = slice(c * LANES, (c + 1) * LANES)
            ubuf[c, 0:SUBLANES, :] = jnp.zeros((SUBLANES, LANES), F32)
            ubuf[c, SUBLANES:SUBLANES + rows, :] = u[:, cs]
            p1_buf[c] = ubuf[c, SUBLANES - 1:SUBLANES - 1 + rows, :]
            p2_buf[c] = ubuf[c, SUBLANES - 2:SUBLANES - 2 + rows, :]
            p1_buf[c, per_batch(0), :] = st1_ref[:, cs]
            p2_buf[c, per_batch(0), :] = st0_ref[:, cs]
            p2_buf[c, per_batch(1), :] = st1_ref[:, cs]
            nc0_ref[:, cs] = ubuf[c, per_batch(SUBLANES + t_dec - 2), :]
            nc1_ref[:, cs] = ubuf[c, per_batch(SUBLANES + t_dec - 1), :]
        u1 = jnp.concatenate([p1_buf[c] for c in range(CONV_W // LANES)], axis=1)
        u2 = jnp.concatenate([p2_buf[c] for c in range(CONV_W // LANES)], axis=1)
        bg = _mm(xn, win_ref[:, C_BG:C_CG])
        gc = _mm(xn, win_ref[:, C_GC:IN_W])
        mix_s[:, ATTN_W:D_MODEL] = _conv_gate(u, u1, u2, cw_ref, bg, gc).astype(BF)

    c0 = pl.multiple_of(step * cols, cols)
    kt_new = kt_s[:, pl.ds(c0, cols)]
    vt_new = vt_s[:, pl.ds(c0, cols)]

    qrow = lax.broadcasted_iota(jnp.int32, (gr, nb * WINDOW), 0)
    ccol = lax.broadcasted_iota(jnp.int32, (gr, nb * WINDOW), 1)
    valid_c = ((ccol // WINDOW) == (qrow // t_dec)) & ((ccol % WINDOW) > (qrow % t_dec))
    qrow_n = lax.broadcasted_iota(jnp.int32, (gr, gr), 0)
    ncol = lax.broadcasted_iota(jnp.int32, (gr, gr), 1)
    valid_n = ((ncol // t_dec) == (qrow_n // t_dec)) & ((ncol % t_dec) <= (qrow_n % t_dec))
    lane = lax.broadcasted_iota(jnp.int32, (KV_W, WINDOW), 1)
    kept = lane < WINDOW - t_dec

    for sg in range(chunk // nb):
        r0 = pl.multiple_of(step * cols + sg * gr, gr)
        qg = q_s[pl.ds(r0, gr), :]
        heads = []
        for g in range(N_KV_HEADS):
            hd = slice(g * HEAD_DIM, (g + 1) * HEAD_DIM)
            ktc = jnp.concatenate([ck_ref[sg * nb + b, hd, :] for b in range(nb)], axis=1).astype(BF)
            vtc = jnp.concatenate([cv_ref[sg * nb + b, hd, :] for b in range(nb)], axis=1).astype(BF)
            ktn = kt_new[hd, sg * gr:(sg + 1) * gr].astype(BF)
            vtn = vt_new[hd, sg * gr:(sg + 1) * gr].astype(BF)
            qst = jnp.concatenate(
                [qg[:, (GROUP * g + h) * HEAD_DIM:(GROUP * g + h + 1) * HEAD_DIM] for h in range(GROUP)], axis=0)
            sc = _mm(qst, ktc)
            sn = _mm(qst, ktn)
            ecs, ens, invs = [], [], []
            for h in range(GROUP):
                scm = jnp.where(valid_c, sc[h * gr:(h + 1) * gr], NEG)
                snm = jnp.where(valid_n, sn[h * gr:(h + 1) * gr], NEG)
                sink = sink_ref[layer, GROUP * g + h]
                m = jnp.maximum(jnp.maximum(jnp.max(scm, axis=-1, keepdims=True),
                                            jnp.max(snm, axis=-1, keepdims=True)), sink)
                ec = jnp.exp(scm - m)
                en = jnp.exp(snm - m)
                den = (jnp.sum(ec, axis=-1, keepdims=True) + jnp.sum(en, axis=-1, keepdims=True)
                       + jnp.exp(sink - m))
                invs.append(1.0 / den)
                ecs.append(ec.astype(BF))
                ens.append(en.astype(BF))
            o = (lax.dot_general(jnp.concatenate(ecs, axis=0), vtc, _NT, preferred_element_type=F32)
                 + lax.dot_general(jnp.concatenate(ens, axis=0), vtn, _NT, preferred_element_type=F32))
            for h in range(GROUP):
                heads.append(o[h * gr:(h + 1) * gr] * invs[h])
        oa = jnp.concatenate(heads, axis=1)
        mix_s[pl.ds(r0, gr), 0:ATTN_W] = (oa * sga_s[pl.ds(r0, gr), :]).astype(BF)

        for b in range(sg * nb, (sg + 1) * nb):
            place = (WINDOW - t_dec - t_dec * b) % cols
            k_in = pltpu.roll(kt_new, place, 1) if place else kt_new
            v_in = pltpu.roll(vt_new, place, 1) if place else vt_new
            nk_ref[b] = jnp.where(kept, pltpu.roll(ck_ref[b], WINDOW - t_dec, 1), k_in)
            nv_ref[b] = jnp.where(kept, pltpu.roll(cv_ref[b], WINDOW - t_dec, 1), v_in)

    @pl.when(step == last_step)
    def _():
        x1 = xs_buf[...] + _mm(mix_s[...], wout_ref[...])
        gate = _sigmoid(_mm(x1.astype(BF), wpg_ref[...]))
        x2 = x1 + gate * _mm(p_ref[...].astype(BF), wpp_ref[...])
        xs_buf[...] = x2

        @pl.when(layer == last_layer)
        def _():
            xo_ref[...] = _rmsnorm(x2, gfin_ref[...])


def _sample_trunk(x, p_all, cos, sin_s, cos_t, sin_t, st0, st1, sinks, g_norm, w_in, conv_w, w_out, w_pg, w_pp,
                  g_final, cache_kt, cache_vt, t_dec):
    rows = x.shape[0]
    dec_batch = rows // t_dec
    chunk = LANES // t_dec
    nb = SAMPLE_GROUP
    const = lambda l, s: (0, 0)
    layer = lambda l, s: (l, 0, 0)
    once = dict(pipeline_mode=pl.Buffered(1))
    full = lambda shape: pl.BlockSpec(shape, const, **once)
    per_layer = lambda *shape: pl.BlockSpec((None,) + shape, layer, **once)
    cache = pl.BlockSpec((None, chunk, KV_W, WINDOW), lambda l, s: (l, s, 0, 0))
    return pl.pallas_call(
        functools.partial(_sample_kernel, rows=rows, t_dec=t_dec, chunk=chunk, nb=nb),
        name="sample_trunk",
        grid=(DEPTH, dec_batch // chunk),
        in_specs=[
            pl.BlockSpec(memory_space=pltpu.SMEM),
            full((rows, D_MODEL)),
            per_layer(rows, PLE_DIM),
            full((rows, LANES)),
            full((rows, LANES)),
            full((KV_W, rows)),
            full((KV_W, rows)),
            per_layer(dec_batch, CONV_W),
            per_layer(dec_batch, CONV_W),
            per_layer(1, D_MODEL),
            per_layer(D_MODEL, IN_W),
            per_layer(CONV_K, CONV_W),
            per_layer(D_MODEL, D_MODEL),
            per_layer(D_MODEL, D_MODEL),
            per_layer(PLE_DIM, D_MODEL),
            full((1, D_MODEL)),
            cache,
            cache,
        ],
        out_specs=[
            pl.BlockSpec((rows, D_MODEL), const),
            pl.BlockSpec((None, dec_batch, CONV_W), layer),
            pl.BlockSpec((None, dec_batch, CONV_W), layer),
            cache,
            cache,
        ],
        out_shape=[
            jax.ShapeDtypeStruct((rows, D_MODEL), F32),
            jax.ShapeDtypeStruct((DEPTH, dec_batch, CONV_W), F32),
            jax.ShapeDtypeStruct((DEPTH, dec_batch, CONV_W), F32),
            jax.ShapeDtypeStruct((DEPTH, dec_batch, KV_W, WINDOW), F32),
            jax.ShapeDtypeStruct((DEPTH, dec_batch, KV_W, WINDOW), F32),
        ],
        scratch_shapes=[
            pltpu.VMEM((rows, D_MODEL), F32),
            pltpu.VMEM((rows, ATTN_W), BF),
            pltpu.VMEM((KV_W, rows), F32),
            pltpu.VMEM((KV_W, rows), F32),
            pltpu.VMEM((rows, ATTN_W), F32),
            pltpu.VMEM((rows, D_MODEL), BF),
            pltpu.VMEM((CONV_W // LANES, rows + SUBLANES, LANES), F32),
            pltpu.VMEM((CONV_W // LANES, rows, LANES), F32),
            pltpu.VMEM((CONV_W // LANES, rows, LANES), F32),
        ],
        compiler_params=pltpu.CompilerParams(
            dimension_semantics=("arbitrary", "arbitrary"), vmem_limit_bytes=VMEM_LIMIT),
    )(sinks, x, p_all, cos, sin_s, cos_t, sin_t, st0, st1, g_norm, w_in, conv_w, w_out, w_pg, w_pp, g_final,
      cache_kt, cache_vt)


def _rope_angles(pos):
    inv = ROPE_THETA ** (-jnp.arange(0, HEAD_DIM, 2, dtype=F32) / HEAD_DIM)
    return pos[:, None] * inv[None, :]


def _rope_tables(pos):
    ang = _rope_angles(pos)
    cos = jnp.tile(jnp.cos(ang), (1, 2 * LANES // HEAD_DIM))
    sin = jnp.sin(ang)
    sin_signed = jnp.tile(jnp.concatenate([-sin, sin], axis=-1), (1, LANES // HEAD_DIM))
    return cos, sin_signed


def _rope_tables_t(pos):
    ang = _rope_angles(pos).T
    cos = jnp.tile(jnp.cos(ang), (2 * KV_W // HEAD_DIM, 1))
    sin = jnp.sin(ang)
    sin_signed = jnp.tile(jnp.concatenate([-sin, sin], axis=0), (KV_W // HEAD_DIM, 1))
    return cos, sin_signed


def kernel(x_prompt, x_sample, cache_k, cache_v, state_conv, p_prompt, p_sample, g_norm, w_in, sinks, conv_w,
           w_out, w_pg, w_pp, g_final):
    batch, seq, _ = x_prompt.shape
    dec_batch, t_dec, _ = x_sample.shape
    rows = dec_batch * t_dec

    cos_p, sin_p = _rope_tables(jnp.arange(seq, dtype=F32))
    pos_s = jnp.tile(PAST_LEN + jnp.arange(t_dec, dtype=F32), dec_batch)
    cos_s, sin_s = _rope_tables(pos_s)
    cos_st, sin_st = _rope_tables_t(pos_s)

    w_in_b, w_out_b, w_pg_b, w_pp_b = (w.astype(BF) for w in (w_in, w_out, w_pg, w_pp))
    g_norm3 = g_norm.reshape(DEPTH, 1, D_MODEL)
    g_fin2 = g_final.reshape(1, D_MODEL)
    to_t = lambda c: jnp.transpose(c, (0, 1, 3, 4, 2)).reshape(DEPTH, -1, KV_W, WINDOW)
    from_t = lambda c: jnp.transpose(c.reshape(DEPTH, -1, N_KV_HEADS, HEAD_DIM, WINDOW), (0, 1, 4, 2, 3))

    xp = x_prompt
    nkp, nvp, ncp = [], [], []
    for i in range(DEPTH):
        xp, kt, vt, ut = _prompt_layer(i, xp, p_prompt, cos_p, sin_p, sinks, g_norm3, w_in_b, conv_w, w_out_b,
                                       w_pg_b, w_pp_b, g_fin2, i == DEPTH - 1)
        nkp.append(kt)
        nvp.append(vt)
        ncp.append(ut)

    xs, nc0, nc1, nks, nvs = _sample_trunk(
        x_sample.reshape(rows, D_MODEL), p_sample.reshape(DEPTH, rows, PLE_DIM), cos_s, sin_s, cos_st, sin_st,
        state_conv[:, :, 0], state_conv[:, :, 1], sinks, g_norm3, w_in_b, conv_w, w_out_b, w_pg_b, w_pp_b,
        g_fin2, to_t(cache_k), to_t(cache_v), t_dec)

    return (xp, xs.reshape(dec_batch, t_dec, D_MODEL), from_t(jnp.stack(nkp)), from_t(jnp.stack(nvp)),
            jnp.stack(ncp), from_t(nks), from_t(nvs), jnp.stack([nc0, nc1], axis=2))
```

```python
import functools

import jax
import jax.numpy as jnp
from jax import lax
from jax.experimental import pallas as pl
from jax.experimental.pallas import tpu as pltpu

D_MODEL = 1024
DEPTH = 4
N_HEADS = 8
HEAD_DIM = 64
N_KV_HEADS = 2
GROUP = N_HEADS // N_KV_HEADS
ATTN_W = N_HEADS * HEAD_DIM
KV_W = N_KV_HEADS * HEAD_DIM
CONV_W = D_MODEL - ATTN_W
CONV_K = 3
WINDOW = 128
BLOCK = 128
PLE_DIM = 256
PAST_LEN = 8192
ROPE_THETA = 10000.0
EPS = 1e-6
SCALE = HEAD_DIM ** -0.5
NEG = -1e30
IN_W = 2 * ATTN_W + 2 * KV_W + 4 * CONV_W

C_Q = 0
C_K = ATTN_W
C_V = C_K + KV_W
C_GA = C_V + KV_W
C_BG = C_GA + ATTN_W
C_CG = C_BG + CONV_W
C_HC = C_CG + CONV_W
C_GC = C_HC + CONV_W

LANES = 128
SUBLANES = 8
ROPE_HALF = HEAD_DIM // 2

MXU_N = 256
PROMPT_TILE = 512
DENSE_PER_STEP = 1
GATE_FIRST_STEP = 1
GATE_EVERY = 2
SAMPLE_GROUP = 8
VMEM_LIMIT = 56 * 1024 * 1024

BF = jnp.bfloat16
F32 = jnp.float32
_NT = (((1,), (1,)), ((), ()))


def _mm(a, w):
    return jnp.dot(a, w, preferred_element_type=F32)


def _sigmoid(x):
    return 1.0 / (1.0 + jnp.exp(-x))


def _silu(x):
    return x * _sigmoid(x)


def _rmsnorm(x, g):
    r = lax.rsqrt(jnp.mean(x * x, axis=-1, keepdims=True) + EPS)
    return (x * r) * g


def _rope(x, cos, sin_signed):
    rows = x.shape[0]
    lane = lax.broadcasted_iota(jnp.int32, (rows, LANES), 1)
    first_half = (lane % HEAD_DIM) < ROPE_HALF
    outs = []
    for c in range(x.shape[1] // LANES):
        xc = x[:, c * LANES:(c + 1) * LANES]
        rot = jnp.where(first_half, pltpu.roll(xc, LANES - ROPE_HALF, 1), pltpu.roll(xc, ROPE_HALF, 1))
        outs.append(xc * cos + rot * sin_signed)
    return outs[0] if len(outs) == 1 else jnp.concatenate(outs, axis=1)


def _conv_gate(u, u1, u2, cw_ref, bg, gc):
    yc = cw_ref[0:1, :] * u2 + cw_ref[1:2, :] * u1 + cw_ref[2:3, :] * u
    return bg * yc * _silu(gc)


def _prompt_kernel(*refs, **static):
    i = pl.program_id(0)
    n_tiles = pl.num_programs(0) - 1

    @pl.when(i == 0)
    def _():
        _prompt_step(*refs, do_in=True, do_out=False, **static)

    @pl.when((i > 0) & (i < n_tiles))
    def _():
        _prompt_step(*refs, do_in=True, do_out=True, **static)

    @pl.when(i == n_tiles)
    def _():
        _prompt_step(*refs, do_in=False, do_out=True, **static)


def _prompt_step(sink_ref, xa_ref, xb_ref, p_ref, cos_ref, sin_ref, gn_ref, win_ref, cw_ref, wout_ref, wpg_ref,
                 wpp_ref, gfin_ref, xo_ref, kt_ref, vt_ref, ut_ref, kbuf, vbuf, ubuf, mix_buf, xn_buf, zc_buf,
                 q_buf, sga_buf, x1_buf, x1b_buf, *, tile, n_seq, final, do_in, do_out):
    first_of_seq = pl.program_id(0) % n_seq == 0

    if do_in:
        @pl.when(first_of_seq)
        def _():
            kbuf[0:BLOCK, :] = jnp.zeros((BLOCK, KV_W), BF)
            vbuf[0:BLOCK, :] = jnp.zeros((BLOCK, KV_W), BF)
            ubuf[0:SUBLANES, :] = jnp.zeros((SUBLANES, CONV_W), F32)

    n_out = D_MODEL // MXU_N
    rows_per = tile // n_out
    for c in range(n_out):
        if do_out:
            cs = slice(c * MXU_N, (c + 1) * MXU_N)
            x1 = (xb_ref[0, :, cs] + _mm(mix_buf[:, 0:ATTN_W], wout_ref[0:ATTN_W, cs])
                  + _mm(mix_buf[:, ATTN_W:D_MODEL], wout_ref[ATTN_W:D_MODEL, cs]))
            x1_buf[:, cs] = x1
            x1b_buf[:, cs] = x1.astype(BF)
        if do_in:
            rs = slice(c * rows_per, (c + 1) * rows_per)
            xn_buf[rs, :] = _rmsnorm(xa_ref[0, rs, :], gn_ref[...]).astype(BF)

    pb = p_ref[0].astype(BF) if do_out else None

    def gate_unit(c):
        cs = slice(c * MXU_N, (c + 1) * MXU_N)
        gate = _sigmoid(_mm(x1b_buf[...], wpg_ref[:, cs]))
        xo_ref[0, :, cs] = x1_buf[:, cs] + gate * _mm(pb, wpp_ref[:, cs])

    def finish_out():
        if final:
            xo_ref[0] = _rmsnorm(xo_ref[0], gfin_ref[...])

    if not do_in:
        for c in range(n_out):
            gate_unit(c)
        finish_out()
        return

    cos = cos_ref[...]
    sin_s = sin_ref[...]

    def proj(c0, width=MXU_N):
        return _mm(xn_buf[...], win_ref[:, c0:c0 + width])

    kv = proj(C_K, 2 * KV_W)
    k = _rope(kv[:, 0:KV_W], cos, sin_s)
    kbuf[BLOCK:BLOCK + tile, :] = k.astype(BF)
    v = kv[:, KV_W:2 * KV_W]
    vbuf[BLOCK:BLOCK + tile, :] = v.astype(BF)
    kt_ref[0] = k[tile - BLOCK:, :].T
    vt_ref[0] = v[tile - BLOCK:, :].T
    for c in range(ATTN_W // MXU_N):
        cs = slice(c * MXU_N, (c + 1) * MXU_N)
        q_buf[:, cs] = (_rope(proj(C_Q + c * MXU_N), cos, sin_s) * SCALE).astype(BF)

    conv_cols = (C_CG, C_HC, C_BG, C_GC)
    n_planes = len(conv_cols)
    n_slabs = CONV_W // MXU_N
    n_gate = ATTN_W // MXU_N
    n_dense = n_gate + n_slabs * n_planes

    def dense_unit(n):
        if n < n_gate:
            sga_buf[:, n * MXU_N:(n + 1) * MXU_N] = _silu(proj(C_GA + n * MXU_N))
        else:
            half, plane = divmod(n - n_gate, n_planes)
            zc_buf[plane, :, half * MXU_N:(half + 1) * MXU_N] = proj(conv_cols[plane] + half * MXU_N)

    def conv_unit(half):
        cs = slice(half * MXU_N, (half + 1) * MXU_N)
        u = zc_buf[0, :, cs] * zc_buf[1, :, cs]
        ubuf[SUBLANES:SUBLANES + tile, cs] = u
        u1 = ubuf[SUBLANES - 1:SUBLANES - 1 + tile, cs]
        u2 = ubuf[SUBLANES - 2:SUBLANES - 2 + tile, cs]
        yc = cw_ref[0:1, cs] * u2 + cw_ref[1:2, cs] * u1 + cw_ref[2:3, cs] * u
        mix_buf[:, ATTN_W + half * MXU_N:ATTN_W + (half + 1) * MXU_N] = (
            zc_buf[2, :, cs] * yc * _silu(zc_buf[3, :, cs])).astype(BF)
        ubuf[0:SUBLANES, cs] = ubuf[tile:tile + SUBLANES, cs]
        ut_ref[0, :, cs] = u[tile - (CONV_K - 1):, :]

    row = lax.broadcasted_iota(jnp.int32, (BLOCK, BLOCK), 0)
    col = lax.broadcasted_iota(jnp.int32, (BLOCK, BLOCK), 1)
    own = col <= row
    no_prev = jnp.where(first_of_seq, NEG, 0.0).astype(F32)

    def scores(n):
        j, g = divmod(n, N_KV_HEADS)
        kcat = kbuf[j * BLOCK:(j + 2) * BLOCK, g * HEAD_DIM:(g + 1) * HEAD_DIM]
        qst = jnp.concatenate(
            [q_buf[j * BLOCK:(j + 1) * BLOCK, (GROUP * g + h) * HEAD_DIM:(GROUP * g + h + 1) * HEAD_DIM]
             for h in range(GROUP)], axis=0)
        return lax.dot_general(qst, kcat, _NT, preferred_element_type=F32)

    def softmax(n, sc):
        j, g = divmod(n, N_KV_HEADS)
        probs, invs = [], []
        for h in range(GROUP):
            sh = sc[h * BLOCK:(h + 1) * BLOCK]
            s_prev = sh[:, :BLOCK]
            if j == 0:
                s_prev = s_prev + no_prev
            sm = jnp.where(own, sh[:, BLOCK:], s_prev)
            sink = sink_ref[GROUP * g + h]
            m = jnp.maximum(jnp.max(sm, axis=-1, keepdims=True), sink)
            e = jnp.exp(sm - m)
            den = jnp.sum(e, axis=-1, keepdims=True) + jnp.exp(sink - m)
            invs.append(1.0 / den)
            eb = e.astype(BF)
            zero = jnp.zeros_like(eb)
            probs.append(jnp.concatenate([jnp.where(own, zero, eb), jnp.where(own, eb, zero)], axis=1))
        return jnp.concatenate(probs, axis=0), invs

    heads = {}

    def values(n, probs, invs):
        j, g = divmod(n, N_KV_HEADS)
        vcat = vbuf[j * BLOCK:(j + 2) * BLOCK, g * HEAD_DIM:(g + 1) * HEAD_DIM]
        o = _mm(probs, vcat)
        heads.setdefault(j, []).extend(o[h * BLOCK:(h + 1) * BLOCK] * invs[h] for h in range(GROUP))
        if g == N_KV_HEADS - 1:
            rows = slice(j * BLOCK, (j + 1) * BLOCK)
            oa = jnp.concatenate(heads.pop(j), axis=1)
            mix_buf[rows, 0:ATTN_W] = (oa * sga_buf[rows, :]).astype(BF)

    n_units = (tile // BLOCK) * N_KV_HEADS
    dense_at = {n: n // DENSE_PER_STEP for n in range(n_dense)}
    gate_at = {c: GATE_FIRST_STEP + GATE_EVERY * c for c in range(n_out)}
    conv_at = {h: dense_at[n_gate + (h + 1) * n_planes - 1] + 1 for h in range(n_slabs)}
    n_steps = max(n_units + 2, max(gate_at.values()) + 1, max(conv_at.values()) + 1)
    sc_q, sm_q = {}, {}
    for t in range(n_steps):
        if t < n_units:
            sc_q[t] = scores(t)
        for n in range(n_dense):
            if dense_at[n] == t:
                dense_unit(n)
        if 0 <= t - 2 < n_units:
            values(t - 2, *sm_q.pop(t - 2))
        for c in range(n_out):
            if do_out and gate_at[c] == t:
                gate_unit(c)
        if 0 <= t - 1 < n_units:
            sm_q[t - 1] = softmax(t - 1, sc_q.pop(t - 1))
        for h in range(n_slabs):
            if conv_at[h] == t:
                conv_unit(h)

    kbuf[0:BLOCK, :] = kbuf[tile:tile + BLOCK, :]
    vbuf[0:BLOCK, :] = vbuf[tile:tile + BLOCK, :]

    if do_out:
        finish_out()


def _prompt_layer(i, x, p_all, cos, sin_s, sinks, g_norm, w_in, conv_w, w_out, w_pg, w_pp, g_final, final):
    batch, seq, _ = x.shape
    tile = PROMPT_TILE
    n_seq = seq // tile
    n_tiles = batch * n_seq
    in_tile = lambda t: jnp.minimum(t, n_tiles - 1)
    out_tile = lambda t: jnp.maximum(t - 1, 0)
    const = lambda t: (0, 0)
    layer = lambda t: (i, 0, 0)
    resident = dict(pipeline_mode=pl.Buffered(1))
    return pl.pallas_call(
        functools.partial(_prompt_kernel, tile=tile, n_seq=n_seq, final=final),
        name=f"prompt_layer{i}",
        grid=(n_tiles + 1,),
        in_specs=[
            pl.BlockSpec(memory_space=pltpu.SMEM),
            pl.BlockSpec((1, tile, D_MODEL), lambda t: (in_tile(t) // n_seq, in_tile(t) % n_seq, 0)),
            pl.BlockSpec((1, tile, D_MODEL), lambda t: (out_tile(t) // n_seq, out_tile(t) % n_seq, 0)),
            pl.BlockSpec((None, 1, tile, PLE_DIM),
                         lambda t: (i, out_tile(t) // n_seq, out_tile(t) % n_seq, 0)),
            pl.BlockSpec((tile, LANES), lambda t: (in_tile(t) % n_seq, 0)),
            pl.BlockSpec((tile, LANES), lambda t: (in_tile(t) % n_seq, 0)),
            pl.BlockSpec((None, 1, D_MODEL), layer, **resident),
            pl.BlockSpec((None, D_MODEL, IN_W), layer, **resident),
            pl.BlockSpec((None, CONV_K, CONV_W), layer, **resident),
            pl.BlockSpec((None, D_MODEL, D_MODEL), layer, **resident),
            pl.BlockSpec((None, D_MODEL, D_MODEL), layer, **resident),
            pl.BlockSpec((None, PLE_DIM, D_MODEL), layer, **resident),
            pl.BlockSpec((1, D_MODEL), const, **resident),
        ],
        out_specs=[
            pl.BlockSpec((1, tile, D_MODEL), lambda t: (out_tile(t) // n_seq, out_tile(t) % n_seq, 0)),
            pl.BlockSpec((1, BLOCK, KV_W), lambda t: (in_tile(t) // n_seq, 0, 0)),
            pl.BlockSpec((1, BLOCK, KV_W), lambda t: (in_tile(t) // n_seq, 0, 0)),
            pl.BlockSpec((1, CONV_K - 1, CONV_W), lambda t: (in_tile(t) // n_seq, 0, 0)),
        ],
        out_shape=[
            jax.ShapeDtypeStruct((batch, seq, D_MODEL), F32),
            jax.ShapeDtypeStruct((batch, BLOCK, KV_W), F32),
            jax.ShapeDtypeStruct((batch, BLOCK, KV_W), F32),
            jax.ShapeDtypeStruct((batch, CONV_K - 1, CONV_W), F32),
        ],
        scratch_shapes=[
            pltpu.VMEM((tile + BLOCK, KV_W), BF),
            pltpu.VMEM((tile + BLOCK, KV_W), BF),
            pltpu.VMEM((tile + SUBLANES, CONV_W), F32),
            pltpu.VMEM((tile, D_MODEL), BF),
            pltpu.VMEM((tile, D_MODEL), BF),
            pltpu.VMEM((4, tile, CONV_W), F32),
            pltpu.VMEM((tile, ATTN_W), BF),
            pltpu.VMEM((tile, ATTN_W), F32),
            pltpu.VMEM((tile, D_MODEL), F32),
            pltpu.VMEM((tile, D_MODEL), BF),
        ],
        compiler_params=pltpu.CompilerParams(dimension_semantics=("arbitrary",), vmem_limit_bytes=VMEM_LIMIT),
    )(sinks[i], x, x, p_all, cos, sin_s, g_norm, w_in, conv_w, w_out, w_pg, w_pp, g_final)


def _rope_t(x, cos_t, sin_t):
    parts = [x[r:r + ROPE_HALF] for r in range(0, x.shape[0], ROPE_HALF)]
    swapped = jnp.concatenate([parts[n ^ 1] for n in range(len(parts))], axis=0)
    return x * cos_t + swapped * sin_t


def _sample_kernel(sink_ref, x_ref, p_ref, cos_ref, sin_ref, cost_ref, sint_ref, st0_ref, st1_ref, gn_ref, win_ref,
                   cw_ref, wout_ref, wpg_ref, wpp_ref, gfin_ref, ck_ref, cv_ref,
                   xo_ref, nc0_ref, nc1_ref, nk_ref, nv_ref,
                   xs_buf, q_s, kt_s, vt_s, sga_s, mix_s, ubuf, p1_buf, p2_buf, *, rows, t_dec, chunk, nb):
    layer = pl.program_id(0)
    step = pl.program_id(1)
    last_layer = pl.num_programs(0) - 1
    last_step = pl.num_programs(1) - 1
    dec_batch = rows // t_dec
    gr = nb * t_dec
    cols = chunk * t_dec

    @pl.when((layer == 0) & (step == 0))
    def _():
        xs_buf[...] = x_ref[...]

    @pl.when(step == 0)
    def _():
        xn = _rmsnorm(xs_buf[...], gn_ref[...]).astype(BF)
        q_s[...] = (_rope(_mm(xn, win_ref[:, C_Q:C_K]), cos_ref[...], sin_ref[...]) * SCALE).astype(BF)
        kvt = lax.dot_general(win_ref[:, C_K:C_GA], xn, (((0,), (1,)), ((), ())),
                              preferred_element_type=F32)
        kt_s[...] = _rope_t(kvt[0:KV_W], cost_ref[...], sint_ref[...])
        vt_s[...] = kvt[KV_W:2 * KV_W]
        sga_s[...] = _silu(_mm(xn, win_ref[:, C_GA:C_BG]))
        u = _mm(xn, win_ref[:, C_CG:C_HC]) * _mm(xn, win_ref[:, C_HC:C_GC])
        per_batch = lambda first: pl.ds(first, dec_batch, stride=t_dec)
        for c in range(CONV_W // LANES):
            cs = slice(c * LANES, (c + 1) * LANES)
            ubuf[c, 0:SUBLANES, :] = jnp.zeros((SUBLANES, LANES), F32)
            ubuf[c, SUBLANES:SUBLANES + rows, :] = u[:, cs]
            p1_buf[c] = ubuf[c, SUBLANES - 1:SUBLANES - 1 + rows, :]
            p2_buf[c] = ubuf[c, SUBLANES - 2:SUBLANES - 2 + rows, :]
            p1_buf[c, per_batch(0), :] = st1_ref[:, cs]
            p2_buf[c, per_batch(0), :] = st0_ref[:, cs]
            p2_buf[c, per_batch(1), :] = st1_ref[:, cs]
            nc0_ref[:, cs] = ubuf[c, per_batch(SUBLANES + t_dec - 2), :]
            nc1_ref[:, cs] = ubuf[c, per_batch(SUBLANES + t_dec - 1), :]
        u1 = jnp.concatenate([p1_buf[c] for c in range(CONV_W // LANES)], axis=1)
        u2 = jnp.concatenate([p2_buf[c] for c in range(CONV_W // LANES)], axis=1)
        bg = _mm(xn, win_ref[:, C_BG:C_CG])
        gc = _mm(xn, win_ref[:, C_GC:IN_W])
        mix_s[:, ATTN_W:D_MODEL] = _conv_gate(u, u1, u2, cw_ref, bg, gc).astype(BF)

    c0 = pl.multiple_of(step * cols, cols)
    kt_new = kt_s[:, pl.ds(c0, cols)]
    vt_new = vt_s[:, pl.ds(c0, cols)]

    qrow = lax.broadcasted_iota(jnp.int32, (gr, nb * WINDOW), 0)
    ccol = lax.broadcasted_iota(jnp.int32, (gr, nb * WINDOW), 1)
    valid_c = ((ccol // WINDOW) == (qrow // t_dec)) & ((ccol % WINDOW) > (qrow % t_dec))
    qrow_n = lax.broadcasted_iota(jnp.int32, (gr, gr), 0)
    ncol = lax.broadcasted_iota(jnp.int32, (gr, gr), 1)
    valid_n = ((ncol // t_dec) == (qrow_n // t_dec)) & ((ncol % t_dec) <= (qrow_n % t_dec))
    lane = lax.broadcasted_iota(jnp.int32, (KV_W, WINDOW), 1)
    kept = lane < WINDOW - t_dec

    for sg in range(chunk // nb):
        r0 = pl.multiple_of(step * cols + sg * gr, gr)
        qg = q_s[pl.ds(r0, gr), :]
        heads = []
        for g in range(N_KV_HEADS):
            hd = slice(g * HEAD_DIM, (g + 1) * HEAD_DIM)
            ktc = jnp.concatenate([ck_ref[sg * nb + b, hd, :] for b in range(nb)], axis=1).astype(BF)
            vtc = jnp.concatenate([cv_ref[sg * nb + b, hd, :] for b in range(nb)], axis=1).astype(BF)
            ktn = kt_new[hd, sg * gr:(sg + 1) * gr].astype(BF)
            vtn = vt_new[hd, sg * gr:(sg + 1) * gr].astype(BF)
            qst = jnp.concatenate(
                [qg[:, (GROUP * g + h) * HEAD_DIM:(GROUP * g + h + 1) * HEAD_DIM] for h in range(GROUP)], axis=0)
            sc = _mm(qst, ktc)
            sn = _mm(qst, ktn)
            ecs, ens, invs = [], [], []
            for h in range(GROUP):
                scm = jnp.where(valid_c, sc[h * gr:(h + 1) * gr], NEG)
                snm = jnp.where(valid_n, sn[h * gr:(h + 1) * gr], NEG)
                sink = sink_ref[layer, GROUP * g + h]
                m = jnp.maximum(jnp.maximum(jnp.max(scm, axis=-1, keepdims=True),
                                            jnp.max(snm, axis=-1, keepdims=True)), sink)
                ec = jnp.exp(scm - m)
                en = jnp.exp(snm - m)
                den = (jnp.sum(ec, axis=-1, keepdims=True) + jnp.sum(en, axis=-1, keepdims=True)
                       + jnp.exp(sink - m))
                invs.append(1.0 / den)
                ecs.append(ec.astype(BF))
                ens.append(en.astype(BF))
            o = (lax.dot_general(jnp.concatenate(ecs, axis=0), vtc, _NT, preferred_element_type=F32)
                 + lax.dot_general(jnp.concatenate(ens, axis=0), vtn, _NT, preferred_element_type=F32))
            for h in range(GROUP):
                heads.append(o[h * gr:(h + 1) * gr] * invs[h])
        oa = jnp.concatenate(heads, axis=1)
        mix_s[pl.ds(r0, gr), 0:ATTN_W] = (oa * sga_s[pl.ds(r0, gr), :]).astype(BF)

        for b in range(sg * nb, (sg + 1) * nb):
            place = (WINDOW - t_dec - t_dec * b) % cols
            k_in = pltpu.roll(kt_new, place, 1) if place else kt_new
            v_in = pltpu.roll(vt_new, place, 1) if place else vt_new
            nk_ref[b] = jnp.where(kept, pltpu.roll(ck_ref[b], WINDOW - t_dec, 1), k_in)
            nv_ref[b] = jnp.where(kept, pltpu.roll(cv_ref[b], WINDOW - t_dec, 1), v_in)

    @pl.when(step == last_step)
    def _():
        x1 = xs_buf[...] + _mm(mix_s[...], wout_ref[...])
        gate = _sigmoid(_mm(x1.astype(BF), wpg_ref[...]))
        x2 = x1 + gate * _mm(p_ref[...].astype(BF), wpp_ref[...])
        xs_buf[...] = x2

        @pl.when(layer == last_layer)
        def _():
            xo_ref[...] = _rmsnorm(x2, gfin_ref[...])


def _sample_trunk(x, p_all, cos, sin_s, cos_t, sin_t, st0, st1, sinks, g_norm, w_in, conv_w, w_out, w_pg, w_pp,
                  g_final, cache_kt, cache_vt, t_dec):
    rows = x.shape[0]
    dec_batch = rows // t_dec
    chunk = LANES // t_dec
    nb = SAMPLE_GROUP
    const = lambda l, s: (0, 0)
    layer = lambda l, s: (l, 0, 0)
    once = dict(pipeline_mode=pl.Buffered(1))
    full = lambda shape: pl.BlockSpec(shape, const, **once)
    per_layer = lambda *shape: pl.BlockSpec((None,) + shape, layer, **once)
    cache = pl.BlockSpec((None, chunk, KV_W, WINDOW), lambda l, s: (l, s, 0, 0))
    return pl.pallas_call(
        functools.partial(_sample_kernel, rows=rows, t_dec=t_dec, chunk=chunk, nb=nb),
        name="sample_trunk",
        grid=(DEPTH, dec_batch // chunk),
        in_specs=[
            pl.BlockSpec(memory_space=pltpu.SMEM),
            full((rows, D_MODEL)),
            per_layer(rows, PLE_DIM),
            full((rows, LANES)),
            full((rows, LANES)),
            full((KV_W, rows)),
            full((KV_W, rows)),
            per_layer(dec_batch, CONV_W),
            per_layer(dec_batch, CONV_W),
            per_layer(1, D_MODEL),
            per_layer(D_MODEL, IN_W),
            per_layer(CONV_K, CONV_W),
            per_layer(D_MODEL, D_MODEL),
            per_layer(D_MODEL, D_MODEL),
            per_layer(PLE_DIM, D_MODEL),
            full((1, D_MODEL)),
            cache,
            cache,
        ],
        out_specs=[
            pl.BlockSpec((rows, D_MODEL), const),
            pl.BlockSpec((None, dec_batch, CONV_W), layer),
            pl.BlockSpec((None, dec_batch, CONV_W), layer),
            cache,
            cache,
        ],
        out_shape=[
            jax.ShapeDtypeStruct((rows, D_MODEL), F32),
            jax.ShapeDtypeStruct((DEPTH, dec_batch, CONV_W), F32),
            jax.ShapeDtypeStruct((DEPTH, dec_batch, CONV_W), F32),
            jax.ShapeDtypeStruct((DEPTH, dec_batch, KV_W, WINDOW), F32),
            jax.ShapeDtypeStruct((DEPTH, dec_batch, KV_W, WINDOW), F32),
        ],
        scratch_shapes=[
            pltpu.VMEM((rows, D_MODEL), F32),
            pltpu.VMEM((rows, ATTN_W), BF),
            pltpu.VMEM((KV_W, rows), F32),
            pltpu.VMEM((KV_W, rows), F32),
            pltpu.VMEM((rows, ATTN_W), F32),
            pltpu.VMEM((rows, D_MODEL), BF),
            pltpu.VMEM((CONV_W // LANES, rows + SUBLANES, LANES), F32),
            pltpu.VMEM((CONV_W // LANES, rows, LANES), F32),
            pltpu.VMEM((CONV_W // LANES, rows, LANES), F32),
        ],
        compiler_params=pltpu.CompilerParams(
            dimension_semantics=("arbitrary", "arbitrary"), vmem_limit_bytes=VMEM_LIMIT),
    )(sinks, x, p_all, cos, sin_s, cos_t, sin_t, st0, st1, g_norm, w_in, conv_w, w_out, w_pg, w_pp, g_final,
      cache_kt, cache_vt)


def _rope_angles(pos):
    inv = ROPE_THETA ** (-jnp.arange(0, HEAD_DIM, 2, dtype=F32) / HEAD_DIM)
    return pos[:, None] * inv[None, :]


def _rope_tables(pos):
    ang = _rope_angles(pos)
    cos = jnp.tile(jnp.cos(ang), (1, 2 * LANES // HEAD_DIM))
    sin = jnp.sin(ang)
    sin_signed = jnp.tile(jnp.concatenate([-sin, sin], axis=-1), (1, LANES // HEAD_DIM))
    return cos, sin_signed


def _rope_tables_t(pos):
    ang = _rope_angles(pos).T
    cos = jnp.tile(jnp.cos(ang), (2 * KV_W // HEAD_DIM, 1))
    sin = jnp.sin(ang)
    sin_signed = jnp.tile(jnp.concatenate([-sin, sin], axis=0), (KV_W // HEAD_DIM, 1))
    return cos, sin_signed


def kernel(x_prompt, x_sample, cache_k, cache_v, state_conv, p_prompt, p_sample, g_norm, w_in, sinks, conv_w,
           w_out, w_pg, w_pp, g_final):
    batch, seq, _ = x_prompt.shape
    dec_batch, t_dec, _ = x_sample.shape
    rows = dec_batch * t_dec

    cos_p, sin_p = _rope_tables(jnp.arange(seq, dtype=F32))
    pos_s = jnp.tile(PAST_LEN + jnp.arange(t_dec, dtype=F32), dec_batch)
    cos_s, sin_s = _rope_tables(pos_s)
    cos_st, sin_st = _rope_tables_t(pos_s)

    w_in_b, w_out_b, w_pg_b, w_pp_b = (w.astype(BF) for w in (w_in, w_out, w_pg, w_pp))
    g_norm3 = g_norm.reshape(DEPTH, 1, D_MODEL)
    g_fin2 = g_final.reshape(1, D_MODEL)
    to_t = lambda c: jnp.transpose(c, (0, 1, 3, 4, 2)).reshape(DEPTH, -1, KV_W, WINDOW)
    from_t = lambda c: jnp.transpose(c.reshape(DEPTH, -1, N_KV_HEADS, HEAD_DIM, WINDOW), (0, 1, 4, 2, 3))

    xp = x_prompt
    nkp, nvp, ncp = [], [], []
    for i in range(DEPTH):
        xp, kt, vt, ut = _prompt_layer(i, xp, p_prompt, cos_p, sin_p, sinks, g_norm3, w_in_b, conv_w, w_out_b,
                                       w_pg_b, w_pp_b, g_fin2, i == DEPTH - 1)
        nkp.append(kt)
        nvp.append(vt)
        ncp.append(ut)

    xs, nc0, nc1, nks, nvs = _sample_trunk(
        x_sample.reshape(rows, D_MODEL), p_sample.reshape(DEPTH, rows, PLE_DIM), cos_s, sin_s, cos_st, sin_st,
        state_conv[:, :, 0], state_conv[:, :, 1], sinks, g_norm3, w_in_b, conv_w, w_out_b, w_pg_b, w_pp_b,
        g_fin2, to_t(cache_k), to_t(cache_v), t_dec)

    return (xp, xs.reshape(dec_batch, t_dec, D_MODEL), from_t(jnp.stack(nkp)), from_t(jnp.stack(nvp)),
            jnp.stack(ncp), from_t(nks), from_t(nvs), jnp.stack([nc0, nc1], axis=2))
```

```python
import functools

import jax
import jax.numpy as jnp
from jax import lax
from jax.experimental import pallas as pl
from jax.experimental.pallas import tpu as pltpu

D_MODEL = 1024
DEPTH = 4
N_HEADS = 8
HEAD_DIM = 64
N_KV_HEADS = 2
GROUP = N_HEADS // N_KV_HEADS
ATTN_W = N_HEADS * HEAD_DIM
KV_W = N_KV_HEADS * HEAD_DIM
CONV_W = D_MODEL - ATTN_W
CONV_K = 3
WINDOW = 128
BLOCK = 128
PLE_DIM = 256
PAST_LEN = 8192
ROPE_THETA = 10000.0
EPS = 1e-6
SCALE = HEAD_DIM ** -0.5
NEG = -1e30
IN_W = 2 * ATTN_W + 2 * KV_W + 4 * CONV_W

C_Q = 0
C_K = ATTN_W
C_V = C_K + KV_W
C_GA = C_V + KV_W
C_BG = C_GA + ATTN_W
C_CG = C_BG + CONV_W
C_HC = C_CG + CONV_W
C_GC = C_HC + CONV_W

LANES = 128
SUBLANES = 8
ROPE_HALF = HEAD_DIM // 2

MXU_N = 256
PROMPT_TILE = 512
DENSE_PER_STEP = 1
GATE_FIRST_STEP = 1
GATE_EVERY = 2
SAMPLE_GROUP = 8
VMEM_LIMIT = 56 * 1024 * 1024

BF = jnp.bfloat16
F32 = jnp.float32
_NT = (((1,), (1,)), ((), ()))


def _mm(a, w):
    return jnp.dot(a, w, preferred_element_type=F32)


def _sigmoid(x):
    return 1.0 / (1.0 + jnp.exp(-x))


def _silu(x):
    return x * _sigmoid(x)


def _rmsnorm(x, g):
    r = lax.rsqrt(jnp.mean(x * x, axis=-1, keepdims=True) + EPS)
    return (x * r) * g


def _rope(x, cos, sin_signed):
    rows = x.shape[0]
    lane = lax.broadcasted_iota(jnp.int32, (rows, LANES), 1)
    first_half = (lane % HEAD_DIM) < ROPE_HALF
    outs = []
    for c in range(x.shape[1] // LANES):
        xc = x[:, c * LANES:(c + 1) * LANES]
        rot = jnp.where(first_half, pltpu.roll(xc, LANES - ROPE_HALF, 1), pltpu.roll(xc, ROPE_HALF, 1))
        outs.append(xc * cos + rot * sin_signed)
    return outs[0] if len(outs) == 1 else jnp.concatenate(outs, axis=1)


def _conv_gate(u, u1, u2, cw_ref, bg, gc):
    yc = cw_ref[0:1, :] * u2 + cw_ref[1:2, :] * u1 + cw_ref[2:3, :] * u
    return bg * yc * _silu(gc)


def _prompt_kernel(*refs, **static):
    i = pl.program_id(0)
    n_tiles = pl.num_programs(0) - 1

    @pl.when(i == 0)
    def _():
        _prompt_step(*refs, do_in=True, do_out=False, **static)

    @pl.when((i > 0) & (i < n_tiles))
    def _():
        _prompt_step(*refs, do_in=True, do_out=True, **static)

    @pl.when(i == n_tiles)
    def _():
        _prompt_step(*refs, do_in=False, do_out=True, **static)


def _prompt_step(sink_ref, xa_ref, p_ref, cos_ref, sin_ref, gn_ref, win_ref, cw_ref, wout_ref, wpg_ref,
                 wpp_ref, gfin_ref, xo_ref, kt_ref, vt_ref, ut_ref, kbuf, vbuf, ubuf, mix_buf, xn_buf, zc_buf,
                 q_buf, sga_buf, x1_buf, x1b_buf, xprev_buf, *, tile, n_seq, final, do_in, do_out):
    first_of_seq = pl.program_id(0) % n_seq == 0

    if do_in:
        @pl.when(first_of_seq)
        def _():
            kbuf[0:BLOCK, :] = jnp.zeros((BLOCK, KV_W), BF)
            vbuf[0:BLOCK, :] = jnp.zeros((BLOCK, KV_W), BF)
            ubuf[0:SUBLANES, :] = jnp.zeros((SUBLANES, CONV_W), F32)

    n_out = D_MODEL // MXU_N
    rows_per = tile // n_out
    for c in range(n_out):
        if do_out:
            cs = slice(c * MXU_N, (c + 1) * MXU_N)
            x1 = (xprev_buf[:, cs] + _mm(mix_buf[:, 0:ATTN_W], wout_ref[0:ATTN_W, cs])
                  + _mm(mix_buf[:, ATTN_W:D_MODEL], wout_ref[ATTN_W:D_MODEL, cs]))
            x1_buf[:, cs] = x1
            x1b_buf[:, cs] = x1.astype(BF)
        if do_in:
            rs = slice(c * rows_per, (c + 1) * rows_per)
            xn_buf[rs, :] = _rmsnorm(xa_ref[0, rs, :], gn_ref[...]).astype(BF)
    if do_in:
        xprev_buf[...] = xa_ref[0]

    pb = p_ref[0].astype(BF) if do_out else None

    def gate_unit(c):
        cs = slice(c * MXU_N, (c + 1) * MXU_N)
        gate = _sigmoid(_mm(x1b_buf[...], wpg_ref[:, cs]))
        xo_ref[0, :, cs] = x1_buf[:, cs] + gate * _mm(pb, wpp_ref[:, cs])

    def finish_out():
        if final:
            xo_ref[0] = _rmsnorm(xo_ref[0], gfin_ref[...])

    if not do_in:
        for c in range(n_out):
            gate_unit(c)
        finish_out()
        return

    seq_row = pl.multiple_of((pl.program_id(0) % n_seq) * tile, tile)
    cos = cos_ref[pl.ds(seq_row, tile), :]
    sin_s = sin_ref[pl.ds(seq_row, tile), :]

    def proj(c0, width=MXU_N):
        return _mm(xn_buf[...], win_ref[:, c0:c0 + width])

    kv = proj(C_K, 2 * KV_W)
    k = _rope(kv[:, 0:KV_W], cos, sin_s)
    kbuf[BLOCK:BLOCK + tile, :] = k.astype(BF)
    v = kv[:, KV_W:2 * KV_W]
    vbuf[BLOCK:BLOCK + tile, :] = v.astype(BF)
    kt_ref[0] = k[tile - BLOCK:, :].T
    vt_ref[0] = v[tile - BLOCK:, :].T
    for c in range(ATTN_W // MXU_N):
        cs = slice(c * MXU_N, (c + 1) * MXU_N)
        q_buf[:, cs] = (_rope(proj(C_Q + c * MXU_N), cos, sin_s) * SCALE).astype(BF)

    conv_cols = (C_CG, C_HC, C_BG, C_GC)
    n_planes = len(conv_cols)
    n_slabs = CONV_W // MXU_N
    n_gate = ATTN_W // MXU_N
    n_dense = n_gate + n_slabs * n_planes

    def dense_unit(n):
        if n < n_gate:
            sga_buf[:, n * MXU_N:(n + 1) * MXU_N] = _silu(proj(C_GA + n * MXU_N))
        else:
            half, plane = divmod(n - n_gate, n_planes)
            zc_buf[plane, :, half * MXU_N:(half + 1) * MXU_N] = proj(conv_cols[plane] + half * MXU_N)

    def conv_unit(half):
        cs = slice(half * MXU_N, (half + 1) * MXU_N)
        u = zc_buf[0, :, cs] * zc_buf[1, :, cs]
        ubuf[SUBLANES:SUBLANES + tile, cs] = u
        u1 = ubuf[SUBLANES - 1:SUBLANES - 1 + tile, cs]
        u2 = ubuf[SUBLANES - 2:SUBLANES - 2 + tile, cs]
        yc = cw_ref[0:1, cs] * u2 + cw_ref[1:2, cs] * u1 + cw_ref[2:3, cs] * u
        mix_buf[:, ATTN_W + half * MXU_N:ATTN_W + (half + 1) * MXU_N] = (
            zc_buf[2, :, cs] * yc * _silu(zc_buf[3, :, cs])).astype(BF)
        ubuf[0:SUBLANES, cs] = ubuf[tile:tile + SUBLANES, cs]
        ut_ref[0, :, cs] = u[tile - (CONV_K - 1):, :]

    row = lax.broadcasted_iota(jnp.int32, (BLOCK, BLOCK), 0)
    col = lax.broadcasted_iota(jnp.int32, (BLOCK, BLOCK), 1)
    own = col <= row
    no_prev = jnp.where(first_of_seq, NEG, 0.0).astype(F32)

    def scores(n):
        j, g = divmod(n, N_KV_HEADS)
        kcat = kbuf[j * BLOCK:(j + 2) * BLOCK, g * HEAD_DIM:(g + 1) * HEAD_DIM]
        qst = jnp.concatenate(
            [q_buf[j * BLOCK:(j + 1) * BLOCK, (GROUP * g + h) * HEAD_DIM:(GROUP * g + h + 1) * HEAD_DIM]
             for h in range(GROUP)], axis=0)
        return lax.dot_general(qst, kcat, _NT, preferred_element_type=F32)

    def softmax(n, sc):
        j, g = divmod(n, N_KV_HEADS)
        probs, invs = [], []
        for h in range(GROUP):
            sh = sc[h * BLOCK:(h + 1) * BLOCK]
            s_prev = sh[:, :BLOCK]
            if j == 0:
                s_prev = s_prev + no_prev
            sm = jnp.where(own, sh[:, BLOCK:], s_prev)
            sink = sink_ref[GROUP * g + h]
            m = jnp.maximum(jnp.max(sm, axis=-1, keepdims=True), sink)
            e = jnp.exp(sm - m)
            den = jnp.sum(e, axis=-1, keepdims=True) + jnp.exp(sink - m)
            invs.append(1.0 / den)
            eb = e.astype(BF)
            zero = jnp.zeros_like(eb)
            probs.append(jnp.concatenate([jnp.where(own, zero, eb), jnp.where(own, eb, zero)], axis=1))
        return jnp.concatenate(probs, axis=0), invs

    heads = {}

    def values(n, probs, invs):
        j, g = divmod(n, N_KV_HEADS)
        vcat = vbuf[j * BLOCK:(j + 2) * BLOCK, g * HEAD_DIM:(g + 1) * HEAD_DIM]
        o = _mm(probs, vcat)
        heads.setdefault(j, []).extend(o[h * BLOCK:(h + 1) * BLOCK] * invs[h] for h in range(GROUP))
        if g == N_KV_HEADS - 1:
            rows = slice(j * BLOCK, (j + 1) * BLOCK)
            oa = jnp.concatenate(heads.pop(j), axis=1)
            mix_buf[rows, 0:ATTN_W] = (oa * sga_buf[rows, :]).astype(BF)

    n_units = (tile // BLOCK) * N_KV_HEADS
    dense_at = {n: n // DENSE_PER_STEP for n in range(n_dense)}
    gate_at = {c: GATE_FIRST_STEP + GATE_EVERY * c for c in range(n_out)}
    conv_at = {h: dense_at[n_gate + (h + 1) * n_planes - 1] + 1 for h in range(n_slabs)}
    n_steps = max(n_units + 2, max(gate_at.values()) + 1, max(conv_at.values()) + 1)
    sc_q, sm_q = {}, {}
    for t in range(n_steps):
        if t < n_units:
            sc_q[t] = scores(t)
        for n in range(n_dense):
            if dense_at[n] == t:
                dense_unit(n)
        if 0 <= t - 2 < n_units:
            values(t - 2, *sm_q.pop(t - 2))
        for c in range(n_out):
            if do_out and gate_at[c] == t:
                gate_unit(c)
        if 0 <= t - 1 < n_units:
            sm_q[t - 1] = softmax(t - 1, sc_q.pop(t - 1))
        for h in range(n_slabs):
            if conv_at[h] == t:
                conv_unit(h)

    kbuf[0:BLOCK, :] = kbuf[tile:tile + BLOCK, :]
    vbuf[0:BLOCK, :] = vbuf[tile:tile + BLOCK, :]

    if do_out:
        finish_out()


def _prompt_layer(i, x, p_all, cos, sin_s, sinks, g_norm, w_in, conv_w, w_out, w_pg, w_pp, g_final, final):
    batch, seq, _ = x.shape
    tile = PROMPT_TILE
    n_seq = seq // tile
    n_tiles = batch * n_seq
    in_tile = lambda t: jnp.minimum(t, n_tiles - 1)
    out_tile = lambda t: jnp.maximum(t - 1, 0)
    const = lambda t: (0, 0)
    layer = lambda t: (i, 0, 0)
    resident = dict(pipeline_mode=pl.Buffered(1))
    return pl.pallas_call(
        functools.partial(_prompt_kernel, tile=tile, n_seq=n_seq, final=final),
        name=f"prompt_layer{i}",
        grid=(n_tiles + 1,),
        in_specs=[
            pl.BlockSpec(memory_space=pltpu.SMEM),
            pl.BlockSpec((1, tile, D_MODEL), lambda t: (in_tile(t) // n_seq, in_tile(t) % n_seq, 0)),
            pl.BlockSpec((None, 1, tile, PLE_DIM),
                         lambda t: (i, out_tile(t) // n_seq, out_tile(t) % n_seq, 0)),
            pl.BlockSpec((seq, LANES), const, **resident),
            pl.BlockSpec((seq, LANES), const, **resident),
            pl.BlockSpec((None, 1, D_MODEL), layer, **resident),
            pl.BlockSpec((None, D_MODEL, IN_W), layer, **resident),
            pl.BlockSpec((None, CONV_K, CONV_W), layer, **resident),
            pl.BlockSpec((None, D_MODEL, D_MODEL), layer, **resident),
            pl.BlockSpec((None, D_MODEL, D_MODEL), layer, **resident),
            pl.BlockSpec((None, PLE_DIM, D_MODEL), layer, **resident),
            pl.BlockSpec((1, D_MODEL), const, **resident),
        ],
        out_specs=[
            pl.BlockSpec((1, tile, D_MODEL), lambda t: (out_tile(t) // n_seq, out_tile(t) % n_seq, 0)),
            pl.BlockSpec((1, BLOCK, KV_W), lambda t: (in_tile(t) // n_seq, 0, 0)),
            pl.BlockSpec((1, BLOCK, KV_W), lambda t: (in_tile(t) // n_seq, 0, 0)),
            pl.BlockSpec((1, CONV_K - 1, CONV_W), lambda t: (in_tile(t) // n_seq, 0, 0)),
        ],
        out_shape=[
            jax.ShapeDtypeStruct((batch, seq, D_MODEL), F32),
            jax.ShapeDtypeStruct((batch, BLOCK, KV_W), F32),
            jax.ShapeDtypeStruct((batch, BLOCK, KV_W), F32),
            jax.ShapeDtypeStruct((batch, CONV_K - 1, CONV_W), F32),
        ],
        scratch_shapes=[
            pltpu.VMEM((tile + BLOCK, KV_W), BF),
            pltpu.VMEM((tile + BLOCK, KV_W), BF),
            pltpu.VMEM((tile + SUBLANES, CONV_W), F32),
            pltpu.VMEM((tile, D_MODEL), BF),
            pltpu.VMEM((tile, D_MODEL), BF),
            pltpu.VMEM((4, tile, CONV_W), F32),
            pltpu.VMEM((tile, ATTN_W), BF),
            pltpu.VMEM((tile, ATTN_W), F32),
            pltpu.VMEM((tile, D_MODEL), F32),
            pltpu.VMEM((tile, D_MODEL), BF),
            pltpu.VMEM((tile, D_MODEL), F32),
        ],
        compiler_params=pltpu.CompilerParams(dimension_semantics=("arbitrary",), vmem_limit_bytes=VMEM_LIMIT),
    )(sinks[i], x, p_all, cos, sin_s, g_norm, w_in, conv_w, w_out, w_pg, w_pp, g_final)


def _rope_t(x, cos_t, sin_t):
    parts = [x[r:r + ROPE_HALF] for r in range(0, x.shape[0], ROPE_HALF)]
    swapped = jnp.concatenate([parts[n ^ 1] for n in range(len(parts))], axis=0)
    return x * cos_t + swapped * sin_t


def _sample_kernel(sink_ref, x_ref, p_ref, cos_ref, sin_ref, cost_ref, sint_ref, st0_ref, st1_ref, gn_ref, win_ref,
                   cw_ref, wout_ref, wpg_ref, wpp_ref, gfin_ref, ck_ref, cv_ref,
                   xo_ref, nc0_ref, nc1_ref, nk_ref, nv_ref,
                   xs_buf, q_s, kt_s, vt_s, sga_s, mix_s, ubuf, p1_buf, p2_buf, *, rows, t_dec, chunk, nb):
    layer = pl.program_id(0)
    step = pl.program_id(1)
    last_layer = pl.num_programs(0) - 1
    last_step = pl.num_programs(1) - 1
    dec_batch = rows // t_dec
    gr = nb * t_dec
    cols = chunk * t_dec

    @pl.when((layer == 0) & (step == 0))
    def _():
        xs_buf[...] = x_ref[...]

    @pl.when(step == 0)
    def _():
        xn = _rmsnorm(xs_buf[...], gn_ref[...]).astype(BF)
        q_s[...] = (_rope(_mm(xn, win_ref[:, C_Q:C_K]), cos_ref[...], sin_ref[...]) * SCALE).astype(BF)
        kvt = lax.dot_general(win_ref[:, C_K:C_GA], xn, (((0,), (1,)), ((), ())),
                              preferred_element_type=F32)
        kt_s[...] = _rope_t(kvt[0:KV_W], cost_ref[...], sint_ref[...])
        vt_s[...] = kvt[KV_W:2 * KV_W]
        sga_s[...] = _silu(_mm(xn, win_ref[:, C_GA:C_BG]))
        u = _mm(xn, win_ref[:, C_CG:C_HC]) * _mm(xn, win_ref[:, C_HC:C_GC])
        per_batch = lambda first: pl.ds(first, dec_batch, stride=t_dec)
        for c in range(CONV_W // LANES):
            cs = slice(c * LANES, (c + 1) * LANES)
            ubuf[c, 0:SUBLANES, :] = jnp.zeros((SUBLANES, LANES), F32)
            ubuf[c, SUBLANES:SUBLANES + rows, :] = u[:, cs]
            p1_buf[c] = ubuf[c, SUBLANES - 1:SUBLANES - 1 + rows, :]
            p2_buf[c] = ubuf[c, SUBLANES - 2:SUBLANES - 2 + rows, :]
            p1_buf[c, per_batch(0), :] = st1_ref[:, cs]
            p2_buf[c, per_batch(0), :] = st0_ref[:, cs]
            p2_buf[c, per_batch(1), :] = st1_ref[:, cs]
            nc0_ref[:, cs] = ubuf[c, per_batch(SUBLANES + t_dec - 2), :]
            nc1_ref[:, cs] = ubuf[c, per_batch(SUBLANES + t_dec - 1), :]
        u1 = jnp.concatenate([p1_buf[c] for c in range(CONV_W // LANES)], axis=1)
        u2 = jnp.concatenate([p2_buf[c] for c in range(CONV_W // LANES)], axis=1)
        bg = _mm(xn, win_ref[:, C_BG:C_CG])
        gc = _mm(xn, win_ref[:, C_GC:IN_W])
        mix_s[:, ATTN_W:D_MODEL] = _conv_gate(u, u1, u2, cw_ref, bg, gc).astype(BF)

    c0 = pl.multiple_of(step * cols, cols)
    kt_new = kt_s[:, pl.ds(c0, cols)]
    vt_new = vt_s[:, pl.ds(c0, cols)]

    qrow = lax.broadcasted_iota(jnp.int32, (gr, nb * WINDOW), 0)
    ccol = lax.broadcasted_iota(jnp.int32, (gr, nb * WINDOW), 1)
    valid_c = ((ccol // WINDOW) == (qrow // t_dec)) & ((ccol % WINDOW) > (qrow % t_dec))
    qrow_n = lax.broadcasted_iota(jnp.int32, (gr, gr), 0)
    ncol = lax.broadcasted_iota(jnp.int32, (gr, gr), 1)
    valid_n = ((ncol // t_dec) == (qrow_n // t_dec)) & ((ncol % t_dec) <= (qrow_n % t_dec))
    lane = lax.broadcasted_iota(jnp.int32, (KV_W, WINDOW), 1)
    kept = lane < WINDOW - t_dec

    for sg in range(chunk // nb):
        r0 = pl.multiple_of(step * cols + sg * gr, gr)
        qg = q_s[pl.ds(r0, gr), :]
        heads = []
        for g in range(N_KV_HEADS):
            hd = slice(g * HEAD_DIM, (g + 1) * HEAD_DIM)
            ktc = jnp.concatenate([ck_ref[sg * nb + b, hd, :] for b in range(nb)], axis=1).astype(BF)
            vtc = jnp.concatenate([cv_ref[sg * nb + b, hd, :] for b in range(nb)], axis=1).astype(BF)
            ktn = kt_new[hd, sg * gr:(sg + 1) * gr].astype(BF)
            vtn = vt_new[hd, sg * gr:(sg + 1) * gr].astype(BF)
            qst = jnp.concatenate(
                [qg[:, (GROUP * g + h) * HEAD_DIM:(GROUP * g + h + 1) * HEAD_DIM] for h in range(GROUP)], axis=0)
            sc = _mm(qst, ktc)
            sn = _mm(qst, ktn)
            ecs, ens, invs = [], [], []
            for h in range(GROUP):
                scm = jnp.where(valid_c, sc[h * gr:(h + 1) * gr], NEG)
                snm = jnp.where(valid_n, sn[h * gr:(h + 1) * gr], NEG)
                sink = sink_ref[layer, GROUP * g + h]
                m = jnp.maximum(jnp.maximum(jnp.max(scm, axis=-1, keepdims=True),
                                            jnp.max(snm, axis=-1, keepdims=True)), sink)
                ec = jnp.exp(scm - m)
                en = jnp.exp(snm - m)
                den = (jnp.sum(ec, axis=-1, keepdims=True) + jnp.sum(en, axis=-1, keepdims=True)
                       + jnp.exp(sink - m))
                invs.append(1.0 / den)
                ecs.append(ec.astype(BF))
                ens.append(en.astype(BF))
            o = (lax.dot_general(jnp.concatenate(ecs, axis=0), vtc, _NT, preferred_element_type=F32)
                 + lax.dot_general(jnp.concatenate(ens, axis=0), vtn, _NT, preferred_element_type=F32))
            for h in range(GROUP):
                heads.append(o[h * gr:(h + 1) * gr] * invs[h])
        oa = jnp.concatenate(heads, axis=1)
        mix_s[pl.ds(r0, gr), 0:ATTN_W] = (oa * sga_s[pl.ds(r0, gr), :]).astype(BF)

        for b in range(sg * nb, (sg + 1) * nb):
            place = (WINDOW - t_dec - t_dec * b) % cols
            k_in = pltpu.roll(kt_new, place, 1) if place else kt_new
            v_in = pltpu.roll(vt_new, place, 1) if place else vt_new
            nk_ref[b] = jnp.where(kept, pltpu.roll(ck_ref[b], WINDOW - t_dec, 1), k_in)
            nv_ref[b] = jnp.where(kept, pltpu.roll(cv_ref[b], WINDOW - t_dec, 1), v_in)

    @pl.when(step == last_step)
    def _():
        x1 = xs_buf[...] + _mm(mix_s[...], wout_ref[...])
        gate = _sigmoid(_mm(x1.astype(BF), wpg_ref[...]))
        x2 = x1 + gate * _mm(p_ref[...].astype(BF), wpp_ref[...])
        xs_buf[...] = x2

        @pl.when(layer == last_layer)
        def _():
            xo_ref[...] = _rmsnorm(x2, gfin_ref[...])


def _sample_trunk(x, p_all, cos, sin_s, cos_t, sin_t, st0, st1, sinks, g_norm, w_in, conv_w, w_out, w_pg, w_pp,
                  g_final, cache_kt, cache_vt, t_dec):
    rows = x.shape[0]
    dec_batch = rows // t_dec
    chunk = LANES // t_dec
    nb = SAMPLE_GROUP
    const = lambda l, s: (0, 0)
    layer = lambda l, s: (l, 0, 0)
    once = dict(pipeline_mode=pl.Buffered(1))
    full = lambda shape: pl.BlockSpec(shape, const, **once)
    per_layer = lambda *shape: pl.BlockSpec((None,) + shape, layer, **once)
    cache = pl.BlockSpec((None, chunk, KV_W, WINDOW), lambda l, s: (l, s, 0, 0))
    return pl.pallas_call(
        functools.partial(_sample_kernel, rows=rows, t_dec=t_dec, chunk=chunk, nb=nb),
        name="sample_trunk",
        grid=(DEPTH, dec_batch // chunk),
        in_specs=[
            pl.BlockSpec(memory_space=pltpu.SMEM),
            full((rows, D_MODEL)),
            per_layer(rows, PLE_DIM),
            full((rows, LANES)),
            full((rows, LANES)),
            full((KV_W, rows)),
            full((KV_W, rows)),
            per_layer(dec_batch, CONV_W),
            per_layer(dec_batch, CONV_W),
            per_layer(1, D_MODEL),
            per_layer(D_MODEL, IN_W),
            per_layer(CONV_K, CONV_W),
            per_layer(D_MODEL, D_MODEL),
            per_layer(D_MODEL, D_MODEL),
            per_layer(PLE_DIM, D_MODEL),
            full((1, D_MODEL)),
            cache,
            cache,
        ],
        out_specs=[
            pl.BlockSpec((rows, D_MODEL), const),
            pl.BlockSpec((None, dec_batch, CONV_W), layer),
            pl.BlockSpec((None, dec_batch, CONV_W), layer),
            cache,
            cache,
        ],
        out_shape=[
            jax.ShapeDtypeStruct((rows, D_MODEL), F32),
            jax.ShapeDtypeStruct((DEPTH, dec_batch, CONV_W), F32),
            jax.ShapeDtypeStruct((DEPTH, dec_batch, CONV_W), F32),
            jax.ShapeDtypeStruct((DEPTH, dec_batch, KV_W, WINDOW), F32),
            jax.ShapeDtypeStruct((DEPTH, dec_batch, KV_W, WINDOW), F32),
        ],
        scratch_shapes=[
            pltpu.VMEM((rows, D_MODEL), F32),
            pltpu.VMEM((rows, ATTN_W), BF),
            pltpu.VMEM((KV_W, rows), F32),
            pltpu.VMEM((KV_W, rows), F32),
            pltpu.VMEM((rows, ATTN_W), F32),
            pltpu.VMEM((rows, D_MODEL), BF),
            pltpu.VMEM((CONV_W // LANES, rows + SUBLANES, LANES), F32),
            pltpu.VMEM((CONV_W // LANES, rows, LANES), F32),
            pltpu.VMEM((CONV_W // LANES, rows, LANES), F32),
        ],
        compiler_params=pltpu.CompilerParams(
            dimension_semantics=("arbitrary", "arbitrary"), vmem_limit_bytes=VMEM_LIMIT),
    )(sinks, x, p_all, cos, sin_s, cos_t, sin_t, st0, st1, g_norm, w_in, conv_w, w_out, w_pg, w_pp, g_final,
      cache_kt, cache_vt)


def _rope_angles(pos):
    inv = ROPE_THETA ** (-jnp.arange(0, HEAD_DIM, 2, dtype=F32) / HEAD_DIM)
    return pos[:, None] * inv[None, :]


def _rope_tables(pos):
    ang = _rope_angles(pos)
    cos = jnp.tile(jnp.cos(ang), (1, 2 * LANES // HEAD_DIM))
    sin = jnp.sin(ang)
    sin_signed = jnp.tile(jnp.concatenate([-sin, sin], axis=-1), (1, LANES // HEAD_DIM))
    return cos, sin_signed


def _rope_tables_t(pos):
    ang = _rope_angles(pos).T
    cos = jnp.tile(jnp.cos(ang), (2 * KV_W // HEAD_DIM, 1))
    sin = jnp.sin(ang)
    sin_signed = jnp.tile(jnp.concatenate([-sin, sin], axis=0), (KV_W // HEAD_DIM, 1))
    return cos, sin_signed


def kernel(x_prompt, x_sample, cache_k, cache_v, state_conv, p_prompt, p_sample, g_norm, w_in, sinks, conv_w,
           w_out, w_pg, w_pp, g_final):
    batch, seq, _ = x_prompt.shape
    dec_batch, t_dec, _ = x_sample.shape
    rows = dec_batch * t_dec

    cos_p, sin_p = _rope_tables(jnp.arange(seq, dtype=F32))
    pos_s = jnp.tile(PAST_LEN + jnp.arange(t_dec, dtype=F32), dec_batch)
    cos_s, sin_s = _rope_tables(pos_s)
    cos_st, sin_st = _rope_tables_t(pos_s)

    w_in_b, w_out_b, w_pg_b, w_pp_b = (w.astype(BF) for w in (w_in, w_out, w_pg, w_pp))
    g_norm3 = g_norm.reshape(DEPTH, 1, D_MODEL)
    g_fin2 = g_final.reshape(1, D_MODEL)
    to_t = lambda c: jnp.transpose(c, (0, 1, 3, 4, 2)).reshape(DEPTH, -1, KV_W, WINDOW)
    from_t = lambda c: jnp.transpose(c.reshape(DEPTH, -1, N_KV_HEADS, HEAD_DIM, WINDOW), (0, 1, 4, 2, 3))

    xp = x_prompt
    nkp, nvp, ncp = [], [], []
    for i in range(DEPTH):
        xp, kt, vt, ut = _prompt_layer(i, xp, p_prompt, cos_p, sin_p, sinks, g_norm3, w_in_b, conv_w, w_out_b,
                                       w_pg_b, w_pp_b, g_fin2, i == DEPTH - 1)
        nkp.append(kt)
        nvp.append(vt)
        ncp.append(ut)

    xs, nc0, nc1, nks, nvs = _sample_trunk(
        x_sample.reshape(rows, D_MODEL), p_sample.reshape(DEPTH, rows, PLE_DIM), cos_s, sin_s, cos_st, sin_st,
        state_conv[:, :, 0], state_conv[:, :, 1], sinks, g_norm3, w_in_b, conv_w, w_out_b, w_pg_b, w_pp_b,
        g_fin2, to_t(cache_k), to_t(cache_v), t_dec)

    return (xp, xs.reshape(dec_batch, t_dec, D_MODEL), from_t(jnp.stack(nkp)), from_t(jnp.stack(nvp)),
            jnp.stack(ncp), from_t(nks), from_t(nvs), jnp.stack([nc0, nc1], axis=2))
```

```python
import functools

import jax
import jax.numpy as jnp
from jax import lax
from jax.experimental import pallas as pl
from jax.experimental.pallas import tpu as pltpu

D_MODEL = 1024
DEPTH = 4
N_HEADS = 8
HEAD_DIM = 64
N_KV_HEADS = 2
GROUP = N_HEADS // N_KV_HEADS
ATTN_W = N_HEADS * HEAD_DIM
KV_W = N_KV_HEADS * HEAD_DIM
CONV_W = D_MODEL - ATTN_W
CONV_K = 3
WINDOW = 128
BLOCK = 128
PLE_DIM = 256
PAST_LEN = 8192
ROPE_THETA = 10000.0
EPS = 1e-6
SCALE = HEAD_DIM ** -0.5
NEG = -1e30
IN_W = 2 * ATTN_W + 2 * KV_W + 4 * CONV_W

C_Q = 0
C_K = ATTN_W
C_V = C_K + KV_W
C_GA = C_V + KV_W
C_BG = C_GA + ATTN_W
C_CG = C_BG + CONV_W
C_HC = C_CG + CONV_W
C_GC = C_HC + CONV_W

LANES = 128
SUBLANES = 8
ROPE_HALF = HEAD_DIM // 2

MXU_N = 256
PROMPT_TILE = 512
SAMPLE_GROUP = 8
VMEM_LIMIT = 56 * 1024 * 1024

BF = jnp.bfloat16
F32 = jnp.float32
_NT = (((1,), (1,)), ((), ()))


def _mm(a, w):
    return jnp.dot(a, w, preferred_element_type=F32)


def _sigmoid(x):
    return 1.0 / (1.0 + jnp.exp(-x))


def _silu(x):
    return x * _sigmoid(x)


def _rmsnorm(x, g):
    r = lax.rsqrt(jnp.mean(x * x, axis=-1, keepdims=True) + EPS)
    return (x * r) * g


def _rope(x, cos, sin_signed):
    rows = x.shape[0]
    lane = lax.broadcasted_iota(jnp.int32, (rows, LANES), 1)
    first_half = (lane % HEAD_DIM) < ROPE_HALF
    outs = []
    for c in range(x.shape[1] // LANES):
        xc = x[:, c * LANES:(c + 1) * LANES]
        rot = jnp.where(first_half, pltpu.roll(xc, LANES - ROPE_HALF, 1), pltpu.roll(xc, ROPE_HALF, 1))
        outs.append(xc * cos + rot * sin_signed)
    return outs[0] if len(outs) == 1 else jnp.concatenate(outs, axis=1)


def _conv_gate(u, u1, u2, cw_ref, bg, gc):
    yc = cw_ref[0:1, :] * u2 + cw_ref[1:2, :] * u1 + cw_ref[2:3, :] * u
    return bg * yc * _silu(gc)


def _prompt_kernel(sink_ref, xa_ref, xb_ref, p_ref, cos_ref, sin_ref, gn_ref, win_ref, cw_ref, wout_ref, wpg_ref,
                   wpp_ref, gfin_ref, xo_ref, kt_ref, vt_ref, ut_ref, kbuf, vbuf, ubuf, mix_buf, xn_buf, zc_buf,
                   q_buf, sga_buf, x1_buf, x1b_buf, *, tile, n_seq, final):
    i = pl.program_id(0)
    n_tiles = pl.num_programs(0) - 1
    first_of_seq = jnp.minimum(i, n_tiles - 1) % n_seq == 0

    @pl.when(i == 0)
    def _():
        mix_buf[...] = jnp.zeros(mix_buf.shape, BF)

    @pl.when(first_of_seq)
    def _():
        kbuf[0:BLOCK, :] = jnp.zeros((BLOCK, KV_W), BF)
        vbuf[0:BLOCK, :] = jnp.zeros((BLOCK, KV_W), BF)
        ubuf[0:SUBLANES, :] = jnp.zeros((SUBLANES, CONV_W), F32)

    n_out = D_MODEL // MXU_N
    rows_per = tile // n_out
    for c in range(n_out):
        cs = slice(c * MXU_N, (c + 1) * MXU_N)
        x1 = (xb_ref[0, :, cs] + _mm(mix_buf[:, 0:ATTN_W], wout_ref[0:ATTN_W, cs])
              + _mm(mix_buf[:, ATTN_W:D_MODEL], wout_ref[ATTN_W:D_MODEL, cs]))
        x1_buf[:, cs] = x1
        x1b_buf[:, cs] = x1.astype(BF)
        rs = slice(c * rows_per, (c + 1) * rows_per)
        xn_buf[rs, :] = _rmsnorm(xa_ref[0, rs, :], gn_ref[...]).astype(BF)
    pb = p_ref[0].astype(BF)

    def gate_unit(c):
        cs = slice(c * MXU_N, (c + 1) * MXU_N)
        gate = _sigmoid(_mm(x1b_buf[...], wpg_ref[:, cs]))
        xo_ref[0, :, cs] = x1_buf[:, cs] + gate * _mm(pb, wpp_ref[:, cs])

    cos = cos_ref[...]
    sin_s = sin_ref[...]

    def proj(c0, width=MXU_N):
        return _mm(xn_buf[...], win_ref[:, c0:c0 + width])

    kv = proj(C_K, 2 * KV_W)
    k = _rope(kv[:, 0:KV_W], cos, sin_s)
    kbuf[BLOCK:BLOCK + tile, :] = k.astype(BF)
    v = kv[:, KV_W:2 * KV_W]
    vbuf[BLOCK:BLOCK + tile, :] = v.astype(BF)
    kt_ref[0] = k[tile - BLOCK:, :].T
    vt_ref[0] = v[tile - BLOCK:, :].T
    for c in range(ATTN_W // MXU_N):
        cs = slice(c * MXU_N, (c + 1) * MXU_N)
        q_buf[:, cs] = (_rope(proj(C_Q + c * MXU_N), cos, sin_s) * SCALE).astype(BF)

    conv_cols = (C_CG, C_HC, C_BG, C_GC)
    n_planes = len(conv_cols)
    n_slabs = CONV_W // MXU_N
    n_gate = ATTN_W // MXU_N
    n_dense = n_gate + n_slabs * n_planes

    def dense_unit(n):
        if n < n_gate:
            sga_buf[:, n * MXU_N:(n + 1) * MXU_N] = _silu(proj(C_GA + n * MXU_N))
        else:
            half, plane = divmod(n - n_gate, n_planes)
            zc_buf[plane, :, half * MXU_N:(half + 1) * MXU_N] = proj(conv_cols[plane] + half * MXU_N)

    def conv_unit(half):
        cs = slice(half * MXU_N, (half + 1) * MXU_N)
        u = zc_buf[0, :, cs] * zc_buf[1, :, cs]
        ubuf[SUBLANES:SUBLANES + tile, cs] = u
        u1 = ubuf[SUBLANES - 1:SUBLANES - 1 + tile, cs]
        u2 = ubuf[SUBLANES - 2:SUBLANES - 2 + tile, cs]
        yc = cw_ref[0:1, cs] * u2 + cw_ref[1:2, cs] * u1 + cw_ref[2:3, cs] * u
        mix_buf[:, ATTN_W + half * MXU_N:ATTN_W + (half + 1) * MXU_N] = (
            zc_buf[2, :, cs] * yc * _silu(zc_buf[3, :, cs])).astype(BF)
        ubuf[0:SUBLANES, cs] = ubuf[tile:tile + SUBLANES, cs]
        ut_ref[0, :, cs] = u[tile - (CONV_K - 1):, :]

    row = lax.broadcasted_iota(jnp.int32, (BLOCK, BLOCK), 0)
    col = lax.broadcasted_iota(jnp.int32, (BLOCK, BLOCK), 1)
    own = col <= row
    no_prev = jnp.where(first_of_seq, NEG, 0.0).astype(F32)

    def scores(n):
        j, g = divmod(n, N_KV_HEADS)
        kcat = kbuf[j * BLOCK:(j + 2) * BLOCK, g * HEAD_DIM:(g + 1) * HEAD_DIM]
        qst = jnp.concatenate(
            [q_buf[j * BLOCK:(j + 1) * BLOCK, (GROUP * g + h) * HEAD_DIM:(GROUP * g + h + 1) * HEAD_DIM]
             for h in range(GROUP)], axis=0)
        return lax.dot_general(qst, kcat, _NT, preferred_element_type=F32)

    def softmax(n, sc):
        j, g = divmod(n, N_KV_HEADS)
        probs, invs = [], []
        for h in range(GROUP):
            sh = sc[h * BLOCK:(h + 1) * BLOCK]
            s_prev = sh[:, :BLOCK]
            if j == 0:
                s_prev = s_prev + no_prev
            sm = jnp.where(own, sh[:, BLOCK:], s_prev)
            sink = sink_ref[GROUP * g + h]
            m = jnp.maximum(jnp.max(sm, axis=-1, keepdims=True), sink)
            e = jnp.exp(sm - m)
            den = jnp.sum(e, axis=-1, keepdims=True) + jnp.exp(sink - m)
            invs.append(1.0 / den)
            eb = e.astype(BF)
            zero = jnp.zeros_like(eb)
            probs.append(jnp.concatenate([jnp.where(own, zero, eb), jnp.where(own, eb, zero)], axis=1))
        return jnp.concatenate(probs, axis=0), invs

    heads = {}

    def values(n, probs, invs):
        j, g = divmod(n, N_KV_HEADS)
        vcat = vbuf[j * BLOCK:(j + 2) * BLOCK, g * HEAD_DIM:(g + 1) * HEAD_DIM]
        o = _mm(probs, vcat)
        heads.setdefault(j, []).extend(o[h * BLOCK:(h + 1) * BLOCK] * invs[h] for h in range(GROUP))
        if g == N_KV_HEADS - 1:
            rows = slice(j * BLOCK, (j + 1) * BLOCK)
            oa = jnp.concatenate(heads.pop(j), axis=1)
            mix_buf[rows, 0:ATTN_W] = (oa * sga_buf[rows, :]).astype(BF)

    n_units = (tile // BLOCK) * N_KV_HEADS
    conv_at = {n_gate + (h + 1) * n_planes: h for h in range(n_slabs)}
    n_steps = max(n_units + 2, max(conv_at) + 1)
    gate_at = {2 * c + 1: c for c in range(n_out)}
    sc_q, sm_q = {}, {}
    for t in range(n_steps):
        if t < n_units:
            sc_q[t] = scores(t)
        if t < n_dense:
            dense_unit(t)
        if 0 <= t - 2 < n_units:
            values(t - 2, *sm_q.pop(t - 2))
        if t in gate_at:
            gate_unit(gate_at[t])
        if 0 <= t - 1 < n_units:
            sm_q[t - 1] = softmax(t - 1, sc_q.pop(t - 1))
        if t in conv_at:
            conv_unit(conv_at[t])

    kbuf[0:BLOCK, :] = kbuf[tile:tile + BLOCK, :]
    vbuf[0:BLOCK, :] = vbuf[tile:tile + BLOCK, :]

    if final:
        xo_ref[0] = _rmsnorm(xo_ref[0], gfin_ref[...])


def _prompt_layer(i, x, p_all, cos, sin_s, sinks, g_norm, w_in, conv_w, w_out, w_pg, w_pp, g_final, final):
    batch, seq, _ = x.shape
    tile = PROMPT_TILE
    n_seq = seq // tile
    n_tiles = batch * n_seq
    in_tile = lambda t: jnp.minimum(t, n_tiles - 1)
    out_tile = lambda t: jnp.maximum(t - 1, 0)
    const = lambda t: (0, 0)
    layer = lambda t: (i, 0, 0)
    resident = dict(pipeline_mode=pl.Buffered(1))
    return pl.pallas_call(
        functools.partial(_prompt_kernel, tile=tile, n_seq=n_seq, final=final),
        name=f"prompt_layer{i}",
        grid=(n_tiles + 1,),
        in_specs=[
            pl.BlockSpec(memory_space=pltpu.SMEM),
            pl.BlockSpec((1, tile, D_MODEL), lambda t: (in_tile(t) // n_seq, in_tile(t) % n_seq, 0)),
            pl.BlockSpec((1, tile, D_MODEL), lambda t: (out_tile(t) // n_seq, out_tile(t) % n_seq, 0)),
            pl.BlockSpec((None, 1, tile, PLE_DIM),
                         lambda t: (i, out_tile(t) // n_seq, out_tile(t) % n_seq, 0)),
            pl.BlockSpec((tile, LANES), lambda t: (in_tile(t) % n_seq, 0)),
            pl.BlockSpec((tile, LANES), lambda t: (in_tile(t) % n_seq, 0)),
            pl.BlockSpec((None, 1, D_MODEL), layer, **resident),
            pl.BlockSpec((None, D_MODEL, IN_W), layer, **resident),
            pl.BlockSpec((None, CONV_K, CONV_W), layer, **resident),
            pl.BlockSpec((None, D_MODEL, D_MODEL), layer, **resident),
            pl.BlockSpec((None, D_MODEL, D_MODEL), layer, **resident),
            pl.BlockSpec((None, PLE_DIM, D_MODEL), layer, **resident),
            pl.BlockSpec((1, D_MODEL), const, **resident),
        ],
        out_specs=[
            pl.BlockSpec((1, tile, D_MODEL), lambda t: (out_tile(t) // n_seq, out_tile(t) % n_seq, 0)),
            pl.BlockSpec((1, BLOCK, KV_W), lambda t: (in_tile(t) // n_seq, 0, 0)),
            pl.BlockSpec((1, BLOCK, KV_W), lambda t: (in_tile(t) // n_seq, 0, 0)),
            pl.BlockSpec((1, CONV_K - 1, CONV_W), lambda t: (in_tile(t) // n_seq, 0, 0)),
        ],
        out_shape=[
            jax.ShapeDtypeStruct((batch, seq, D_MODEL), F32),
            jax.ShapeDtypeStruct((batch, BLOCK, KV_W), F32),
            jax.ShapeDtypeStruct((batch, BLOCK, KV_W), F32),
            jax.ShapeDtypeStruct((batch, CONV_K - 1, CONV_W), F32),
        ],
        scratch_shapes=[
            pltpu.VMEM((tile + BLOCK, KV_W), BF),
            pltpu.VMEM((tile + BLOCK, KV_W), BF),
            pltpu.VMEM((tile + SUBLANES, CONV_W), F32),
            pltpu.VMEM((tile, D_MODEL), BF),
            pltpu.VMEM((tile, D_MODEL), BF),
            pltpu.VMEM((4, tile, CONV_W), F32),
            pltpu.VMEM((tile, ATTN_W), BF),
            pltpu.VMEM((tile, ATTN_W), F32),
            pltpu.VMEM((tile, D_MODEL), F32),
            pltpu.VMEM((tile, D_MODEL), BF),
        ],
        compiler_params=pltpu.CompilerParams(dimension_semantics=("arbitrary",), vmem_limit_bytes=VMEM_LIMIT),
    )(sinks[i], x, x, p_all, cos, sin_s, g_norm, w_in, conv_w, w_out, w_pg, w_pp, g_final)


def _rope_t(x, cos_t, sin_t):
    parts = [x[r:r + ROPE_HALF] for r in range(0, x.shape[0], ROPE_HALF)]
    swapped = jnp.concatenate([parts[n ^ 1] for n in range(len(parts))], axis=0)
    return x * cos_t + swapped * sin_t


def _sample_kernel(sink_ref, x_ref, p_ref, cos_ref, sin_ref, cost_ref, sint_ref, st0_ref, st1_ref, gn_ref, win_ref,
                   cw_ref, wout_ref, wpg_ref, wpp_ref, gfin_ref, ck_ref, cv_ref,
                   xo_ref, nc0_ref, nc1_ref, nk_ref, nv_ref,
                   xs_buf, q_s, kt_s, vt_s, sga_s, mix_s, ubuf, p1_buf, p2_buf, *, rows, t_dec, chunk, nb):
    layer = pl.program_id(0)
    step = pl.program_id(1)
    last_layer = pl.num_programs(0) - 1
    last_step = pl.num_programs(1) - 1
    dec_batch = rows // t_dec
    gr = nb * t_dec
    cols = chunk * t_dec

    @pl.when((layer == 0) & (step == 0))
    def _():
        xs_buf[...] = x_ref[...]

    @pl.when(step == 0)
    def _():
        xn = _rmsnorm(xs_buf[...], gn_ref[...]).astype(BF)
        q_s[...] = (_rope(_mm(xn, win_ref[:, C_Q:C_K]), cos_ref[...], sin_ref[...]) * SCALE).astype(BF)
        kvt = lax.dot_general(win_ref[:, C_K:C_GA], xn, (((0,), (1,)), ((), ())),
                              preferred_element_type=F32)
        kt_s[...] = _rope_t(kvt[0:KV_W], cost_ref[...], sint_ref[...])
        vt_s[...] = kvt[KV_W:2 * KV_W]
        sga_s[...] = _silu(_mm(xn, win_ref[:, C_GA:C_BG]))
        u = _mm(xn, win_ref[:, C_CG:C_HC]) * _mm(xn, win_ref[:, C_HC:C_GC])
        per_batch = lambda first: pl.ds(first, dec_batch, stride=t_dec)
        for c in range(CONV_W // LANES):
            cs = slice(c * LANES, (c + 1) * LANES)
            ubuf[c, 0:SUBLANES, :] = jnp.zeros((SUBLANES, LANES), F32)
            ubuf[c, SUBLANES:SUBLANES + rows, :] = u[:, cs]
            p1_buf[c] = ubuf[c, SUBLANES - 1:SUBLANES - 1 + rows, :]
            p2_buf[c] = ubuf[c, SUBLANES - 2:SUBLANES - 2 + rows, :]
            p1_buf[c, per_batch(0), :] = st1_ref[:, cs]
            p2_buf[c, per_batch(0), :] = st0_ref[:, cs]
            p2_buf[c, per_batch(1), :] = st1_ref[:, cs]
            nc0_ref[:, cs] = ubuf[c, per_batch(SUBLANES + t_dec - 2), :]
            nc1_ref[:, cs] = ubuf[c, per_batch(SUBLANES + t_dec - 1), :]
        u1 = jnp.concatenate([p1_buf[c] for c in range(CONV_W // LANES)], axis=1)
        u2 = jnp.concatenate([p2_buf[c] for c in range(CONV_W // LANES)], axis=1)
        bg = _mm(xn, win_ref[:, C_BG:C_CG])
        gc = _mm(xn, win_ref[:, C_GC:IN_W])
        mix_s[:, ATTN_W:D_MODEL] = _conv_gate(u, u1, u2, cw_ref, bg, gc).astype(BF)

    c0 = pl.multiple_of(step * cols, cols)
    kt_new = kt_s[:, pl.ds(c0, cols)]
    vt_new = vt_s[:, pl.ds(c0, cols)]

    qrow = lax.broadcasted_iota(jnp.int32, (gr, nb * WINDOW), 0)
    ccol = lax.broadcasted_iota(jnp.int32, (gr, nb * WINDOW), 1)
    valid_c = ((ccol // WINDOW) == (qrow // t_dec)) & ((ccol % WINDOW) > (qrow % t_dec))
    qrow_n = lax.broadcasted_iota(jnp.int32, (gr, gr), 0)
    ncol = lax.broadcasted_iota(jnp.int32, (gr, gr), 1)
    valid_n = ((ncol // t_dec) == (qrow_n // t_dec)) & ((ncol % t_dec) <= (qrow_n % t_dec))
    lane = lax.broadcasted_iota(jnp.int32, (KV_W, WINDOW), 1)
    kept = lane < WINDOW - t_dec

    def update_window(b):
        place = (WINDOW - t_dec - t_dec * b) % cols
        k_in = pltpu.roll(kt_new, place, 1) if place else kt_new
        v_in = pltpu.roll(vt_new, place, 1) if place else vt_new
        nk_ref[b] = jnp.where(kept, pltpu.roll(ck_ref[b], WINDOW - t_dec, 1), k_in)
        nv_ref[b] = jnp.where(kept, pltpu.roll(cv_ref[b], WINDOW - t_dec, 1), v_in)

    def rows_of(sg):
        return pl.ds(pl.multiple_of(step * cols + sg * gr, gr), gr)

    def scores(n):
        sg, g = divmod(n, N_KV_HEADS)
        hd = slice(g * HEAD_DIM, (g + 1) * HEAD_DIM)
        qg = q_s[rows_of(sg), :]
        ktc = jnp.concatenate([ck_ref[sg * nb + b, hd, :] for b in range(nb)], axis=1).astype(BF)
        ktn = kt_new[hd, sg * gr:(sg + 1) * gr].astype(BF)
        qst = jnp.concatenate(
            [qg[:, (GROUP * g + h) * HEAD_DIM:(GROUP * g + h + 1) * HEAD_DIM] for h in range(GROUP)], axis=0)
        return _mm(qst, ktc), _mm(qst, ktn)

    def softmax(n, sc, sn):
        g = n % N_KV_HEADS
        ecs, ens, invs = [], [], []
        for h in range(GROUP):
            scm = jnp.where(valid_c, sc[h * gr:(h + 1) * gr], NEG)
            snm = jnp.where(valid_n, sn[h * gr:(h + 1) * gr], NEG)
            sink = sink_ref[layer, GROUP * g + h]
            m = jnp.maximum(jnp.maximum(jnp.max(scm, axis=-1, keepdims=True),
                                        jnp.max(snm, axis=-1, keepdims=True)), sink)
            ec = jnp.exp(scm - m)
            en = jnp.exp(snm - m)
            den = (jnp.sum(ec, axis=-1, keepdims=True) + jnp.sum(en, axis=-1, keepdims=True)
                   + jnp.exp(sink - m))
            invs.append(1.0 / den)
            ecs.append(ec.astype(BF))
            ens.append(en.astype(BF))
        return jnp.concatenate(ecs, axis=0), jnp.concatenate(ens, axis=0), invs

    heads = {}

    def values(n, ec, en, invs):
        sg, g = divmod(n, N_KV_HEADS)
        hd = slice(g * HEAD_DIM, (g + 1) * HEAD_DIM)
        vtc = jnp.concatenate([cv_ref[sg * nb + b, hd, :] for b in range(nb)], axis=1).astype(BF)
        vtn = vt_new[hd, sg * gr:(sg + 1) * gr].astype(BF)
        o = (lax.dot_general(ec, vtc, _NT, preferred_element_type=F32)
             + lax.dot_general(en, vtn, _NT, preferred_element_type=F32))
        heads.setdefault(sg, []).extend(o[h * gr:(h + 1) * gr] * invs[h] for h in range(GROUP))
        if g == N_KV_HEADS - 1:
            oa = jnp.concatenate(heads.pop(sg), axis=1)
            mix_s[rows_of(sg), 0:ATTN_W] = (oa * sga_s[rows_of(sg), :]).astype(BF)
            for b in range(sg * nb, (sg + 1) * nb):
                update_window(b)

    for n in range((chunk // nb) * N_KV_HEADS):
        values(n, *softmax(n, *scores(n)))

    @pl.when(step == last_step)
    def _():
        x1 = xs_buf[...] + _mm(mix_s[...], wout_ref[...])
        gate = _sigmoid(_mm(x1.astype(BF), wpg_ref[...]))
        x2 = x1 + gate * _mm(p_ref[...].astype(BF), wpp_ref[...])
        xs_buf[...] = x2

        @pl.when(layer == last_layer)
        def _():
            xo_ref[...] = _rmsnorm(x2, gfin_ref[...])


def _sample_trunk(x, p_all, cos, sin_s, cos_t, sin_t, st0, st1, sinks, g_norm, w_in, conv_w, w_out, w_pg, w_pp,
                  g_final, cache_kt, cache_vt, t_dec):
    rows = x.shape[0]
    dec_batch = rows // t_dec
    chunk = LANES // t_dec
    nb = SAMPLE_GROUP
    const = lambda l, s: (0, 0)
    layer = lambda l, s: (l, 0, 0)
    once = dict(pipeline_mode=pl.Buffered(1))
    full = lambda shape: pl.BlockSpec(shape, const, **once)
    per_layer = lambda *shape: pl.BlockSpec((None,) + shape, layer, **once)
    cache = pl.BlockSpec((None, chunk, KV_W, WINDOW), lambda l, s: (l, s, 0, 0))
    return pl.pallas_call(
        functools.partial(_sample_kernel, rows=rows, t_dec=t_dec, chunk=chunk, nb=nb),
        name="sample_trunk",
        grid=(DEPTH, dec_batch // chunk),
        in_specs=[
            pl.BlockSpec(memory_space=pltpu.SMEM),
            full((rows, D_MODEL)),
            per_layer(rows, PLE_DIM),
            full((rows, LANES)),
            full((rows, LANES)),
            full((KV_W, rows)),
            full((KV_W, rows)),
            per_layer(dec_batch, CONV_W),
            per_layer(dec_batch, CONV_W),
            per_layer(1, D_MODEL),
            pl.BlockSpec((None, D_MODEL, IN_W), layer),
            per_layer(CONV_K, CONV_W),
            per_layer(D_MODEL, D_MODEL),
            per_layer(D_MODEL, D_MODEL),
            per_layer(PLE_DIM, D_MODEL),
            full((1, D_MODEL)),
            cache,
            cache,
        ],
        out_specs=[
            pl.BlockSpec((rows, D_MODEL), const),
            pl.BlockSpec((None, dec_batch, CONV_W), layer),
            pl.BlockSpec((None, dec_batch, CONV_W), layer),
            cache,
            cache,
        ],
        out_shape=[
            jax.ShapeDtypeStruct((rows, D_MODEL), F32),
            jax.ShapeDtypeStruct((DEPTH, dec_batch, CONV_W), F32),
            jax.ShapeDtypeStruct((DEPTH, dec_batch, CONV_W), F32),
            jax.ShapeDtypeStruct((DEPTH, dec_batch, KV_W, WINDOW), F32),
            jax.ShapeDtypeStruct((DEPTH, dec_batch, KV_W, WINDOW), F32),
        ],
        scratch_shapes=[
            pltpu.VMEM((rows, D_MODEL), F32),
            pltpu.VMEM((rows, ATTN_W), BF),
            pltpu.VMEM((KV_W, rows), F32),
            pltpu.VMEM((KV_W, rows), F32),
            pltpu.VMEM((rows, ATTN_W), F32),
            pltpu.VMEM((rows, D_MODEL), BF),
            pltpu.VMEM((CONV_W // LANES, rows + SUBLANES, LANES), F32),
            pltpu.VMEM((CONV_W // LANES, rows, LANES), F32),
            pltpu.VMEM((CONV_W // LANES, rows, LANES), F32),
        ],
        compiler_params=pltpu.CompilerParams(
            dimension_semantics=("arbitrary", "arbitrary"), vmem_limit_bytes=VMEM_LIMIT),
    )(sinks, x, p_all, cos, sin_s, cos_t, sin_t, st0, st1, g_norm, w_in, conv_w, w_out, w_pg, w_pp, g_final,
      cache_kt, cache_vt)


def _rope_angles(pos):
    inv = ROPE_THETA ** (-jnp.arange(0, HEAD_DIM, 2, dtype=F32) / HEAD_DIM)
    return pos[:, None] * inv[None, :]


def _rope_tables(pos):
    ang = _rope_angles(pos)
    cos = jnp.tile(jnp.cos(ang), (1, 2 * LANES // HEAD_DIM))
    sin = jnp.sin(ang)
    sin_signed = jnp.tile(jnp.concatenate([-sin, sin], axis=-1), (1, LANES // HEAD_DIM))
    return cos, sin_signed


def _rope_tables_t(pos):
    ang = _rope_angles(pos).T
    cos = jnp.tile(jnp.cos(ang), (2 * KV_W // HEAD_DIM, 1))
    sin = jnp.sin(ang)
    sin_signed = jnp.tile(jnp.concatenate([-sin, sin], axis=0), (KV_W // HEAD_DIM, 1))
    return cos, sin_signed


def kernel(x_prompt, x_sample, cache_k, cache_v, state_conv, p_prompt, p_sample, g_norm, w_in, sinks, conv_w,
           w_out, w_pg, w_pp, g_final):
    batch, seq, _ = x_prompt.shape
    dec_batch, t_dec, _ = x_sample.shape
    rows = dec_batch * t_dec

    cos_p, sin_p = _rope_tables(jnp.arange(seq, dtype=F32))
    pos_s = jnp.tile(PAST_LEN + jnp.arange(t_dec, dtype=F32), dec_batch)
    cos_s, sin_s = _rope_tables(pos_s)
    cos_st, sin_st = _rope_tables_t(pos_s)

    w_in_b, w_out_b, w_pg_b, w_pp_b = (w.astype(BF) for w in (w_in, w_out, w_pg, w_pp))
    g_norm3 = g_norm.reshape(DEPTH, 1, D_MODEL)
    g_fin2 = g_final.reshape(1, D_MODEL)
    to_t = lambda c: jnp.transpose(c, (0, 1, 3, 4, 2)).reshape(DEPTH, -1, KV_W, WINDOW)
    from_t = lambda c: jnp.transpose(c.reshape(DEPTH, -1, N_KV_HEADS, HEAD_DIM, WINDOW), (0, 1, 4, 2, 3))

    xp = x_prompt
    nkp, nvp, ncp = [], [], []
    for i in range(DEPTH):
        xp, kt, vt, ut = _prompt_layer(i, xp, p_prompt, cos_p, sin_p, sinks, g_norm3, w_in_b, conv_w, w_out_b,
                                       w_pg_b, w_pp_b, g_fin2, i == DEPTH - 1)
        nkp.append(kt)
        nvp.append(vt)
        ncp.append(ut)

    xs, nc0, nc1, nks, nvs = _sample_trunk(
        x_sample.reshape(rows, D_MODEL), p_sample.reshape(DEPTH, rows, PLE_DIM), cos_s, sin_s, cos_st, sin_st,
        state_conv[:, :, 0], state_conv[:, :, 1], sinks, g_norm3, w_in_b, conv_w, w_out_b, w_pg_b, w_pp_b,
        g_fin2, to_t(cache_k), to_t(cache_v), t_dec)

    return (xp, xs.reshape(dec_batch, t_dec, D_MODEL), from_t(jnp.stack(nkp)), from_t(jnp.stack(nvp)),
            jnp.stack(ncp), from_t(nks), from_t(nvs), jnp.stack([nc0, nc1], axis=2))
```

```python
import functools

import jax
import jax.numpy as jnp
from jax import lax
from jax.experimental import pallas as pl
from jax.experimental.pallas import tpu as pltpu

D_MODEL = 1024
DEPTH = 4
N_HEADS = 8
HEAD_DIM = 64
N_KV_HEADS = 2
GROUP = N_HEADS // N_KV_HEADS
ATTN_W = N_HEADS * HEAD_DIM
KV_W = N_KV_HEADS * HEAD_DIM
CONV_W = D_MODEL - ATTN_W
CONV_K = 3
WINDOW = 128
BLOCK = 128
PLE_DIM = 256
PAST_LEN = 8192
ROPE_THETA = 10000.0
EPS = 1e-6
SCALE = HEAD_DIM ** -0.5
NEG = -1e30
IN_W = 2 * ATTN_W + 2 * KV_W + 4 * CONV_W

C_Q = 0
C_K = ATTN_W
C_V = C_K + KV_W
C_GA = C_V + KV_W
C_BG = C_GA + ATTN_W
C_CG = C_BG + CONV_W
C_HC = C_CG + CONV_W
C_GC = C_HC + CONV_W

LANES = 128
SUBLANES = 8
ROPE_HALF = HEAD_DIM // 2

MXU_N = 256
PROMPT_TILE = 512
SAMPLE_GROUP = 8
VMEM_LIMIT = 56 * 1024 * 1024

BF = jnp.bfloat16
F32 = jnp.float32
_NT = (((1,), (1,)), ((), ()))


def _mm(a, w):
    return jnp.dot(a, w, preferred_element_type=F32)


def _sigmoid(x):
    return 1.0 / (1.0 + jnp.exp(-x))


def _silu(x):
    return x * _sigmoid(x)


def _rmsnorm(x, g):
    r = lax.rsqrt(jnp.mean(x * x, axis=-1, keepdims=True) + EPS)
    return (x * r) * g


def _rope(x, cos, sin_signed):
    rows = x.shape[0]
    lane = lax.broadcasted_iota(jnp.int32, (rows, LANES), 1)
    first_half = (lane % HEAD_DIM) < ROPE_HALF
    outs = []
    for c in range(x.shape[1] // LANES):
        xc = x[:, c * LANES:(c + 1) * LANES]
        rot = jnp.where(first_half, pltpu.roll(xc, LANES - ROPE_HALF, 1), pltpu.roll(xc, ROPE_HALF, 1))
        outs.append(xc * cos + rot * sin_signed)
    return outs[0] if len(outs) == 1 else jnp.concatenate(outs, axis=1)


def _conv_gate(u, u1, u2, cw_ref, bg, gc):
    yc = cw_ref[0:1, :] * u2 + cw_ref[1:2, :] * u1 + cw_ref[2:3, :] * u
    return bg * yc * _silu(gc)


def _prompt_kernel(sink_ref, xa_ref, xb_ref, p_ref, cos_ref, sin_ref, gn_ref, win_ref, cw_ref, wout_ref, wpg_ref,
                   wpp_ref, gfin_ref, xo_ref, kt_ref, vt_ref, ut_ref, kbuf, vbuf, ubuf, mix_buf, xn_buf, zc_buf,
                   q_buf, sga_buf, x1_buf, x1b_buf, *, tile, n_seq, final):
    i = pl.program_id(0)
    n_tiles = pl.num_programs(0) - 1
    first_of_seq = jnp.minimum(i, n_tiles - 1) % n_seq == 0

    @pl.when(i == 0)
    def _():
        mix_buf[...] = jnp.zeros(mix_buf.shape, BF)

    @pl.when(first_of_seq)
    def _():
        kbuf[0:BLOCK, :] = jnp.zeros((BLOCK, KV_W), BF)
        vbuf[0:BLOCK, :] = jnp.zeros((BLOCK, KV_W), BF)
        ubuf[0:SUBLANES, :] = jnp.zeros((SUBLANES, CONV_W), F32)

    n_out = D_MODEL // MXU_N
    rows_per = tile // n_out
    for c in range(n_out):
        cs = slice(c * MXU_N, (c + 1) * MXU_N)
        x1 = (xb_ref[0, :, cs] + _mm(mix_buf[:, 0:ATTN_W], wout_ref[0:ATTN_W, cs])
              + _mm(mix_buf[:, ATTN_W:D_MODEL], wout_ref[ATTN_W:D_MODEL, cs]))
        x1_buf[:, cs] = x1
        x1b_buf[:, cs] = x1.astype(BF)
        rs = slice(c * rows_per, (c + 1) * rows_per)
        xn_buf[rs, :] = _rmsnorm(xa_ref[0, rs, :], gn_ref[...]).astype(BF)
    pb = p_ref[0].astype(BF)

    def gate_unit(c):
        cs = slice(c * MXU_N, (c + 1) * MXU_N)
        gate = _sigmoid(_mm(x1b_buf[...], wpg_ref[:, cs]))
        xo_ref[0, :, cs] = x1_buf[:, cs] + gate * _mm(pb, wpp_ref[:, cs])

    cos = cos_ref[...]
    sin_s = sin_ref[...]

    def proj(c0, width=MXU_N):
        return _mm(xn_buf[...], win_ref[:, c0:c0 + width])

    kv = proj(C_K, 2 * KV_W)
    k = _rope(kv[:, 0:KV_W], cos, sin_s)
    kbuf[BLOCK:BLOCK + tile, :] = k.astype(BF)
    v = kv[:, KV_W:2 * KV_W]
    vbuf[BLOCK:BLOCK + tile, :] = v.astype(BF)
    kt_ref[0] = k[tile - BLOCK:, :].T
    vt_ref[0] = v[tile - BLOCK:, :].T
    for c in range(ATTN_W // MXU_N):
        cs = slice(c * MXU_N, (c + 1) * MXU_N)
        q_buf[:, cs] = (_rope(proj(C_Q + c * MXU_N), cos, sin_s) * SCALE).astype(BF)

    conv_cols = (C_CG, C_HC, C_BG, C_GC)
    n_planes = len(conv_cols)
    n_slabs = CONV_W // MXU_N
    n_gate = ATTN_W // MXU_N
    n_dense = n_gate + n_slabs * n_planes

    def dense_unit(n):
        if n < n_gate:
            sga_buf[:, n * MXU_N:(n + 1) * MXU_N] = _silu(proj(C_GA + n * MXU_N))
        else:
            half, plane = divmod(n - n_gate, n_planes)
            zc_buf[plane, :, half * MXU_N:(half + 1) * MXU_N] = proj(conv_cols[plane] + half * MXU_N)

    def conv_unit(half):
        cs = slice(half * MXU_N, (half + 1) * MXU_N)
        u = zc_buf[0, :, cs] * zc_buf[1, :, cs]
        ubuf[SUBLANES:SUBLANES + tile, cs] = u
        u1 = ubuf[SUBLANES - 1:SUBLANES - 1 + tile, cs]
        u2 = ubuf[SUBLANES - 2:SUBLANES - 2 + tile, cs]
        yc = cw_ref[0:1, cs] * u2 + cw_ref[1:2, cs] * u1 + cw_ref[2:3, cs] * u
        mix_buf[:, ATTN_W + half * MXU_N:ATTN_W + (half + 1) * MXU_N] = (
            zc_buf[2, :, cs] * yc * _silu(zc_buf[3, :, cs])).astype(BF)
        ubuf[0:SUBLANES, cs] = ubuf[tile:tile + SUBLANES, cs]
        ut_ref[0, :, cs] = u[tile - (CONV_K - 1):, :]

    row = lax.broadcasted_iota(jnp.int32, (BLOCK, BLOCK), 0)
    col = lax.broadcasted_iota(jnp.int32, (BLOCK, BLOCK), 1)
    own = col <= row
    no_prev = jnp.where(first_of_seq, NEG, 0.0).astype(F32)

    def scores(n):
        j, g = divmod(n, N_KV_HEADS)
        kcat = kbuf[j * BLOCK:(j + 2) * BLOCK, g * HEAD_DIM:(g + 1) * HEAD_DIM]
        qst = jnp.concatenate(
            [q_buf[j * BLOCK:(j + 1) * BLOCK, (GROUP * g + h) * HEAD_DIM:(GROUP * g + h + 1) * HEAD_DIM]
             for h in range(GROUP)], axis=0)
        return lax.dot_general(qst, kcat, _NT, preferred_element_type=F32)

    def softmax(n, sc):
        j, g = divmod(n, N_KV_HEADS)
        probs, invs = [], []
        for h in range(GROUP):
            sh = sc[h * BLOCK:(h + 1) * BLOCK]
            s_prev = sh[:, :BLOCK]
            if j == 0:
                s_prev = s_prev + no_prev
            sm = jnp.where(own, sh[:, BLOCK:], s_prev)
            sink = sink_ref[GROUP * g + h]
            m = jnp.maximum(jnp.max(sm, axis=-1, keepdims=True), sink)
            e = jnp.exp(sm - m)
            den = jnp.sum(e, axis=-1, keepdims=True) + jnp.exp(sink - m)
            invs.append(1.0 / den)
            eb = e.astype(BF)
            zero = jnp.zeros_like(eb)
            probs.append(jnp.concatenate([jnp.where(own, zero, eb), jnp.where(own, eb, zero)], axis=1))
        return jnp.concatenate(probs, axis=0), invs

    heads = {}

    def values(n, probs, invs):
        j, g = divmod(n, N_KV_HEADS)
        vcat = vbuf[j * BLOCK:(j + 2) * BLOCK, g * HEAD_DIM:(g + 1) * HEAD_DIM]
        o = _mm(probs, vcat)
        heads.setdefault(j, []).extend(o[h * BLOCK:(h + 1) * BLOCK] * invs[h] for h in range(GROUP))
        if g == N_KV_HEADS - 1:
            rows = slice(j * BLOCK, (j + 1) * BLOCK)
            oa = jnp.concatenate(heads.pop(j), axis=1)
            mix_buf[rows, 0:ATTN_W] = (oa * sga_buf[rows, :]).astype(BF)

    n_units = (tile // BLOCK) * N_KV_HEADS
    conv_at = {n_gate + (h + 1) * n_planes: h for h in range(n_slabs)}
    n_steps = max(n_units + 2, max(conv_at) + 1)
    gate_at = {2 * c + 1: c for c in range(n_out)}
    sc_q, sm_q = {}, {}
    for t in range(n_steps):
        if t < n_units:
            sc_q[t] = scores(t)
        if t < n_dense:
            dense_unit(t)
        if 0 <= t - 2 < n_units:
            values(t - 2, *sm_q.pop(t - 2))
        if t in gate_at:
            gate_unit(gate_at[t])
        if 0 <= t - 1 < n_units:
            sm_q[t - 1] = softmax(t - 1, sc_q.pop(t - 1))
        if t in conv_at:
            conv_unit(conv_at[t])

    kbuf[0:BLOCK, :] = kbuf[tile:tile + BLOCK, :]
    vbuf[0:BLOCK, :] = vbuf[tile:tile + BLOCK, :]

    if final:
        xo_ref[0] = _rmsnorm(xo_ref[0], gfin_ref[...])


def _prompt_layer(i, x, p_all, cos, sin_s, sinks, g_norm, w_in, conv_w, w_out, w_pg, w_pp, g_final, final):
    batch, seq, _ = x.shape
    tile = PROMPT_TILE
    n_seq = seq // tile
    n_tiles = batch * n_seq
    in_tile = lambda t: jnp.minimum(t, n_tiles - 1)
    out_tile = lambda t: jnp.maximum(t - 1, 0)
    const = lambda t: (0, 0)
    layer = lambda t: (i, 0, 0)
    resident = dict(pipeline_mode=pl.Buffered(1))
    return pl.pallas_call(
        functools.partial(_prompt_kernel, tile=tile, n_seq=n_seq, final=final),
        name=f"prompt_layer{i}",
        grid=(n_tiles + 1,),
        in_specs=[
            pl.BlockSpec(memory_space=pltpu.SMEM),
            pl.BlockSpec((1, tile, D_MODEL), lambda t: (in_tile(t) // n_seq, in_tile(t) % n_seq, 0)),
            pl.BlockSpec((1, tile, D_MODEL), lambda t: (out_tile(t) // n_seq, out_tile(t) % n_seq, 0)),
            pl.BlockSpec((None, 1, tile, PLE_DIM),
                         lambda t: (i, out_tile(t) // n_seq, out_tile(t) % n_seq, 0)),
            pl.BlockSpec((tile, LANES), lambda t: (in_tile(t) % n_seq, 0)),
            pl.BlockSpec((tile, LANES), lambda t: (in_tile(t) % n_seq, 0)),
            pl.BlockSpec((None, 1, D_MODEL), layer, **resident),
            pl.BlockSpec((None, D_MODEL, IN_W), layer, **resident),
            pl.BlockSpec((None, CONV_K, CONV_W), layer, **resident),
            pl.BlockSpec((None, D_MODEL, D_MODEL), layer, **resident),
            pl.BlockSpec((None, D_MODEL, D_MODEL), layer, **resident),
            pl.BlockSpec((None, PLE_DIM, D_MODEL), layer, **resident),
            pl.BlockSpec((1, D_MODEL), const, **resident),
        ],
        out_specs=[
            pl.BlockSpec((1, tile, D_MODEL), lambda t: (out_tile(t) // n_seq, out_tile(t) % n_seq, 0)),
            pl.BlockSpec((1, BLOCK, KV_W), lambda t: (in_tile(t) // n_seq, 0, 0)),
            pl.BlockSpec((1, BLOCK, KV_W), lambda t: (in_tile(t) // n_seq, 0, 0)),
            pl.BlockSpec((1, CONV_K - 1, CONV_W), lambda t: (in_tile(t) // n_seq, 0, 0)),
        ],
        out_shape=[
            jax.ShapeDtypeStruct((batch, seq, D_MODEL), F32),
            jax.ShapeDtypeStruct((batch, BLOCK, KV_W), F32),
            jax.ShapeDtypeStruct((batch, BLOCK, KV_W), F32),
            jax.ShapeDtypeStruct((batch, CONV_K - 1, CONV_W), F32),
        ],
        scratch_shapes=[
            pltpu.VMEM((tile + BLOCK, KV_W), BF),
            pltpu.VMEM((tile + BLOCK, KV_W), BF),
            pltpu.VMEM((tile + SUBLANES, CONV_W), F32),
            pltpu.VMEM((tile, D_MODEL), BF),
            pltpu.VMEM((tile, D_MODEL), BF),
            pltpu.VMEM((4, tile, CONV_W), F32),
            pltpu.VMEM((tile, ATTN_W), BF),
            pltpu.VMEM((tile, ATTN_W), F32),
            pltpu.VMEM((tile, D_MODEL), F32),
            pltpu.VMEM((tile, D_MODEL), BF),
        ],
        compiler_params=pltpu.CompilerParams(dimension_semantics=("arbitrary",), vmem_limit_bytes=VMEM_LIMIT),
    )(sinks[i], x, x, p_all, cos, sin_s, g_norm, w_in, conv_w, w_out, w_pg, w_pp, g_final)


def _rope_t(x, cos_t, sin_t):
    parts = [x[r:r + ROPE_HALF] for r in range(0, x.shape[0], ROPE_HALF)]
    swapped = jnp.concatenate([parts[n ^ 1] for n in range(len(parts))], axis=0)
    return x * cos_t + swapped * sin_t


def _sample_kernel(sink_ref, x_ref, p_ref, cos_ref, sin_ref, cost_ref, sint_ref, st0_ref, st1_ref, gn_ref, win_ref,
                   cw_ref, wout_ref, wpg_ref, wpp_ref, gfin_ref, ck_ref, cv_ref,
                   xo_ref, nc0_ref, nc1_ref, nk_ref, nv_ref,
                   xs_buf, q_s, kt_s, vt_s, sga_s, mix_s, ubuf, p1_buf, p2_buf, *, rows, t_dec, chunk, nb):
    layer = pl.program_id(0)
    step = pl.program_id(1)
    last_layer = pl.num_programs(0) - 1
    last_step = pl.num_programs(1) - 1
    dec_batch = rows // t_dec
    gr = nb * t_dec
    cols = chunk * t_dec

    @pl.when((layer == 0) & (step == 0))
    def _():
        xs_buf[...] = x_ref[...]

    @pl.when(step == 0)
    def _():
        xn = _rmsnorm(xs_buf[...], gn_ref[...]).astype(BF)
        q_s[...] = (_rope(_mm(xn, win_ref[:, C_Q:C_K]), cos_ref[...], sin_ref[...]) * SCALE).astype(BF)
        kvt = lax.dot_general(win_ref[:, C_K:C_GA], xn, (((0,), (1,)), ((), ())),
                              preferred_element_type=F32)
        kt_s[...] = _rope_t(kvt[0:KV_W], cost_ref[...], sint_ref[...])
        vt_s[...] = kvt[KV_W:2 * KV_W]
        sga_s[...] = _silu(_mm(xn, win_ref[:, C_GA:C_BG]))
        u = _mm(xn, win_ref[:, C_CG:C_HC]) * _mm(xn, win_ref[:, C_HC:C_GC])
        per_batch = lambda first: pl.ds(first, dec_batch, stride=t_dec)
        for c in range(CONV_W // LANES):
            cs = slice(c * LANES, (c + 1) * LANES)
            ubuf[c, 0:SUBLANES, :] = jnp.zeros((SUBLANES, LANES), F32)
            ubuf[c, SUBLANES:SUBLANES + rows, :] = u[:, cs]
            p1_buf[c] = ubuf[c, SUBLANES - 1:SUBLANES - 1 + rows, :]
            p2_buf[c] = ubuf[c, SUBLANES - 2:SUBLANES - 2 + rows, :]
            p1_buf[c, per_batch(0), :] = st1_ref[:, cs]
            p2_buf[c, per_batch(0), :] = st0_ref[:, cs]
            p2_buf[c, per_batch(1), :] = st1_ref[:, cs]
            nc0_ref[:, cs] = ubuf[c, per_batch(SUBLANES + t_dec - 2), :]
            nc1_ref[:, cs] = ubuf[c, per_batch(SUBLANES + t_dec - 1), :]
        u1 = jnp.concatenate([p1_buf[c] for c in range(CONV_W // LANES)], axis=1)
        u2 = jnp.concatenate([p2_buf[c] for c in range(CONV_W // LANES)], axis=1)
        bg = _mm(xn, win_ref[:, C_BG:C_CG])
        gc = _mm(xn, win_ref[:, C_GC:IN_W])
        mix_s[:, ATTN_W:D_MODEL] = _conv_gate(u, u1, u2, cw_ref, bg, gc).astype(BF)

    c0 = pl.multiple_of(step * cols, cols)
    kt_new = kt_s[:, pl.ds(c0, cols)]
    vt_new = vt_s[:, pl.ds(c0, cols)]

    qrow = lax.broadcasted_iota(jnp.int32, (gr, nb * WINDOW), 0)
    ccol = lax.broadcasted_iota(jnp.int32, (gr, nb * WINDOW), 1)
    valid_c = ((ccol // WINDOW) == (qrow // t_dec)) & ((ccol % WINDOW) > (qrow % t_dec))
    qrow_n = lax.broadcasted_iota(jnp.int32, (gr, gr), 0)
    ncol = lax.broadcasted_iota(jnp.int32, (gr, gr), 1)
    valid_n = ((ncol // t_dec) == (qrow_n // t_dec)) & ((ncol % t_dec) <= (qrow_n % t_dec))
    lane = lax.broadcasted_iota(jnp.int32, (KV_W, WINDOW), 1)
    kept = lane < WINDOW - t_dec

    def softmax(g, sc, sn):
        ecs, ens, invs = [], [], []
        for h in range(GROUP):
            scm = jnp.where(valid_c, sc[h * gr:(h + 1) * gr], NEG)
            snm = jnp.where(valid_n, sn[h * gr:(h + 1) * gr], NEG)
            sink = sink_ref[layer, GROUP * g + h]
            m = jnp.maximum(jnp.maximum(jnp.max(scm, axis=-1, keepdims=True),
                                        jnp.max(snm, axis=-1, keepdims=True)), sink)
            ec = jnp.exp(scm - m)
            en = jnp.exp(snm - m)
            den = (jnp.sum(ec, axis=-1, keepdims=True) + jnp.sum(en, axis=-1, keepdims=True)
                   + jnp.exp(sink - m))
            invs.append(1.0 / den)
            ecs.append(ec.astype(BF))
            ens.append(en.astype(BF))
        return jnp.concatenate(ecs, axis=0), jnp.concatenate(ens, axis=0), invs

    def group(sg, carry):
        rows = pl.ds(pl.multiple_of(step * cols + sg * gr, gr), gr)
        qg = q_s[rows, :]
        back = (cols - sg * gr) % cols
        kt_g = pltpu.roll(kt_new, back, 1)
        vt_g = pltpu.roll(vt_new, back, 1)
        heads = []
        for g in range(N_KV_HEADS):
            hd = slice(g * HEAD_DIM, (g + 1) * HEAD_DIM)
            ktc = jnp.concatenate([ck_ref[sg * nb + b, hd, :] for b in range(nb)], axis=1).astype(BF)
            vtc = jnp.concatenate([cv_ref[sg * nb + b, hd, :] for b in range(nb)], axis=1).astype(BF)
            ktn = kt_g[hd, 0:gr].astype(BF)
            vtn = vt_g[hd, 0:gr].astype(BF)
            qst = jnp.concatenate(
                [qg[:, (GROUP * g + h) * HEAD_DIM:(GROUP * g + h + 1) * HEAD_DIM] for h in range(GROUP)], axis=0)
            ec, en, invs = softmax(g, _mm(qst, ktc), _mm(qst, ktn))
            o = (lax.dot_general(ec, vtc, _NT, preferred_element_type=F32)
                 + lax.dot_general(en, vtn, _NT, preferred_element_type=F32))
            heads.extend(o[h * gr:(h + 1) * gr] * invs[h] for h in range(GROUP))
        oa = jnp.concatenate(heads, axis=1)
        mix_s[rows, 0:ATTN_W] = (oa * sga_s[rows, :]).astype(BF)

        for b in range(nb):
            place = (WINDOW - t_dec - t_dec * b) % cols
            nk_ref[sg * nb + b] = jnp.where(kept, pltpu.roll(ck_ref[sg * nb + b], WINDOW - t_dec, 1),
                                            pltpu.roll(kt_g, place, 1))
            nv_ref[sg * nb + b] = jnp.where(kept, pltpu.roll(cv_ref[sg * nb + b], WINDOW - t_dec, 1),
                                            pltpu.roll(vt_g, place, 1))
        return carry

    lax.fori_loop(0, chunk // nb, group, 0)

    @pl.when(step == last_step)
    def _():
        x1 = xs_buf[...] + _mm(mix_s[...], wout_ref[...])
        gate = _sigmoid(_mm(x1.astype(BF), wpg_ref[...]))
        x2 = x1 + gate * _mm(p_ref[...].astype(BF), wpp_ref[...])
        xs_buf[...] = x2

        @pl.when(layer == last_layer)
        def _():
            xo_ref[...] = _rmsnorm(x2, gfin_ref[...])


def _sample_trunk(x, p_all, cos, sin_s, cos_t, sin_t, st0, st1, sinks, g_norm, w_in, conv_w, w_out, w_pg, w_pp,
                  g_final, cache_kt, cache_vt, t_dec):
    rows = x.shape[0]
    dec_batch = rows // t_dec
    chunk = LANES // t_dec
    nb = SAMPLE_GROUP
    const = lambda l, s: (0, 0)
    layer = lambda l, s: (l, 0, 0)
    once = dict(pipeline_mode=pl.Buffered(1))
    full = lambda shape: pl.BlockSpec(shape, const, **once)
    per_layer = lambda *shape: pl.BlockSpec((None,) + shape, layer, **once)
    cache = pl.BlockSpec((None, chunk, KV_W, WINDOW), lambda l, s: (l, s, 0, 0))
    return pl.pallas_call(
        functools.partial(_sample_kernel, rows=rows, t_dec=t_dec, chunk=chunk, nb=nb),
        name="sample_trunk",
        grid=(DEPTH, dec_batch // chunk),
        in_specs=[
            pl.BlockSpec(memory_space=pltpu.SMEM),
            full((rows, D_MODEL)),
            per_layer(rows, PLE_DIM),
            full((rows, LANES)),
            full((rows, LANES)),
            full((KV_W, rows)),
            full((KV_W, rows)),
            per_layer(dec_batch, CONV_W),
            per_layer(dec_batch, CONV_W),
            per_layer(1, D_MODEL),
            pl.BlockSpec((None, D_MODEL, IN_W), layer),
            per_layer(CONV_K, CONV_W),
            per_layer(D_MODEL, D_MODEL),
            per_layer(D_MODEL, D_MODEL),
            per_layer(PLE_DIM, D_MODEL),
            full((1, D_MODEL)),
            cache,
            cache,
        ],
        out_specs=[
            pl.BlockSpec((rows, D_MODEL), const),
            pl.BlockSpec((None, dec_batch, CONV_W), layer),
            pl.BlockSpec((None, dec_batch, CONV_W), layer),
            cache,
            cache,
        ],
        out_shape=[
            jax.ShapeDtypeStruct((rows, D_MODEL), F32),
            jax.ShapeDtypeStruct((DEPTH, dec_batch, CONV_W), F32),
            jax.ShapeDtypeStruct((DEPTH, dec_batch, CONV_W), F32),
            jax.ShapeDtypeStruct((DEPTH, dec_batch, KV_W, WINDOW), F32),
            jax.ShapeDtypeStruct((DEPTH, dec_batch, KV_W, WINDOW), F32),
        ],
        scratch_shapes=[
            pltpu.VMEM((rows, D_MODEL), F32),
            pltpu.VMEM((rows, ATTN_W), BF),
            pltpu.VMEM((KV_W, rows), F32),
            pltpu.VMEM((KV_W, rows), F32),
            pltpu.VMEM((rows, ATTN_W), F32),
            pltpu.VMEM((rows, D_MODEL), BF),
            pltpu.VMEM((CONV_W // LANES, rows + SUBLANES, LANES), F32),
            pltpu.VMEM((CONV_W // LANES, rows, LANES), F32),
            pltpu.VMEM((CONV_W // LANES, rows, LANES), F32),
        ],
        compiler_params=pltpu.CompilerParams(
            dimension_semantics=("arbitrary", "arbitrary"), vmem_limit_bytes=VMEM_LIMIT),
    )(sinks, x, p_all, cos, sin_s, cos_t, sin_t, st0, st1, g_norm, w_in, conv_w, w_out, w_pg, w_pp, g_final,
      cache_kt, cache_vt)


def _rope_angles(pos):
    inv = ROPE_THETA ** (-jnp.arange(0, HEAD_DIM, 2, dtype=F32) / HEAD_DIM)
    return pos[:, None] * inv[None, :]


def _rope_tables(pos):
    ang = _rope_angles(pos)
    cos = jnp.tile(jnp.cos(ang), (1, 2 * LANES // HEAD_DIM))
    sin = jnp.sin(ang)
    sin_signed = jnp.tile(jnp.concatenate([-sin, sin], axis=-1), (1, LANES // HEAD_DIM))
    return cos, sin_signed


def _rope_tables_t(pos):
    ang = _rope_angles(pos).T
    cos = jnp.tile(jnp.cos(ang), (2 * KV_W // HEAD_DIM, 1))
    sin = jnp.sin(ang)
    sin_signed = jnp.tile(jnp.concatenate([-sin, sin], axis=0), (KV_W // HEAD_DIM, 1))
    return cos, sin_signed


def kernel(x_prompt, x_sample, cache_k, cache_v, state_conv, p_prompt, p_sample, g_norm, w_in, sinks, conv_w,
           w_out, w_pg, w_pp, g_final):
    batch, seq, _ = x_prompt.shape
    dec_batch, t_dec, _ = x_sample.shape
    rows = dec_batch * t_dec

    cos_p, sin_p = _rope_tables(jnp.arange(seq, dtype=F32))
    pos_s = jnp.tile(PAST_LEN + jnp.arange(t_dec, dtype=F32), dec_batch)
    cos_s, sin_s = _rope_tables(pos_s)
    cos_st, sin_st = _rope_tables_t(pos_s)

    w_in_b, w_out_b, w_pg_b, w_pp_b = (w.astype(BF) for w in (w_in, w_out, w_pg, w_pp))
    g_norm3 = g_norm.reshape(DEPTH, 1, D_MODEL)
    g_fin2 = g_final.reshape(1, D_MODEL)
    to_t = lambda c: jnp.transpose(c, (0, 1, 3, 4, 2)).reshape(DEPTH, -1, KV_W, WINDOW)
    from_t = lambda c: jnp.transpose(c.reshape(DEPTH, -1, N_KV_HEADS, HEAD_DIM, WINDOW), (0, 1, 4, 2, 3))

    xp = x_prompt
    nkp, nvp, ncp = [], [], []
    for i in range(DEPTH):
        xp, kt, vt, ut = _prompt_layer(i, xp, p_prompt, cos_p, sin_p, sinks, g_norm3, w_in_b, conv_w, w_out_b,
                                       w_pg_b, w_pp_b, g_fin2, i == DEPTH - 1)
        nkp.append(kt)
        nvp.append(vt)
        ncp.append(ut)

    xs, nc0, nc1, nks, nvs = _sample_trunk(
        x_sample.reshape(rows, D_MODEL), p_sample.reshape(DEPTH, rows, PLE_DIM), cos_s, sin_s, cos_st, sin_st,
        state_conv[:, :, 0], state_conv[:, :, 1], sinks, g_norm3, w_in_b, conv_w, w_out_b, w_pg_b, w_pp_b,
        g_fin2, to_t(cache_k), to_t(cache_v), t_dec)

    return (xp, xs.reshape(dec_batch, t_dec, D_MODEL), from_t(jnp.stack(nkp)), from_t(jnp.stack(nvp)),
            jnp.stack(ncp), from_t(nks), from_t(nvs), jnp.stack([nc0, nc1], axis=2))
```

```python
import functools

import jax
import jax.numpy as jnp
from jax import lax
from jax.experimental import pallas as pl
from jax.experimental.pallas import tpu as pltpu

D_MODEL = 1024
DEPTH = 4
N_HEADS = 8
HEAD_DIM = 64
N_KV_HEADS = 2
GROUP = N_HEADS // N_KV_HEADS
ATTN_W = N_HEADS * HEAD_DIM
KV_W = N_KV_HEADS * HEAD_DIM
CONV_W = D_MODEL - ATTN_W
CONV_K = 3
WINDOW = 128
BLOCK = 128
PLE_DIM = 256
PAST_LEN = 8192
ROPE_THETA = 10000.0
EPS = 1e-6
SCALE = HEAD_DIM ** -0.5
NEG = -1e30
IN_W = 2 * ATTN_W + 2 * KV_W + 4 * CONV_W

C_Q = 0
C_K = ATTN_W
C_V = C_K + KV_W
C_GA = C_V + KV_W
C_BG = C_GA + ATTN_W
C_CG = C_BG + CONV_W
C_HC = C_CG + CONV_W
C_GC = C_HC + CONV_W

LANES = 128
SUBLANES = 8
ROPE_HALF = HEAD_DIM // 2

MXU_N = 256
PROMPT_TILE = 512
SAMPLE_GROUP = 8
VMEM_LIMIT = 56 * 1024 * 1024
TRUNK_VMEM_LIMIT = 60 * 1024 * 1024

BF = jnp.bfloat16
F32 = jnp.float32
_NT = (((1,), (1,)), ((), ()))


def _mm(a, w):
    return jnp.dot(a, w, preferred_element_type=F32)


def _sigmoid(x):
    return 1.0 / (1.0 + jnp.exp(-x))


def _silu(x):
    return x * _sigmoid(x)


def _rmsnorm(x, g):
    r = lax.rsqrt(jnp.mean(x * x, axis=-1, keepdims=True) + EPS)
    return (x * r) * g


def _rope(x, cos, sin_signed):
    rows = x.shape[0]
    lane = lax.broadcasted_iota(jnp.int32, (rows, LANES), 1)
    first_half = (lane % HEAD_DIM) < ROPE_HALF
    outs = []
    for c in range(x.shape[1] // LANES):
        xc = x[:, c * LANES:(c + 1) * LANES]
        rot = jnp.where(first_half, pltpu.roll(xc, LANES - ROPE_HALF, 1), pltpu.roll(xc, ROPE_HALF, 1))
        outs.append(xc * cos + rot * sin_signed)
    return outs[0] if len(outs) == 1 else jnp.concatenate(outs, axis=1)


def _conv_gate(u, u1, u2, cw_ref, bg, gc):
    yc = cw_ref[0:1, :] * u2 + cw_ref[1:2, :] * u1 + cw_ref[2:3, :] * u
    return bg * yc * _silu(gc)


def _prompt_kernel(sink_ref, xa_ref, xb_ref, p_ref, cos_ref, sin_ref, gn_ref, win_ref, cw_ref, wout_ref, wpg_ref,
                   wpp_ref, gfin_ref, xo_ref, kt_ref, vt_ref, ut_ref, kbuf, vbuf, ubuf, mix_buf, xn_buf, zc_buf,
                   q_buf, sga_buf, x1_buf, x1b_buf, *, tile, n_seq, final):
    i = pl.program_id(0)
    n_tiles = pl.num_programs(0) - 1
    first_of_seq = jnp.minimum(i, n_tiles - 1) % n_seq == 0

    @pl.when(i == 0)
    def _():
        mix_buf[...] = jnp.zeros(mix_buf.shape, BF)

    @pl.when(first_of_seq)
    def _():
        kbuf[0:BLOCK, :] = jnp.zeros((BLOCK, KV_W), BF)
        vbuf[0:BLOCK, :] = jnp.zeros((BLOCK, KV_W), BF)
        ubuf[0:SUBLANES, :] = jnp.zeros((SUBLANES, CONV_W), F32)

    n_out = D_MODEL // MXU_N
    rows_per = tile // n_out
    for c in range(n_out):
        cs = slice(c * MXU_N, (c + 1) * MXU_N)
        x1 = (xb_ref[0, :, cs] + _mm(mix_buf[:, 0:ATTN_W], wout_ref[0:ATTN_W, cs])
              + _mm(mix_buf[:, ATTN_W:D_MODEL], wout_ref[ATTN_W:D_MODEL, cs]))
        x1_buf[:, cs] = x1
        x1b_buf[:, cs] = x1.astype(BF)
        rs = slice(c * rows_per, (c + 1) * rows_per)
        xn_buf[rs, :] = _rmsnorm(xa_ref[0, rs, :], gn_ref[...]).astype(BF)
    pb = p_ref[0].astype(BF)

    def gate_unit(c):
        cs = slice(c * MXU_N, (c + 1) * MXU_N)
        gate = _sigmoid(_mm(x1b_buf[...], wpg_ref[:, cs]))
        xo_ref[0, :, cs] = x1_buf[:, cs] + gate * _mm(pb, wpp_ref[:, cs])

    cos = cos_ref[...]
    sin_s = sin_ref[...]

    def proj(c0, width=MXU_N):
        return _mm(xn_buf[...], win_ref[:, c0:c0 + width])

    kv = proj(C_K, 2 * KV_W)
    k = _rope(kv[:, 0:KV_W], cos, sin_s)
    kbuf[BLOCK:BLOCK + tile, :] = k.astype(BF)
    v = kv[:, KV_W:2 * KV_W]
    vbuf[BLOCK:BLOCK + tile, :] = v.astype(BF)
    kt_ref[0] = k[tile - BLOCK:, :].T
    vt_ref[0] = v[tile - BLOCK:, :].T
    for c in range(ATTN_W // MXU_N):
        cs = slice(c * MXU_N, (c + 1) * MXU_N)
        q_buf[:, cs] = (_rope(proj(C_Q + c * MXU_N), cos, sin_s) * SCALE).astype(BF)

    conv_cols = (C_CG, C_HC, C_BG, C_GC)
    n_planes = len(conv_cols)
    n_slabs = CONV_W // MXU_N
    n_gate = ATTN_W // MXU_N
    n_dense = n_gate + n_slabs * n_planes

    def dense_unit(n):
        if n < n_gate:
            sga_buf[:, n * MXU_N:(n + 1) * MXU_N] = _silu(proj(C_GA + n * MXU_N))
        else:
            half, plane = divmod(n - n_gate, n_planes)
            zc_buf[plane, :, half * MXU_N:(half + 1) * MXU_N] = proj(conv_cols[plane] + half * MXU_N)

    def conv_unit(half):
        cs = slice(half * MXU_N, (half + 1) * MXU_N)
        u = zc_buf[0, :, cs] * zc_buf[1, :, cs]
        ubuf[SUBLANES:SUBLANES + tile, cs] = u
        u1 = ubuf[SUBLANES - 1:SUBLANES - 1 + tile, cs]
        u2 = ubuf[SUBLANES - 2:SUBLANES - 2 + tile, cs]
        yc = cw_ref[0:1, cs] * u2 + cw_ref[1:2, cs] * u1 + cw_ref[2:3, cs] * u
        mix_buf[:, ATTN_W + half * MXU_N:ATTN_W + (half + 1) * MXU_N] = (
            zc_buf[2, :, cs] * yc * _silu(zc_buf[3, :, cs])).astype(BF)
        ubuf[0:SUBLANES, cs] = ubuf[tile:tile + SUBLANES, cs]
        ut_ref[0, :, cs] = u[tile - (CONV_K - 1):, :]

    row = lax.broadcasted_iota(jnp.int32, (BLOCK, BLOCK), 0)
    col = lax.broadcasted_iota(jnp.int32, (BLOCK, BLOCK), 1)
    own = col <= row
    no_prev = jnp.where(first_of_seq, NEG, 0.0).astype(F32)

    def scores(n):
        j, g = divmod(n, N_KV_HEADS)
        kcat = kbuf[j * BLOCK:(j + 2) * BLOCK, g * HEAD_DIM:(g + 1) * HEAD_DIM]
        qst = jnp.concatenate(
            [q_buf[j * BLOCK:(j + 1) * BLOCK, (GROUP * g + h) * HEAD_DIM:(GROUP * g + h + 1) * HEAD_DIM]
             for h in range(GROUP)], axis=0)
        return lax.dot_general(qst, kcat, _NT, preferred_element_type=F32)

    def softmax(n, sc):
        j, g = divmod(n, N_KV_HEADS)
        probs, invs = [], []
        for h in range(GROUP):
            sh = sc[h * BLOCK:(h + 1) * BLOCK]
            s_prev = sh[:, :BLOCK]
            if j == 0:
                s_prev = s_prev + no_prev
            sm = jnp.where(own, sh[:, BLOCK:], s_prev)
            sink = sink_ref[GROUP * g + h]
            m = jnp.maximum(jnp.max(sm, axis=-1, keepdims=True), sink)
            e = jnp.exp(sm - m)
            den = jnp.sum(e, axis=-1, keepdims=True) + jnp.exp(sink - m)
            invs.append(1.0 / den)
            eb = e.astype(BF)
            zero = jnp.zeros_like(eb)
            probs.append(jnp.concatenate([jnp.where(own, zero, eb), jnp.where(own, eb, zero)], axis=1))
        return jnp.concatenate(probs, axis=0), invs

    heads = {}

    def values(n, probs, invs):
        j, g = divmod(n, N_KV_HEADS)
        vcat = vbuf[j * BLOCK:(j + 2) * BLOCK, g * HEAD_DIM:(g + 1) * HEAD_DIM]
        o = _mm(probs, vcat)
        heads.setdefault(j, []).extend(o[h * BLOCK:(h + 1) * BLOCK] * invs[h] for h in range(GROUP))
        if g == N_KV_HEADS - 1:
            rows = slice(j * BLOCK, (j + 1) * BLOCK)
            oa = jnp.concatenate(heads.pop(j), axis=1)
            mix_buf[rows, 0:ATTN_W] = (oa * sga_buf[rows, :]).astype(BF)

    n_units = (tile // BLOCK) * N_KV_HEADS
    conv_at = {n_gate + (h + 1) * n_planes: h for h in range(n_slabs)}
    n_steps = max(n_units + 2, max(conv_at) + 1)
    gate_at = {2 * c + 1: c for c in range(n_out)}
    sc_q, sm_q = {}, {}
    for t in range(n_steps):
        if t < n_units:
            sc_q[t] = scores(t)
        if t < n_dense:
            dense_unit(t)
        if 0 <= t - 2 < n_units:
            values(t - 2, *sm_q.pop(t - 2))
        if t in gate_at:
            gate_unit(gate_at[t])
        if 0 <= t - 1 < n_units:
            sm_q[t - 1] = softmax(t - 1, sc_q.pop(t - 1))
        if t in conv_at:
            conv_unit(conv_at[t])

    kbuf[0:BLOCK, :] = kbuf[tile:tile + BLOCK, :]
    vbuf[0:BLOCK, :] = vbuf[tile:tile + BLOCK, :]

    if final:
        xo_ref[0] = _rmsnorm(xo_ref[0], gfin_ref[...])


def _prompt_layer(i, x, p_all, cos, sin_s, sinks, g_norm, w_in, conv_w, w_out, w_pg, w_pp, g_final, final):
    batch, seq, _ = x.shape
    tile = PROMPT_TILE
    n_seq = seq // tile
    n_tiles = batch * n_seq
    in_tile = lambda t: jnp.minimum(t, n_tiles - 1)
    out_tile = lambda t: jnp.maximum(t - 1, 0)
    const = lambda t: (0, 0)
    layer = lambda t: (i, 0, 0)
    resident = dict(pipeline_mode=pl.Buffered(1))
    return pl.pallas_call(
        functools.partial(_prompt_kernel, tile=tile, n_seq=n_seq, final=final),
        name=f"prompt_layer{i}",
        grid=(n_tiles + 1,),
        in_specs=[
            pl.BlockSpec(memory_space=pltpu.SMEM),
            pl.BlockSpec((1, tile, D_MODEL), lambda t: (in_tile(t) // n_seq, in_tile(t) % n_seq, 0)),
            pl.BlockSpec((1, tile, D_MODEL), lambda t: (out_tile(t) // n_seq, out_tile(t) % n_seq, 0)),
            pl.BlockSpec((None, 1, tile, PLE_DIM),
                         lambda t: (i, out_tile(t) // n_seq, out_tile(t) % n_seq, 0)),
            pl.BlockSpec((tile, LANES), lambda t: (in_tile(t) % n_seq, 0)),
            pl.BlockSpec((tile, LANES), lambda t: (in_tile(t) % n_seq, 0)),
            pl.BlockSpec((None, 1, D_MODEL), layer, **resident),
            pl.BlockSpec((None, D_MODEL, IN_W), layer, **resident),
            pl.BlockSpec((None, CONV_K, CONV_W), layer, **resident),
            pl.BlockSpec((None, D_MODEL, D_MODEL), layer, **resident),
            pl.BlockSpec((None, D_MODEL, D_MODEL), layer, **resident),
            pl.BlockSpec((None, PLE_DIM, D_MODEL), layer, **resident),
            pl.BlockSpec((1, D_MODEL), const, **resident),
        ],
        out_specs=[
            pl.BlockSpec((1, tile, D_MODEL), lambda t: (out_tile(t) // n_seq, out_tile(t) % n_seq, 0)),
            pl.BlockSpec((1, BLOCK, KV_W), lambda t: (in_tile(t) // n_seq, 0, 0)),
            pl.BlockSpec((1, BLOCK, KV_W), lambda t: (in_tile(t) // n_seq, 0, 0)),
            pl.BlockSpec((1, CONV_K - 1, CONV_W), lambda t: (in_tile(t) // n_seq, 0, 0)),
        ],
        out_shape=[
            jax.ShapeDtypeStruct((batch, seq, D_MODEL), F32),
            jax.ShapeDtypeStruct((batch, BLOCK, KV_W), F32),
            jax.ShapeDtypeStruct((batch, BLOCK, KV_W), F32),
            jax.ShapeDtypeStruct((batch, CONV_K - 1, CONV_W), F32),
        ],
        scratch_shapes=[
            pltpu.VMEM((tile + BLOCK, KV_W), BF),
            pltpu.VMEM((tile + BLOCK, KV_W), BF),
            pltpu.VMEM((tile + SUBLANES, CONV_W), F32),
            pltpu.VMEM((tile, D_MODEL), BF),
            pltpu.VMEM((tile, D_MODEL), BF),
            pltpu.VMEM((4, tile, CONV_W), F32),
            pltpu.VMEM((tile, ATTN_W), BF),
            pltpu.VMEM((tile, ATTN_W), F32),
            pltpu.VMEM((tile, D_MODEL), F32),
            pltpu.VMEM((tile, D_MODEL), BF),
        ],
        compiler_params=pltpu.CompilerParams(dimension_semantics=("arbitrary",), vmem_limit_bytes=VMEM_LIMIT),
    )(sinks[i], x, x, p_all, cos, sin_s, g_norm, w_in, conv_w, w_out, w_pg, w_pp, g_final)


def _rope_t(x, cos_t, sin_t):
    parts = [x[r:r + ROPE_HALF] for r in range(0, x.shape[0], ROPE_HALF)]
    swapped = jnp.concatenate([parts[n ^ 1] for n in range(len(parts))], axis=0)
    return x * cos_t + swapped * sin_t


def _sample_kernel(sink_ref, x_ref, p_ref, cos_ref, sin_ref, cost_ref, sint_ref, st0_ref, st1_ref, gn_ref, win_ref,
                   cw_ref, wout_ref, wpg_ref, wpp_ref, gfin_ref, ck_ref, cv_ref,
                   xo_ref, nc0_ref, nc1_ref, nk_ref, nv_ref,
                   xs_buf, q_s, kt_s, vt_s, sga_s, mix_s, ubuf, p1_buf, p2_buf, *, rows, t_dec, chunk, nb):
    layer = pl.program_id(0)
    step = pl.program_id(1)
    last_layer = pl.num_programs(0) - 1
    last_step = pl.num_programs(1) - 1
    dec_batch = rows // t_dec
    gr = nb * t_dec
    cols = chunk * t_dec

    @pl.when((layer == 0) & (step == 0))
    def _():
        xs_buf[...] = x_ref[...]

    @pl.when(step == 0)
    def _():
        xn = _rmsnorm(xs_buf[...], gn_ref[...]).astype(BF)
        q_s[...] = (_rope(_mm(xn, win_ref[:, C_Q:C_K]), cos_ref[...], sin_ref[...]) * SCALE).astype(BF)
        kvt = lax.dot_general(win_ref[:, C_K:C_GA], xn, (((0,), (1,)), ((), ())),
                              preferred_element_type=F32)
        kt_s[...] = _rope_t(kvt[0:KV_W], cost_ref[...], sint_ref[...])
        vt_s[...] = kvt[KV_W:2 * KV_W]
        sga_s[...] = _silu(_mm(xn, win_ref[:, C_GA:C_BG]))
        u = _mm(xn, win_ref[:, C_CG:C_HC]) * _mm(xn, win_ref[:, C_HC:C_GC])
        per_batch = lambda first: pl.ds(first, dec_batch, stride=t_dec)
        for c in range(CONV_W // LANES):
            cs = slice(c * LANES, (c + 1) * LANES)
            ubuf[c, 0:SUBLANES, :] = jnp.zeros((SUBLANES, LANES), F32)
            ubuf[c, SUBLANES:SUBLANES + rows, :] = u[:, cs]
            p1_buf[c] = ubuf[c, SUBLANES - 1:SUBLANES - 1 + rows, :]
            p2_buf[c] = ubuf[c, SUBLANES - 2:SUBLANES - 2 + rows, :]
            p1_buf[c, per_batch(0), :] = st1_ref[:, cs]
            p2_buf[c, per_batch(0), :] = st0_ref[:, cs]
            p2_buf[c, per_batch(1), :] = st1_ref[:, cs]
            nc0_ref[:, cs] = ubuf[c, per_batch(SUBLANES + t_dec - 2), :]
            nc1_ref[:, cs] = ubuf[c, per_batch(SUBLANES + t_dec - 1), :]
        u1 = jnp.concatenate([p1_buf[c] for c in range(CONV_W // LANES)], axis=1)
        u2 = jnp.concatenate([p2_buf[c] for c in range(CONV_W // LANES)], axis=1)
        bg = _mm(xn, win_ref[:, C_BG:C_CG])
        gc = _mm(xn, win_ref[:, C_GC:IN_W])
        mix_s[:, ATTN_W:D_MODEL] = _conv_gate(u, u1, u2, cw_ref, bg, gc).astype(BF)

    c0 = pl.multiple_of(step * cols, cols)
    kt_new = kt_s[:, pl.ds(c0, cols)]
    vt_new = vt_s[:, pl.ds(c0, cols)]

    qrow = lax.broadcasted_iota(jnp.int32, (gr, nb * WINDOW), 0)
    ccol = lax.broadcasted_iota(jnp.int32, (gr, nb * WINDOW), 1)
    valid_c = ((ccol // WINDOW) == (qrow // t_dec)) & ((ccol % WINDOW) > (qrow % t_dec))
    qrow_n = lax.broadcasted_iota(jnp.int32, (gr, gr), 0)
    ncol = lax.broadcasted_iota(jnp.int32, (gr, gr), 1)
    valid_n = ((ncol // t_dec) == (qrow_n // t_dec)) & ((ncol % t_dec) <= (qrow_n % t_dec))
    lane = lax.broadcasted_iota(jnp.int32, (KV_W, WINDOW), 1)
    kept = lane < WINDOW - t_dec

    def update_window(b):
        place = (WINDOW - t_dec - t_dec * b) % cols
        k_in = pltpu.roll(kt_new, place, 1) if place else kt_new
        v_in = pltpu.roll(vt_new, place, 1) if place else vt_new
        nk_ref[b] = jnp.where(kept, pltpu.roll(ck_ref[b], WINDOW - t_dec, 1), k_in)
        nv_ref[b] = jnp.where(kept, pltpu.roll(cv_ref[b], WINDOW - t_dec, 1), v_in)

    def rows_of(sg):
        return pl.ds(pl.multiple_of(step * cols + sg * gr, gr), gr)

    def scores(n):
        sg, g = divmod(n, N_KV_HEADS)
        hd = slice(g * HEAD_DIM, (g + 1) * HEAD_DIM)
        qg = q_s[rows_of(sg), :]
        ktc = jnp.concatenate([ck_ref[sg * nb + b, hd, :] for b in range(nb)], axis=1).astype(BF)
        ktn = kt_new[hd, sg * gr:(sg + 1) * gr].astype(BF)
        qst = jnp.concatenate(
            [qg[:, (GROUP * g + h) * HEAD_DIM:(GROUP * g + h + 1) * HEAD_DIM] for h in range(GROUP)], axis=0)
        return _mm(qst, ktc), _mm(qst, ktn)

    def softmax(n, sc, sn):
        g = n % N_KV_HEADS
        ecs, ens, invs = [], [], []
        for h in range(GROUP):
            scm = jnp.where(valid_c, sc[h * gr:(h + 1) * gr], NEG)
            snm = jnp.where(valid_n, sn[h * gr:(h + 1) * gr], NEG)
            sink = sink_ref[layer, GROUP * g + h]
            m = jnp.maximum(jnp.maximum(jnp.max(scm, axis=-1, keepdims=True),
                                        jnp.max(snm, axis=-1, keepdims=True)), sink)
            ec = jnp.exp(scm - m)
            en = jnp.exp(snm - m)
            den = (jnp.sum(ec, axis=-1, keepdims=True) + jnp.sum(en, axis=-1, keepdims=True)
                   + jnp.exp(sink - m))
            invs.append(1.0 / den)
            ecs.append(ec.astype(BF))
            ens.append(en.astype(BF))
        return jnp.concatenate(ecs, axis=0), jnp.concatenate(ens, axis=0), invs

    heads = {}

    def values(n, ec, en, invs):
        sg, g = divmod(n, N_KV_HEADS)
        hd = slice(g * HEAD_DIM, (g + 1) * HEAD_DIM)
        vtc = jnp.concatenate([cv_ref[sg * nb + b, hd, :] for b in range(nb)], axis=1).astype(BF)
        vtn = vt_new[hd, sg * gr:(sg + 1) * gr].astype(BF)
        o = (lax.dot_general(ec, vtc, _NT, preferred_element_type=F32)
             + lax.dot_general(en, vtn, _NT, preferred_element_type=F32))
        heads.setdefault(sg, []).extend(o[h * gr:(h + 1) * gr] * invs[h] for h in range(GROUP))
        if g == N_KV_HEADS - 1:
            oa = jnp.concatenate(heads.pop(sg), axis=1)
            mix_s[rows_of(sg), 0:ATTN_W] = (oa * sga_s[rows_of(sg), :]).astype(BF)
            for b in range(sg * nb, (sg + 1) * nb):
                update_window(b)

    for n in range((chunk // nb) * N_KV_HEADS):
        values(n, *softmax(n, *scores(n)))

    @pl.when(step == last_step)
    def _():
        x1 = xs_buf[...] + _mm(mix_s[...], wout_ref[...])
        gate = _sigmoid(_mm(x1.astype(BF), wpg_ref[...]))
        x2 = x1 + gate * _mm(p_ref[...].astype(BF), wpp_ref[...])
        xs_buf[...] = x2

        @pl.when(layer == last_layer)
        def _():
            xo_ref[...] = _rmsnorm(x2, gfin_ref[...])


def _sample_trunk(x, p_all, cos, sin_s, cos_t, sin_t, st0, st1, sinks, g_norm, w_in, conv_w, w_out, w_pg, w_pp,
                  g_final, cache_kt, cache_vt, t_dec):
    rows = x.shape[0]
    dec_batch = rows // t_dec
    chunk = LANES // t_dec
    nb = SAMPLE_GROUP
    const = lambda l, s: (0, 0)
    layer = lambda l, s: (l, 0, 0)
    once = dict(pipeline_mode=pl.Buffered(1))
    full = lambda shape: pl.BlockSpec(shape, const, **once)
    per_layer = lambda *shape: pl.BlockSpec((None,) + shape, layer)
    cache = pl.BlockSpec((None, chunk, KV_W, WINDOW), lambda l, s: (l, s, 0, 0))
    return pl.pallas_call(
        functools.partial(_sample_kernel, rows=rows, t_dec=t_dec, chunk=chunk, nb=nb),
        name="sample_trunk",
        grid=(DEPTH, dec_batch // chunk),
        in_specs=[
            pl.BlockSpec(memory_space=pltpu.SMEM),
            full((rows, D_MODEL)),
            per_layer(rows, PLE_DIM),
            full((rows, LANES)),
            full((rows, LANES)),
            full((KV_W, rows)),
            full((KV_W, rows)),
            per_layer(dec_batch, CONV_W),
            per_layer(dec_batch, CONV_W),
            per_layer(1, D_MODEL),
            per_layer(D_MODEL, IN_W),
            per_layer(CONV_K, CONV_W),
            per_layer(D_MODEL, D_MODEL),
            per_layer(D_MODEL, D_MODEL),
            per_layer(PLE_DIM, D_MODEL),
            full((1, D_MODEL)),
            cache,
            cache,
        ],
        out_specs=[
            pl.BlockSpec((rows, D_MODEL), const),
            pl.BlockSpec((None, dec_batch, CONV_W), layer),
            pl.BlockSpec((None, dec_batch, CONV_W), layer),
            cache,
            cache,
        ],
        out_shape=[
            jax.ShapeDtypeStruct((rows, D_MODEL), F32),
            jax.ShapeDtypeStruct((DEPTH, dec_batch, CONV_W), F32),
            jax.ShapeDtypeStruct((DEPTH, dec_batch, CONV_W), F32),
            jax.ShapeDtypeStruct((DEPTH, dec_batch, KV_W, WINDOW), F32),
            jax.ShapeDtypeStruct((DEPTH, dec_batch, KV_W, WINDOW), F32),
        ],
        scratch_shapes=[
            pltpu.VMEM((rows, D_MODEL), F32),
            pltpu.VMEM((rows, ATTN_W), BF),
            pltpu.VMEM((KV_W, rows), F32),
            pltpu.VMEM((KV_W, rows), F32),
            pltpu.VMEM((rows, ATTN_W), F32),
            pltpu.VMEM((rows, D_MODEL), BF),
            pltpu.VMEM((CONV_W // LANES, rows + SUBLANES, LANES), F32),
            pltpu.VMEM((CONV_W // LANES, rows, LANES), F32),
            pltpu.VMEM((CONV_W // LANES, rows, LANES), F32),
        ],
        compiler_params=pltpu.CompilerParams(
            dimension_semantics=("arbitrary", "arbitrary"), vmem_limit_bytes=TRUNK_VMEM_LIMIT),
    )(sinks, x, p_all, cos, sin_s, cos_t, sin_t, st0, st1, g_norm, w_in, conv_w, w_out, w_pg, w_pp, g_final,
      cache_kt, cache_vt)


def _rope_angles(pos):
    inv = ROPE_THETA ** (-jnp.arange(0, HEAD_DIM, 2, dtype=F32) / HEAD_DIM)
    return pos[:, None] * inv[None, :]


def _rope_tables(pos):
    ang = _rope_angles(pos)
    cos = jnp.tile(jnp.cos(ang), (1, 2 * LANES // HEAD_DIM))
    sin = jnp.sin(ang)
    sin_signed = jnp.tile(jnp.concatenate([-sin, sin], axis=-1), (1, LANES // HEAD_DIM))
    return cos, sin_signed


def _rope_tables_t(pos):
    ang = _rope_angles(pos).T
    cos = jnp.tile(jnp.cos(ang), (2 * KV_W // HEAD_DIM, 1))
    sin = jnp.sin(ang)
    sin_signed = jnp.tile(jnp.concatenate([-sin, sin], axis=0), (KV_W // HEAD_DIM, 1))
    return cos, sin_signed


def kernel(x_prompt, x_sample, cache_k, cache_v, state_conv, p_prompt, p_sample, g_norm, w_in, sinks, conv_w,
           w_out, w_pg, w_pp, g_final):
    batch, seq, _ = x_prompt.shape
    dec_batch, t_dec, _ = x_sample.shape
    rows = dec_batch * t_dec

    cos_p, sin_p = _rope_tables(jnp.arange(seq, dtype=F32))
    pos_s = jnp.tile(PAST_LEN + jnp.arange(t_dec, dtype=F32), dec_batch)
    cos_s, sin_s = _rope_tables(pos_s)
    cos_st, sin_st = _rope_tables_t(pos_s)

    w_in_b, w_out_b, w_pg_b, w_pp_b = (w.astype(BF) for w in (w_in, w_out, w_pg, w_pp))
    g_norm3 = g_norm.reshape(DEPTH, 1, D_MODEL)
    g_fin2 = g_final.reshape(1, D_MODEL)
    to_t = lambda c: jnp.transpose(c, (0, 1, 3, 4, 2)).reshape(DEPTH, -1, KV_W, WINDOW)
    from_t = lambda c: jnp.transpose(c.reshape(DEPTH, -1, N_KV_HEADS, HEAD_DIM, WINDOW), (0, 1, 4, 2, 3))

    xp = x_prompt
    nkp, nvp, ncp = [], [], []
    for i in range(DEPTH):
        xp, kt, vt, ut = _prompt_layer(i, xp, p_prompt, cos_p, sin_p, sinks, g_norm3, w_in_b, conv_w, w_out_b,
                                       w_pg_b, w_pp_b, g_fin2, i == DEPTH - 1)
        nkp.append(kt)
        nvp.append(vt)
        ncp.append(ut)

    xs, nc0, nc1, nks, nvs = _sample_trunk(
        x_sample.reshape(rows, D_MODEL), p_sample.reshape(DEPTH, rows, PLE_DIM), cos_s, sin_s, cos_st, sin_st,
        state_conv[:, :, 0], state_conv[:, :, 1], sinks, g_norm3, w_in_b, conv_w, w_out_b, w_pg_b, w_pp_b,
        g_fin2, to_t(cache_k), to_t(cache_v), t_dec)

    return (xp, xs.reshape(dec_batch, t_dec, D_MODEL), from_t(jnp.stack(nkp)), from_t(jnp.stack(nvp)),
            jnp.stack(ncp), from_t(nks), from_t(nvs), jnp.stack([nc0, nc1], axis=2))
```

```python
import functools

import jax
import jax.numpy as jnp
from jax import lax
from jax.experimental import pallas as pl
from jax.experimental.pallas import tpu as pltpu

D_MODEL = 1024
DEPTH = 4
N_HEADS = 8
HEAD_DIM = 64
N_KV_HEADS = 2
GROUP = N_HEADS // N_KV_HEADS
ATTN_W = N_HEADS * HEAD_DIM
KV_W = N_KV_HEADS * HEAD_DIM
CONV_W = D_MODEL - ATTN_W
CONV_K = 3
WINDOW = 128
BLOCK = 128
PLE_DIM = 256
PAST_LEN = 8192
ROPE_THETA = 10000.0
EPS = 1e-6
SCALE = HEAD_DIM ** -0.5
NEG = -1e30
IN_W = 2 * ATTN_W + 2 * KV_W + 4 * CONV_W

C_Q = 0
C_K = ATTN_W
C_V = C_K + KV_W
C_GA = C_V + KV_W
C_BG = C_GA + ATTN_W
C_CG = C_BG + CONV_W
C_HC = C_CG + CONV_W
C_GC = C_HC + CONV_W

LANES = 128
SUBLANES = 8
ROPE_HALF = HEAD_DIM // 2

MXU_N = 256
PROMPT_TILE = 512
SAMPLE_GROUP = 8
VMEM_LIMIT = 56 * 1024 * 1024
TRUNK_VMEM_LIMIT = 60 * 1024 * 1024

BF = jnp.bfloat16
F32 = jnp.float32
_NT = (((1,), (1,)), ((), ()))


def _mm(a, w):
    return jnp.dot(a, w, preferred_element_type=F32)


def _sigmoid(x):
    return 1.0 / (1.0 + jnp.exp(-x))


def _silu(x):
    return x * _sigmoid(x)


def _rmsnorm(x, g):
    r = lax.rsqrt(jnp.mean(x * x, axis=-1, keepdims=True) + EPS)
    return (x * r) * g


def _rope(x, cos, sin_signed):
    rows = x.shape[0]
    lane = lax.broadcasted_iota(jnp.int32, (rows, LANES), 1)
    first_half = (lane % HEAD_DIM) < ROPE_HALF
    outs = []
    for c in range(x.shape[1] // LANES):
        xc = x[:, c * LANES:(c + 1) * LANES]
        rot = jnp.where(first_half, pltpu.roll(xc, LANES - ROPE_HALF, 1), pltpu.roll(xc, ROPE_HALF, 1))
        outs.append(xc * cos + rot * sin_signed)
    return outs[0] if len(outs) == 1 else jnp.concatenate(outs, axis=1)


def _conv_gate(u, u1, u2, cw_ref, bg, gc):
    yc = cw_ref[0:1, :] * u2 + cw_ref[1:2, :] * u1 + cw_ref[2:3, :] * u
    return bg * yc * _silu(gc)


def _prompt_kernel(*refs, **static):
    i = pl.program_id(0)
    n_tiles = pl.num_programs(0) - 1

    @pl.when(i < n_tiles)
    def _():
        _prompt_step(*refs, do_in=True, **static)

    @pl.when(i == n_tiles)
    def _():
        _prompt_step(*refs, do_in=False, **static)


def _prompt_step(sink_ref, xa_ref, xb_ref, p_ref, cos_ref, sin_ref, gn_ref, win_ref, cw_ref, wout_ref, wpg_ref,
                 wpp_ref, gfin_ref, xo_ref, kt_ref, vt_ref, ut_ref, kbuf, vbuf, ubuf, mix_buf, xn_buf, zc_buf,
                 q_buf, sga_buf, x1_buf, x1b_buf, *, tile, n_seq, final, do_in):
    i = pl.program_id(0)
    first_of_seq = i % n_seq == 0

    if do_in:
        @pl.when(i == 0)
        def _():
            mix_buf[...] = jnp.zeros(mix_buf.shape, BF)

        @pl.when(first_of_seq)
        def _():
            kbuf[0:BLOCK, :] = jnp.zeros((BLOCK, KV_W), BF)
            vbuf[0:BLOCK, :] = jnp.zeros((BLOCK, KV_W), BF)
            ubuf[0:SUBLANES, :] = jnp.zeros((SUBLANES, CONV_W), F32)

    n_out = D_MODEL // MXU_N
    rows_per = tile // n_out
    for c in range(n_out):
        cs = slice(c * MXU_N, (c + 1) * MXU_N)
        x1 = (xb_ref[0, :, cs] + _mm(mix_buf[:, 0:ATTN_W], wout_ref[0:ATTN_W, cs])
              + _mm(mix_buf[:, ATTN_W:D_MODEL], wout_ref[ATTN_W:D_MODEL, cs]))
        x1_buf[:, cs] = x1
        x1b_buf[:, cs] = x1.astype(BF)
        if do_in:
            rs = slice(c * rows_per, (c + 1) * rows_per)
            xn_buf[rs, :] = _rmsnorm(xa_ref[0, rs, :], gn_ref[...]).astype(BF)
    pb = p_ref[0].astype(BF)

    def gate_unit(c):
        cs = slice(c * MXU_N, (c + 1) * MXU_N)
        gate = _sigmoid(_mm(x1b_buf[...], wpg_ref[:, cs]))
        xo_ref[0, :, cs] = x1_buf[:, cs] + gate * _mm(pb, wpp_ref[:, cs])

    if not do_in:
        for c in range(n_out):
            gate_unit(c)
        if final:
            xo_ref[0] = _rmsnorm(xo_ref[0], gfin_ref[...])
        return

    cos = cos_ref[...]
    sin_s = sin_ref[...]

    def proj(c0, width=MXU_N):
        return _mm(xn_buf[...], win_ref[:, c0:c0 + width])

    kv = proj(C_K, 2 * KV_W)
    k = _rope(kv[:, 0:KV_W], cos, sin_s)
    kbuf[BLOCK:BLOCK + tile, :] = k.astype(BF)
    v = kv[:, KV_W:2 * KV_W]
    vbuf[BLOCK:BLOCK + tile, :] = v.astype(BF)
    kt_ref[0] = k[tile - BLOCK:, :].T
    vt_ref[0] = v[tile - BLOCK:, :].T
    for c in range(ATTN_W // MXU_N):
        cs = slice(c * MXU_N, (c + 1) * MXU_N)
        q_buf[:, cs] = (_rope(proj(C_Q + c * MXU_N), cos, sin_s) * SCALE).astype(BF)

    conv_cols = (C_CG, C_HC, C_BG, C_GC)
    n_planes = len(conv_cols)
    n_slabs = CONV_W // MXU_N
    n_gate = ATTN_W // MXU_N
    n_dense = n_gate + n_slabs * n_planes

    def dense_unit(n):
        if n < n_gate:
            sga_buf[:, n * MXU_N:(n + 1) * MXU_N] = _silu(proj(C_GA + n * MXU_N))
        else:
            half, plane = divmod(n - n_gate, n_planes)
            zc_buf[plane, :, half * MXU_N:(half + 1) * MXU_N] = proj(conv_cols[plane] + half * MXU_N)

    def conv_unit(half):
        cs = slice(half * MXU_N, (half + 1) * MXU_N)
        u = zc_buf[0, :, cs] * zc_buf[1, :, cs]
        ubuf[SUBLANES:SUBLANES + tile, cs] = u
        u1 = ubuf[SUBLANES - 1:SUBLANES - 1 + tile, cs]
        u2 = ubuf[SUBLANES - 2:SUBLANES - 2 + tile, cs]
        yc = cw_ref[0:1, cs] * u2 + cw_ref[1:2, cs] * u1 + cw_ref[2:3, cs] * u
        mix_buf[:, ATTN_W + half * MXU_N:ATTN_W + (half + 1) * MXU_N] = (
            zc_buf[2, :, cs] * yc * _silu(zc_buf[3, :, cs])).astype(BF)
        ubuf[0:SUBLANES, cs] = ubuf[tile:tile + SUBLANES, cs]
        ut_ref[0, :, cs] = u[tile - (CONV_K - 1):, :]

    row = lax.broadcasted_iota(jnp.int32, (BLOCK, BLOCK), 0)
    col = lax.broadcasted_iota(jnp.int32, (BLOCK, BLOCK), 1)
    own = col <= row
    no_prev = jnp.where(first_of_seq, NEG, 0.0).astype(F32)

    def scores(n):
        j, g = divmod(n, N_KV_HEADS)
        kcat = kbuf[j * BLOCK:(j + 2) * BLOCK, g * HEAD_DIM:(g + 1) * HEAD_DIM]
        qst = jnp.concatenate(
            [q_buf[j * BLOCK:(j + 1) * BLOCK, (GROUP * g + h) * HEAD_DIM:(GROUP * g + h + 1) * HEAD_DIM]
             for h in range(GROUP)], axis=0)
        return lax.dot_general(qst, kcat, _NT, preferred_element_type=F32)

    def softmax(n, sc):
        j, g = divmod(n, N_KV_HEADS)
        probs, invs = [], []
        for h in range(GROUP):
            sh = sc[h * BLOCK:(h + 1) * BLOCK]
            s_prev = sh[:, :BLOCK]
            if j == 0:
                s_prev = s_prev + no_prev
            sm = jnp.where(own, sh[:, BLOCK:], s_prev)
            sink = sink_ref[GROUP * g + h]
            m = jnp.maximum(jnp.max(sm, axis=-1, keepdims=True), sink)
            e = jnp.exp(sm - m)
            den = jnp.sum(e, axis=-1, keepdims=True) + jnp.exp(sink - m)
            invs.append(1.0 / den)
            eb = e.astype(BF)
            zero = jnp.zeros_like(eb)
            probs.append(jnp.concatenate([jnp.where(own, zero, eb), jnp.where(own, eb, zero)], axis=1))
        return jnp.concatenate(probs, axis=0), invs

    heads = {}

    def values(n, probs, invs):
        j, g = divmod(n, N_KV_HEADS)
        vcat = vbuf[j * BLOCK:(j + 2) * BLOCK, g * HEAD_DIM:(g + 1) * HEAD_DIM]
        o = _mm(probs, vcat)
        heads.setdefault(j, []).extend(o[h * BLOCK:(h + 1) * BLOCK] * invs[h] for h in range(GROUP))
        if g == N_KV_HEADS - 1:
            rows = slice(j * BLOCK, (j + 1) * BLOCK)
            oa = jnp.concatenate(heads.pop(j), axis=1)
            mix_buf[rows, 0:ATTN_W] = (oa * sga_buf[rows, :]).astype(BF)

    n_units = (tile // BLOCK) * N_KV_HEADS
    conv_at = {n_gate + (h + 1) * n_planes: h for h in range(n_slabs)}
    n_steps = max(n_units + 2, max(conv_at) + 1)
    gate_at = {2 * c + 1: c for c in range(n_out)}
    sc_q, sm_q = {}, {}
    for t in range(n_steps):
        if t < n_units:
            sc_q[t] = scores(t)
        if t < n_dense:
            dense_unit(t)
        if 0 <= t - 2 < n_units:
            values(t - 2, *sm_q.pop(t - 2))
        if t in gate_at:
            gate_unit(gate_at[t])
        if 0 <= t - 1 < n_units:
            sm_q[t - 1] = softmax(t - 1, sc_q.pop(t - 1))
        if t in conv_at:
            conv_unit(conv_at[t])

    kbuf[0:BLOCK, :] = kbuf[tile:tile + BLOCK, :]
    vbuf[0:BLOCK, :] = vbuf[tile:tile + BLOCK, :]

    if final:
        xo_ref[0] = _rmsnorm(xo_ref[0], gfin_ref[...])


def _prompt_layer(i, x, p_all, cos, sin_s, sinks, g_norm, w_in, conv_w, w_out, w_pg, w_pp, g_final, final):
    batch, seq, _ = x.shape
    tile = PROMPT_TILE
    n_seq = seq // tile
    n_tiles = batch * n_seq
    in_tile = lambda t: jnp.minimum(t, n_tiles - 1)
    out_tile = lambda t: jnp.maximum(t - 1, 0)
    const = lambda t: (0, 0)
    layer = lambda t: (i, 0, 0)
    resident = dict(pipeline_mode=pl.Buffered(1))
    return pl.pallas_call(
        functools.partial(_prompt_kernel, tile=tile, n_seq=n_seq, final=final),
        name=f"prompt_layer{i}",
        grid=(n_tiles + 1,),
        in_specs=[
            pl.BlockSpec(memory_space=pltpu.SMEM),
            pl.BlockSpec((1, tile, D_MODEL), lambda t: (in_tile(t) // n_seq, in_tile(t) % n_seq, 0)),
            pl.BlockSpec((1, tile, D_MODEL), lambda t: (out_tile(t) // n_seq, out_tile(t) % n_seq, 0)),
            pl.BlockSpec((None, 1, tile, PLE_DIM),
                         lambda t: (i, out_tile(t) // n_seq, out_tile(t) % n_seq, 0)),
            pl.BlockSpec((tile, LANES), lambda t: (in_tile(t) % n_seq, 0)),
            pl.BlockSpec((tile, LANES), lambda t: (in_tile(t) % n_seq, 0)),
            pl.BlockSpec((None, 1, D_MODEL), layer, **resident),
            pl.BlockSpec((None, D_MODEL, IN_W), layer, **resident),
            pl.BlockSpec((None, CONV_K, CONV_W), layer, **resident),
            pl.BlockSpec((None, D_MODEL, D_MODEL), layer, **resident),
            pl.BlockSpec((None, D_MODEL, D_MODEL), layer, **resident),
            pl.BlockSpec((None, PLE_DIM, D_MODEL), layer, **resident),
            pl.BlockSpec((1, D_MODEL), const, **resident),
        ],
        out_specs=[
            pl.BlockSpec((1, tile, D_MODEL), lambda t: (out_tile(t) // n_seq, out_tile(t) % n_seq, 0)),
            pl.BlockSpec((1, BLOCK, KV_W), lambda t: (in_tile(t) // n_seq, 0, 0)),
            pl.BlockSpec((1, BLOCK, KV_W), lambda t: (in_tile(t) // n_seq, 0, 0)),
            pl.BlockSpec((1, CONV_K - 1, CONV_W), lambda t: (in_tile(t) // n_seq, 0, 0)),
        ],
        out_shape=[
            jax.ShapeDtypeStruct((batch, seq, D_MODEL), F32),
            jax.ShapeDtypeStruct((batch, BLOCK, KV_W), F32),
            jax.ShapeDtypeStruct((batch, BLOCK, KV_W), F32),
            jax.ShapeDtypeStruct((batch, CONV_K - 1, CONV_W), F32),
        ],
        scratch_shapes=[
            pltpu.VMEM((tile + BLOCK, KV_W), BF),
            pltpu.VMEM((tile + BLOCK, KV_W), BF),
            pltpu.VMEM((tile + SUBLANES, CONV_W), F32),
            pltpu.VMEM((tile, D_MODEL), BF),
            pltpu.VMEM((tile, D_MODEL), BF),
            pltpu.VMEM((4, tile, CONV_W), F32),
            pltpu.VMEM((tile, ATTN_W), BF),
            pltpu.VMEM((tile, ATTN_W), F32),
            pltpu.VMEM((tile, D_MODEL), F32),
            pltpu.VMEM((tile, D_MODEL), BF),
        ],
        compiler_params=pltpu.CompilerParams(dimension_semantics=("arbitrary",), vmem_limit_bytes=VMEM_LIMIT),
    )(sinks[i], x, x, p_all, cos, sin_s, g_norm, w_in, conv_w, w_out, w_pg, w_pp, g_final)


def _rope_t(x, cos_t, sin_t):
    parts = [x[r:r + ROPE_HALF] for r in range(0, x.shape[0], ROPE_HALF)]
    swapped = jnp.concatenate([parts[n ^ 1] for n in range(len(parts))], axis=0)
    return x * cos_t + swapped * sin_t


def _sample_kernel(sink_ref, x_ref, p_ref, cos_ref, sin_ref, cost_ref, sint_ref, st0_ref, st1_ref, gn_ref, win_ref,
                   cw_ref, wout_ref, wpg_ref, wpp_ref, gfin_ref, ck_ref, cv_ref,
                   xo_ref, nc0_ref, nc1_ref, nk_ref, nv_ref,
                   xs_buf, q_s, kt_s, vt_s, sga_s, mix_s, ubuf, p1_buf, p2_buf, *, rows, t_dec, chunk, nb):
    layer = pl.program_id(0)
    step = pl.program_id(1)
    last_layer = pl.num_programs(0) - 1
    last_step = pl.num_programs(1) - 1
    dec_batch = rows // t_dec
    gr = nb * t_dec
    cols = chunk * t_dec

    @pl.when((layer == 0) & (step == 0))
    def _():
        xs_buf[...] = x_ref[...]

    @pl.when(step == 0)
    def _():
        xn = _rmsnorm(xs_buf[...], gn_ref[...]).astype(BF)
        q_s[...] = (_rope(_mm(xn, win_ref[:, C_Q:C_K]), cos_ref[...], sin_ref[...]) * SCALE).astype(BF)
        kvt = lax.dot_general(win_ref[:, C_K:C_GA], xn, (((0,), (1,)), ((), ())),
                              preferred_element_type=F32)
        kt_s[...] = _rope_t(kvt[0:KV_W], cost_ref[...], sint_ref[...])
        vt_s[...] = kvt[KV_W:2 * KV_W]
        sga_s[...] = _silu(_mm(xn, win_ref[:, C_GA:C_BG]))
        u = _mm(xn, win_ref[:, C_CG:C_HC]) * _mm(xn, win_ref[:, C_HC:C_GC])
        per_batch = lambda first: pl.ds(first, dec_batch, stride=t_dec)
        for c in range(CONV_W // LANES):
            cs = slice(c * LANES, (c + 1) * LANES)
            ubuf[c, 0:SUBLANES, :] = jnp.zeros((SUBLANES, LANES), F32)
            ubuf[c, SUBLANES:SUBLANES + rows, :] = u[:, cs]
            p1_buf[c] = ubuf[c, SUBLANES - 1:SUBLANES - 1 + rows, :]
            p2_buf[c] = ubuf[c, SUBLANES - 2:SUBLANES - 2 + rows, :]
            p1_buf[c, per_batch(0), :] = st1_ref[:, cs]
            p2_buf[c, per_batch(0), :] = st0_ref[:, cs]
            p2_buf[c, per_batch(1), :] = st1_ref[:, cs]
            nc0_ref[:, cs] = ubuf[c, per_batch(SUBLANES + t_dec - 2), :]
            nc1_ref[:, cs] = ubuf[c, per_batch(SUBLANES + t_dec - 1), :]
        u1 = jnp.concatenate([p1_buf[c] for c in range(CONV_W // LANES)], axis=1)
        u2 = jnp.concatenate([p2_buf[c] for c in range(CONV_W // LANES)], axis=1)
        bg = _mm(xn, win_ref[:, C_BG:C_CG])
        gc = _mm(xn, win_ref[:, C_GC:IN_W])
        mix_s[:, ATTN_W:D_MODEL] = _conv_gate(u, u1, u2, cw_ref, bg, gc).astype(BF)

    c0 = pl.multiple_of(step * cols, cols)
    kt_new = kt_s[:, pl.ds(c0, cols)]
    vt_new = vt_s[:, pl.ds(c0, cols)]

    qrow = lax.broadcasted_iota(jnp.int32, (gr, nb * WINDOW), 0)
    ccol = lax.broadcasted_iota(jnp.int32, (gr, nb * WINDOW), 1)
    valid_c = ((ccol // WINDOW) == (qrow // t_dec)) & ((ccol % WINDOW) > (qrow % t_dec))
    qrow_n = lax.broadcasted_iota(jnp.int32, (gr, gr), 0)
    ncol = lax.broadcasted_iota(jnp.int32, (gr, gr), 1)
    valid_n = ((ncol // t_dec) == (qrow_n // t_dec)) & ((ncol % t_dec) <= (qrow_n % t_dec))
    lane = lax.broadcasted_iota(jnp.int32, (KV_W, WINDOW), 1)
    kept = lane < WINDOW - t_dec

    def update_window(b):
        place = (WINDOW - t_dec - t_dec * b) % cols
        k_in = pltpu.roll(kt_new, place, 1) if place else kt_new
        v_in = pltpu.roll(vt_new, place, 1) if place else vt_new
        nk_ref[b] = jnp.where(kept, pltpu.roll(ck_ref[b], WINDOW - t_dec, 1), k_in)
        nv_ref[b] = jnp.where(kept, pltpu.roll(cv_ref[b], WINDOW - t_dec, 1), v_in)

    def rows_of(sg):
        return pl.ds(pl.multiple_of(step * cols + sg * gr, gr), gr)

    def scores(n):
        sg, g = divmod(n, N_KV_HEADS)
        hd = slice(g * HEAD_DIM, (g + 1) * HEAD_DIM)
        qg = q_s[rows_of(sg), :]
        ktc = jnp.concatenate([ck_ref[sg * nb + b, hd, :] for b in range(nb)], axis=1).astype(BF)
        ktn = kt_new[hd, sg * gr:(sg + 1) * gr].astype(BF)
        qst = jnp.concatenate(
            [qg[:, (GROUP * g + h) * HEAD_DIM:(GROUP * g + h + 1) * HEAD_DIM] for h in range(GROUP)], axis=0)
        return _mm(qst, ktc), _mm(qst, ktn)

    def softmax(n, sc, sn):
        g = n % N_KV_HEADS
        ecs, ens, invs = [], [], []
        for h in range(GROUP):
            scm = jnp.where(valid_c, sc[h * gr:(h + 1) * gr], NEG)
            snm = jnp.where(valid_n, sn[h * gr:(h + 1) * gr], NEG)
            sink = sink_ref[layer, GROUP * g + h]
            m = jnp.maximum(jnp.maximum(jnp.max(scm, axis=-1, keepdims=True),
                                        jnp.max(snm, axis=-1, keepdims=True)), sink)
            ec = jnp.exp(scm - m)
            en = jnp.exp(snm - m)
            den = (jnp.sum(ec, axis=-1, keepdims=True) + jnp.sum(en, axis=-1, keepdims=True)
                   + jnp.exp(sink - m))
            invs.append(1.0 / den)
            ecs.append(ec.astype(BF))
            ens.append(en.astype(BF))
        return jnp.concatenate(ecs, axis=0), jnp.concatenate(ens, axis=0), invs

    heads = {}

    def values(n, ec, en, invs):
        sg, g = divmod(n, N_KV_HEADS)
        hd = slice(g * HEAD_DIM, (g + 1) * HEAD_DIM)
        vtc = jnp.concatenate([cv_ref[sg * nb + b, hd, :] for b in range(nb)], axis=1).astype(BF)
        vtn = vt_new[hd, sg * gr:(sg + 1) * gr].astype(BF)
        o = (lax.dot_general(ec, vtc, _NT, preferred_element_type=F32)
             + lax.dot_general(en, vtn, _NT, preferred_element_type=F32))
        heads.setdefault(sg, []).extend(o[h * gr:(h + 1) * gr] * invs[h] for h in range(GROUP))
        if g == N_KV_HEADS - 1:
            oa = jnp.concatenate(heads.pop(sg), axis=1)
            mix_s[rows_of(sg), 0:ATTN_W] = (oa * sga_s[rows_of(sg), :]).astype(BF)
            for b in range(sg * nb, (sg + 1) * nb):
                update_window(b)

    for n in range((chunk // nb) * N_KV_HEADS):
        values(n, *softmax(n, *scores(n)))

    @pl.when(step == last_step)
    def _():
        x1 = xs_buf[...] + _mm(mix_s[...], wout_ref[...])
        gate = _sigmoid(_mm(x1.astype(BF), wpg_ref[...]))
        x2 = x1 + gate * _mm(p_ref[...].astype(BF), wpp_ref[...])
        xs_buf[...] = x2

        @pl.when(layer == last_layer)
        def _():
            xo_ref[...] = _rmsnorm(x2, gfin_ref[...])


def _sample_trunk(x, p_all, cos, sin_s, cos_t, sin_t, st0, st1, sinks, g_norm, w_in, conv_w, w_out, w_pg, w_pp,
                  g_final, cache_kt, cache_vt, t_dec):
    rows = x.shape[0]
    dec_batch = rows // t_dec
    chunk = LANES // t_dec
    nb = SAMPLE_GROUP
    const = lambda l, s: (0, 0)
    layer = lambda l, s: (l, 0, 0)
    once = dict(pipeline_mode=pl.Buffered(1))
    full = lambda shape: pl.BlockSpec(shape, const, **once)
    per_layer = lambda *shape: pl.BlockSpec((None,) + shape, layer)
    cache = pl.BlockSpec((None, chunk, KV_W, WINDOW), lambda l, s: (l, s, 0, 0))
    return pl.pallas_call(
        functools.partial(_sample_kernel, rows=rows, t_dec=t_dec, chunk=chunk, nb=nb),
        name="sample_trunk",
        grid=(DEPTH, dec_batch // chunk),
        in_specs=[
            pl.BlockSpec(memory_space=pltpu.SMEM),
            full((rows, D_MODEL)),
            per_layer(rows, PLE_DIM),
            full((rows, LANES)),
            full((rows, LANES)),
            full((KV_W, rows)),
            full((KV_W, rows)),
            per_layer(dec_batch, CONV_W),
            per_layer(dec_batch, CONV_W),
            per_layer(1, D_MODEL),
            per_layer(D_MODEL, IN_W),
            per_layer(CONV_K, CONV_W),
            per_layer(D_MODEL, D_MODEL),
            per_layer(D_MODEL, D_MODEL),
            per_layer(PLE_DIM, D_MODEL),
            full((1, D_MODEL)),
            cache,
            cache,
        ],
        out_specs=[
            pl.BlockSpec((rows, D_MODEL), const),
            pl.BlockSpec((None, dec_batch, CONV_W), layer),
            pl.BlockSpec((None, dec_batch, CONV_W), layer),
            cache,
            cache,
        ],
        out_shape=[
            jax.ShapeDtypeStruct((rows, D_MODEL), F32),
            jax.ShapeDtypeStruct((DEPTH, dec_batch, CONV_W), F32),
            jax.ShapeDtypeStruct((DEPTH, dec_batch, CONV_W), F32),
            jax.ShapeDtypeStruct((DEPTH, dec_batch, KV_W, WINDOW), F32),
            jax.ShapeDtypeStruct((DEPTH, dec_batch, KV_W, WINDOW), F32),
        ],
        scratch_shapes=[
            pltpu.VMEM((rows, D_MODEL), F32),
            pltpu.VMEM((rows, ATTN_W), BF),
            pltpu.VMEM((KV_W, rows), F32),
            pltpu.VMEM((KV_W, rows), F32),
            pltpu.VMEM((rows, ATTN_W), F32),
            pltpu.VMEM((rows, D_MODEL), BF),
            pltpu.VMEM((CONV_W // LANES, rows + SUBLANES, LANES), F32),
            pltpu.VMEM((CONV_W // LANES, rows, LANES), F32),
            pltpu.VMEM((CONV_W // LANES, rows, LANES), F32),
        ],
        compiler_params=pltpu.CompilerParams(
            dimension_semantics=("arbitrary", "arbitrary"), vmem_limit_bytes=TRUNK_VMEM_LIMIT),
    )(sinks, x, p_all, cos, sin_s, cos_t, sin_t, st0, st1, g_norm, w_in, conv_w, w_out, w_pg, w_pp, g_final,
      cache_kt, cache_vt)


def _rope_angles(pos):
    inv = ROPE_THETA ** (-jnp.arange(0, HEAD_DIM, 2, dtype=F32) / HEAD_DIM)
    return pos[:, None] * inv[None, :]


def _rope_tables(pos):
    ang = _rope_angles(pos)
    cos = jnp.tile(jnp.cos(ang), (1, 2 * LANES // HEAD_DIM))
    sin = jnp.sin(ang)
    sin_signed = jnp.tile(jnp.concatenate([-sin, sin], axis=-1), (1, LANES // HEAD_DIM))
    return cos, sin_signed


def _rope_tables_t(pos):
    ang = _rope_angles(pos).T
    cos = jnp.tile(jnp.cos(ang), (2 * KV_W // HEAD_DIM, 1))
    sin = jnp.sin(ang)
    sin_signed = jnp.tile(jnp.concatenate([-sin, sin], axis=0), (KV_W // HEAD_DIM, 1))
    return cos, sin_signed


def kernel(x_prompt, x_sample, cache_k, cache_v, state_conv, p_prompt, p_sample, g_norm, w_in, sinks, conv_w,
           w_out, w_pg, w_pp, g_final):
    batch, seq, _ = x_prompt.shape
    dec_batch, t_dec, _ = x_sample.shape
    rows = dec_batch * t_dec

    cos_p, sin_p = _rope_tables(jnp.arange(seq, dtype=F32))
    pos_s = jnp.tile(PAST_LEN + jnp.arange(t_dec, dtype=F32), dec_batch)
    cos_s, sin_s = _rope_tables(pos_s)
    cos_st, sin_st = _rope_tables_t(pos_s)

    w_in_b, w_out_b, w_pg_b, w_pp_b = (w.astype(BF) for w in (w_in, w_out, w_pg, w_pp))
    g_norm3 = g_norm.reshape(DEPTH, 1, D_MODEL)
    g_fin2 = g_final.reshape(1, D_MODEL)
    to_t = lambda c: jnp.transpose(c, (0, 1, 3, 4, 2)).reshape(DEPTH, -1, KV_W, WINDOW)
    from_t = lambda c: jnp.transpose(c.reshape(DEPTH, -1, N_KV_HEADS, HEAD_DIM, WINDOW), (0, 1, 4, 2, 3))

    xp = x_prompt
    nkp, nvp, ncp = [], [], []
    for i in range(DEPTH):
        xp, kt, vt, ut = _prompt_layer(i, xp, p_prompt, cos_p, sin_p, sinks, g_norm3, w_in_b, conv_w, w_out_b,
                                       w_pg_b, w_pp_b, g_fin2, i == DEPTH - 1)
        nkp.append(kt)
        nvp.append(vt)
        ncp.append(ut)

    xs, nc0, nc1, nks, nvs = _sample_trunk(
        x_sample.reshape(rows, D_MODEL), p_sample.reshape(DEPTH, rows, PLE_DIM), cos_s, sin_s, cos_st, sin_st,
        state_conv[:, :, 0], state_conv[:, :, 1], sinks, g_norm3, w_in_b, conv_w, w_out_b, w_pg_b, w_pp_b,
        g_fin2, to_t(cache_k), to_t(cache_v), t_dec)

    return (xp, xs.reshape(dec_batch, t_dec, D_MODEL), from_t(jnp.stack(nkp)), from_t(jnp.stack(nvp)),
            jnp.stack(ncp), from_t(nks), from_t(nvs), jnp.stack([nc0, nc1], axis=2))
```

```python
import functools

import jax
import jax.numpy as jnp
from jax import lax
from jax.experimental import pallas as pl
from jax.experimental.pallas import tpu as pltpu

D_MODEL = 1024
DEPTH = 4
N_HEADS = 8
HEAD_DIM = 64
N_KV_HEADS = 2
GROUP = N_HEADS // N_KV_HEADS
ATTN_W = N_HEADS * HEAD_DIM
KV_W = N_KV_HEADS * HEAD_DIM
CONV_W = D_MODEL - ATTN_W
CONV_K = 3
WINDOW = 128
BLOCK = 128
PLE_DIM = 256
PAST_LEN = 8192
ROPE_THETA = 10000.0
EPS = 1e-6
SCALE = HEAD_DIM ** -0.5
NEG = -1e30
IN_W = 2 * ATTN_W + 2 * KV_W + 4 * CONV_W

C_Q = 0
C_K = ATTN_W
C_V = C_K + KV_W
C_GA = C_V + KV_W
C_BG = C_GA + ATTN_W
C_CG = C_BG + CONV_W
C_HC = C_CG + CONV_W
C_GC = C_HC + CONV_W

LANES = 128
SUBLANES = 8
ROPE_HALF = HEAD_DIM // 2

MXU_N = 256
PROMPT_TILE = 1024
SAMPLE_GROUP = 8
VMEM_LIMIT = 62 * 1024 * 1024
TRUNK_VMEM_LIMIT = 60 * 1024 * 1024

BF = jnp.bfloat16
F32 = jnp.float32
_NT = (((1,), (1,)), ((), ()))


def _mm(a, w):
    return jnp.dot(a, w, preferred_element_type=F32)


def _sigmoid(x):
    return 1.0 / (1.0 + jnp.exp(-x))


def _silu(x):
    return x * _sigmoid(x)


def _rmsnorm(x, g):
    r = lax.rsqrt(jnp.mean(x * x, axis=-1, keepdims=True) + EPS)
    return (x * r) * g


def _rope(x, cos, sin_signed):
    rows = x.shape[0]
    lane = lax.broadcasted_iota(jnp.int32, (rows, LANES), 1)
    first_half = (lane % HEAD_DIM) < ROPE_HALF
    outs = []
    for c in range(x.shape[1] // LANES):
        xc = x[:, c * LANES:(c + 1) * LANES]
        rot = jnp.where(first_half, pltpu.roll(xc, LANES - ROPE_HALF, 1), pltpu.roll(xc, ROPE_HALF, 1))
        outs.append(xc * cos + rot * sin_signed)
    return outs[0] if len(outs) == 1 else jnp.concatenate(outs, axis=1)


def _conv_gate(u, u1, u2, cw_ref, bg, gc):
    yc = cw_ref[0:1, :] * u2 + cw_ref[1:2, :] * u1 + cw_ref[2:3, :] * u
    return bg * yc * _silu(gc)


def _prompt_kernel(sink_ref, xa_ref, xb_ref, p_ref, cos_ref, sin_ref, gn_ref, win_ref, cw_ref, wout_ref, wpg_ref,
                   wpp_ref, gfin_ref, xo_ref, kt_ref, vt_ref, ut_ref, kbuf, vbuf, ubuf, mix_buf, xn_buf, zc_buf,
                   q_buf, sga_buf, x1b_buf, *, tile, n_seq, final):
    i = pl.program_id(0)
    n_tiles = pl.num_programs(0) - 1
    first_of_seq = jnp.minimum(i, n_tiles - 1) % n_seq == 0

    @pl.when(i == 0)
    def _():
        mix_buf[...] = jnp.zeros(mix_buf.shape, BF)

    @pl.when(first_of_seq)
    def _():
        kbuf[0:BLOCK, :] = jnp.zeros((BLOCK, KV_W), BF)
        vbuf[0:BLOCK, :] = jnp.zeros((BLOCK, KV_W), BF)
        ubuf[0:SUBLANES, :] = jnp.zeros((SUBLANES, CONV_W), F32)

    n_out = D_MODEL // MXU_N
    rows_per = tile // n_out
    for c in range(n_out):
        cs = slice(c * MXU_N, (c + 1) * MXU_N)
        x1 = (xb_ref[0, :, cs] + _mm(mix_buf[:, 0:ATTN_W], wout_ref[0:ATTN_W, cs])
              + _mm(mix_buf[:, ATTN_W:D_MODEL], wout_ref[ATTN_W:D_MODEL, cs]))
        xo_ref[0, :, cs] = x1
        x1b_buf[:, cs] = x1.astype(BF)
        rs = slice(c * rows_per, (c + 1) * rows_per)
        xn_buf[rs, :] = _rmsnorm(xa_ref[0, rs, :], gn_ref[...]).astype(BF)
    pb = p_ref[0].astype(BF)

    def gate_unit(c):
        cs = slice(c * MXU_N, (c + 1) * MXU_N)
        gate = _sigmoid(_mm(x1b_buf[...], wpg_ref[:, cs]))
        xo_ref[0, :, cs] = xo_ref[0, :, cs] + gate * _mm(pb, wpp_ref[:, cs])

    cos = cos_ref[...]
    sin_s = sin_ref[...]

    def proj(c0, width=MXU_N):
        return _mm(xn_buf[...], win_ref[:, c0:c0 + width])

    kv = proj(C_K, 2 * KV_W)
    k = _rope(kv[:, 0:KV_W], cos, sin_s)
    kbuf[BLOCK:BLOCK + tile, :] = k.astype(BF)
    v = kv[:, KV_W:2 * KV_W]
    vbuf[BLOCK:BLOCK + tile, :] = v.astype(BF)
    kt_ref[0] = k[tile - BLOCK:, :].T
    vt_ref[0] = v[tile - BLOCK:, :].T
    for c in range(ATTN_W // MXU_N):
        cs = slice(c * MXU_N, (c + 1) * MXU_N)
        q_buf[:, cs] = (_rope(proj(C_Q + c * MXU_N), cos, sin_s) * SCALE).astype(BF)

    conv_cols = (C_CG, C_HC, C_BG, C_GC)
    n_planes = len(conv_cols)
    n_slabs = CONV_W // MXU_N
    n_gate = ATTN_W // MXU_N
    n_dense = n_gate + n_slabs * n_planes

    def dense_unit(n):
        if n < n_gate:
            sga_buf[:, n * MXU_N:(n + 1) * MXU_N] = _silu(proj(C_GA + n * MXU_N))
        else:
            half, plane = divmod(n - n_gate, n_planes)
            zc_buf[plane] = proj(conv_cols[plane] + half * MXU_N)

    def conv_unit(half):
        cs = slice(half * MXU_N, (half + 1) * MXU_N)
        u = zc_buf[0] * zc_buf[1]
        ubuf[SUBLANES:SUBLANES + tile, cs] = u
        u1 = ubuf[SUBLANES - 1:SUBLANES - 1 + tile, cs]
        u2 = ubuf[SUBLANES - 2:SUBLANES - 2 + tile, cs]
        yc = cw_ref[0:1, cs] * u2 + cw_ref[1:2, cs] * u1 + cw_ref[2:3, cs] * u
        mix_buf[:, ATTN_W + half * MXU_N:ATTN_W + (half + 1) * MXU_N] = (
            zc_buf[2] * yc * _silu(zc_buf[3])).astype(BF)
        ubuf[0:SUBLANES, cs] = ubuf[tile:tile + SUBLANES, cs]
        ut_ref[0, :, cs] = u[tile - (CONV_K - 1):, :]

    row = lax.broadcasted_iota(jnp.int32, (BLOCK, BLOCK), 0)
    col = lax.broadcasted_iota(jnp.int32, (BLOCK, BLOCK), 1)
    own = col <= row
    no_prev = jnp.where(first_of_seq, NEG, 0.0).astype(F32)

    def scores(n):
        j, g = divmod(n, N_KV_HEADS)
        kcat = kbuf[j * BLOCK:(j + 2) * BLOCK, g * HEAD_DIM:(g + 1) * HEAD_DIM]
        qst = jnp.concatenate(
            [q_buf[j * BLOCK:(j + 1) * BLOCK, (GROUP * g + h) * HEAD_DIM:(GROUP * g + h + 1) * HEAD_DIM]
             for h in range(GROUP)], axis=0)
        return lax.dot_general(qst, kcat, _NT, preferred_element_type=F32)

    def softmax(n, sc):
        j, g = divmod(n, N_KV_HEADS)
        probs, invs = [], []
        for h in range(GROUP):
            sh = sc[h * BLOCK:(h + 1) * BLOCK]
            s_prev = sh[:, :BLOCK]
            if j == 0:
                s_prev = s_prev + no_prev
            sm = jnp.where(own, sh[:, BLOCK:], s_prev)
            sink = sink_ref[GROUP * g + h]
            m = jnp.maximum(jnp.max(sm, axis=-1, keepdims=True), sink)
            e = jnp.exp(sm - m)
            den = jnp.sum(e, axis=-1, keepdims=True) + jnp.exp(sink - m)
            invs.append(1.0 / den)
            eb = e.astype(BF)
            zero = jnp.zeros_like(eb)
            probs.append(jnp.concatenate([jnp.where(own, zero, eb), jnp.where(own, eb, zero)], axis=1))
        return jnp.concatenate(probs, axis=0), invs

    heads = {}

    def values(n, probs, invs):
        j, g = divmod(n, N_KV_HEADS)
        vcat = vbuf[j * BLOCK:(j + 2) * BLOCK, g * HEAD_DIM:(g + 1) * HEAD_DIM]
        o = _mm(probs, vcat)
        heads.setdefault(j, []).extend(o[h * BLOCK:(h + 1) * BLOCK] * invs[h] for h in range(GROUP))
        if g == N_KV_HEADS - 1:
            rows = slice(j * BLOCK, (j + 1) * BLOCK)
            oa = jnp.concatenate(heads.pop(j), axis=1)
            mix_buf[rows, 0:ATTN_W] = (oa * sga_buf[rows, :]).astype(BF)

    n_units = (tile // BLOCK) * N_KV_HEADS
    conv_at = {n_gate + (h + 1) * n_planes: h for h in range(n_slabs)}
    n_steps = max(n_units + 2, max(conv_at) + 1)
    gate_at = {n_dense + 2 * c: c for c in range(n_out)}
    sc_q, sm_q = {}, {}
    for t in range(n_steps):
        if t < n_units:
            sc_q[t] = scores(t)
        if t in conv_at:
            conv_unit(conv_at[t])
        if t < n_dense:
            dense_unit(t)
        if 0 <= t - 2 < n_units:
            values(t - 2, *sm_q.pop(t - 2))
        if t in gate_at:
            gate_unit(gate_at[t])
        if 0 <= t - 1 < n_units:
            sm_q[t - 1] = softmax(t - 1, sc_q.pop(t - 1))

    kbuf[0:BLOCK, :] = kbuf[tile:tile + BLOCK, :]
    vbuf[0:BLOCK, :] = vbuf[tile:tile + BLOCK, :]

    if final:
        xo_ref[0] = _rmsnorm(xo_ref[0], gfin_ref[...])


def _prompt_layer(i, x, p_all, cos, sin_s, sinks, g_norm, w_in, conv_w, w_out, w_pg, w_pp, g_final, final):
    batch, seq, _ = x.shape
    tile = PROMPT_TILE
    n_seq = seq // tile
    n_tiles = batch * n_seq
    in_tile = lambda t: jnp.minimum(t, n_tiles - 1)
    out_tile = lambda t: jnp.maximum(t - 1, 0)
    const = lambda t: (0, 0)
    layer = lambda t: (i, 0, 0)
    resident = dict(pipeline_mode=pl.Buffered(1))
    return pl.pallas_call(
        functools.partial(_prompt_kernel, tile=tile, n_seq=n_seq, final=final),
        name=f"prompt_layer{i}",
        grid=(n_tiles + 1,),
        in_specs=[
            pl.BlockSpec(memory_space=pltpu.SMEM),
            pl.BlockSpec((1, tile, D_MODEL), lambda t: (in_tile(t) // n_seq, in_tile(t) % n_seq, 0)),
            pl.BlockSpec((1, tile, D_MODEL), lambda t: (out_tile(t) // n_seq, out_tile(t) % n_seq, 0)),
            pl.BlockSpec((None, 1, tile, PLE_DIM),
                         lambda t: (i, out_tile(t) // n_seq, out_tile(t) % n_seq, 0)),
            pl.BlockSpec((tile, LANES), lambda t: (in_tile(t) % n_seq, 0)),
            pl.BlockSpec((tile, LANES), lambda t: (in_tile(t) % n_seq, 0)),
            pl.BlockSpec((None, 1, D_MODEL), layer, **resident),
            pl.BlockSpec((None, D_MODEL, IN_W), layer, **resident),
            pl.BlockSpec((None, CONV_K, CONV_W), layer, **resident),
            pl.BlockSpec((None, D_MODEL, D_MODEL), layer, **resident),
            pl.BlockSpec((None, D_MODEL, D_MODEL), layer, **resident),
            pl.BlockSpec((None, PLE_DIM, D_MODEL), layer, **resident),
            pl.BlockSpec((1, D_MODEL), const, **resident),
        ],
        out_specs=[
            pl.BlockSpec((1, tile, D_MODEL), lambda t: (out_tile(t) // n_seq, out_tile(t) % n_seq, 0)),
            pl.BlockSpec((1, BLOCK, KV_W), lambda t: (in_tile(t) // n_seq, 0, 0)),
            pl.BlockSpec((1, BLOCK, KV_W), lambda t: (in_tile(t) // n_seq, 0, 0)),
            pl.BlockSpec((1, CONV_K - 1, CONV_W), lambda t: (in_tile(t) // n_seq, 0, 0)),
        ],
        out_shape=[
            jax.ShapeDtypeStruct((batch, seq, D_MODEL), F32),
            jax.ShapeDtypeStruct((batch, BLOCK, KV_W), F32),
            jax.ShapeDtypeStruct((batch, BLOCK, KV_W), F32),
            jax.ShapeDtypeStruct((batch, CONV_K - 1, CONV_W), F32),
        ],
        scratch_shapes=[
            pltpu.VMEM((tile + BLOCK, KV_W), BF),
            pltpu.VMEM((tile + BLOCK, KV_W), BF),
            pltpu.VMEM((tile + SUBLANES, CONV_W), F32),
            pltpu.VMEM((tile, D_MODEL), BF),
            pltpu.VMEM((tile, D_MODEL), BF),
            pltpu.VMEM((4, tile, MXU_N), F32),
            pltpu.VMEM((tile, ATTN_W), BF),
            pltpu.VMEM((tile, ATTN_W), F32),
            pltpu.VMEM((tile, D_MODEL), BF),
        ],
        compiler_params=pltpu.CompilerParams(dimension_semantics=("arbitrary",), vmem_limit_bytes=VMEM_LIMIT),
    )(sinks[i], x, x, p_all, cos, sin_s, g_norm, w_in, conv_w, w_out, w_pg, w_pp, g_final)


def _rope_t(x, cos_t, sin_t):
    parts = [x[r:r + ROPE_HALF] for r in range(0, x.shape[0], ROPE_HALF)]
    swapped = jnp.concatenate([parts[n ^ 1] for n in range(len(parts))], axis=0)
    return x * cos_t + swapped * sin_t


def _sample_kernel(sink_ref, x_ref, p_ref, cos_ref, sin_ref, cost_ref, sint_ref, st0_ref, st1_ref, gn_ref, win_ref,
                   cw_ref, wout_ref, wpg_ref, wpp_ref, gfin_ref, ck_ref, cv_ref,
                   xo_ref, nc0_ref, nc1_ref, nk_ref, nv_ref,
                   xs_buf, q_s, kt_s, vt_s, sga_s, mix_s, ubuf, p1_buf, p2_buf, *, rows, t_dec, chunk, nb):
    layer = pl.program_id(0)
    step = pl.program_id(1)
    last_layer = pl.num_programs(0) - 1
    last_step = pl.num_programs(1) - 1
    dec_batch = rows // t_dec
    gr = nb * t_dec
    cols = chunk * t_dec

    @pl.when((layer == 0) & (step == 0))
    def _():
        xs_buf[...] = x_ref[...]

    @pl.when(step == 0)
    def _():
        xn = _rmsnorm(xs_buf[...], gn_ref[...]).astype(BF)
        q_s[...] = (_rope(_mm(xn, win_ref[:, C_Q:C_K]), cos_ref[...], sin_ref[...]) * SCALE).astype(BF)
        kvt = lax.dot_general(win_ref[:, C_K:C_GA], xn, (((0,), (1,)), ((), ())),
                              preferred_element_type=F32)
        kt_s[...] = _rope_t(kvt[0:KV_W], cost_ref[...], sint_ref[...])
        vt_s[...] = kvt[KV_W:2 * KV_W]
        sga_s[...] = _silu(_mm(xn, win_ref[:, C_GA:C_BG]))
        u = _mm(xn, win_ref[:, C_CG:C_HC]) * _mm(xn, win_ref[:, C_HC:C_GC])
        per_batch = lambda first: pl.ds(first, dec_batch, stride=t_dec)
        for c in range(CONV_W // LANES):
            cs = slice(c * LANES, (c + 1) * LANES)
            ubuf[c, 0:SUBLANES, :] = jnp.zeros((SUBLANES, LANES), F32)
            ubuf[c, SUBLANES:SUBLANES + rows, :] = u[:, cs]
            p1_buf[c] = ubuf[c, SUBLANES - 1:SUBLANES - 1 + rows, :]
            p2_buf[c] = ubuf[c, SUBLANES - 2:SUBLANES - 2 + rows, :]
            p1_buf[c, per_batch(0), :] = st1_ref[:, cs]
            p2_buf[c, per_batch(0), :] = st0_ref[:, cs]
            p2_buf[c, per_batch(1), :] = st1_ref[:, cs]
            nc0_ref[:, cs] = ubuf[c, per_batch(SUBLANES + t_dec - 2), :]
            nc1_ref[:, cs] = ubuf[c, per_batch(SUBLANES + t_dec - 1), :]
        u1 = jnp.concatenate([p1_buf[c] for c in range(CONV_W // LANES)], axis=1)
        u2 = jnp.concatenate([p2_buf[c] for c in range(CONV_W // LANES)], axis=1)
        bg = _mm(xn, win_ref[:, C_BG:C_CG])
        gc = _mm(xn, win_ref[:, C_GC:IN_W])
        mix_s[:, ATTN_W:D_MODEL] = _conv_gate(u, u1, u2, cw_ref, bg, gc).astype(BF)

    c0 = pl.multiple_of(step * cols, cols)
    kt_new = kt_s[:, pl.ds(c0, cols)]
    vt_new = vt_s[:, pl.ds(c0, cols)]

    qrow = lax.broadcasted_iota(jnp.int32, (gr, nb * WINDOW), 0)
    ccol = lax.broadcasted_iota(jnp.int32, (gr, nb * WINDOW), 1)
    valid_c = ((ccol // WINDOW) == (qrow // t_dec)) & ((ccol % WINDOW) > (qrow % t_dec))
    qrow_n = lax.broadcasted_iota(jnp.int32, (gr, gr), 0)
    ncol = lax.broadcasted_iota(jnp.int32, (gr, gr), 1)
    valid_n = ((ncol // t_dec) == (qrow_n // t_dec)) & ((ncol % t_dec) <= (qrow_n % t_dec))
    lane = lax.broadcasted_iota(jnp.int32, (KV_W, WINDOW), 1)
    kept = lane < WINDOW - t_dec

    def update_window(b):
        place = (WINDOW - t_dec - t_dec * b) % cols
        k_in = pltpu.roll(kt_new, place, 1) if place else kt_new
        v_in = pltpu.roll(vt_new, place, 1) if place else vt_new
        nk_ref[b] = jnp.where(kept, pltpu.roll(ck_ref[b], WINDOW - t_dec, 1), k_in)
        nv_ref[b] = jnp.where(kept, pltpu.roll(cv_ref[b], WINDOW - t_dec, 1), v_in)

    def rows_of(sg):
        return pl.ds(pl.multiple_of(step * cols + sg * gr, gr), gr)

    def scores(n):
        sg, g = divmod(n, N_KV_HEADS)
        hd = slice(g * HEAD_DIM, (g + 1) * HEAD_DIM)
        qg = q_s[rows_of(sg), :]
        ktc = jnp.concatenate([ck_ref[sg * nb + b, hd, :] for b in range(nb)], axis=1).astype(BF)
        ktn = kt_new[hd, sg * gr:(sg + 1) * gr].astype(BF)
        qst = jnp.concatenate(
            [qg[:, (GROUP * g + h) * HEAD_DIM:(GROUP * g + h + 1) * HEAD_DIM] for h in range(GROUP)], axis=0)
        return _mm(qst, ktc), _mm(qst, ktn)

    def softmax(n, sc, sn):
        g = n % N_KV_HEADS
        ecs, ens, invs = [], [], []
        for h in range(GROUP):
            scm = jnp.where(valid_c, sc[h * gr:(h + 1) * gr], NEG)
            snm = jnp.where(valid_n, sn[h * gr:(h + 1) * gr], NEG)
            sink = sink_ref[layer, GROUP * g + h]
            m = jnp.maximum(jnp.maximum(jnp.max(scm, axis=-1, keepdims=True),
                                        jnp.max(snm, axis=-1, keepdims=True)), sink)
            ec = jnp.exp(scm - m)
            en = jnp.exp(snm - m)
            den = (jnp.sum(ec, axis=-1, keepdims=True) + jnp.sum(en, axis=-1, keepdims=True)
                   + jnp.exp(sink - m))
            invs.append(1.0 / den)
            ecs.append(ec.astype(BF))
            ens.append(en.astype(BF))
        return jnp.concatenate(ecs, axis=0), jnp.concatenate(ens, axis=0), invs

    heads = {}

    def values(n, ec, en, invs):
        sg, g = divmod(n, N_KV_HEADS)
        hd = slice(g * HEAD_DIM, (g + 1) * HEAD_DIM)
        vtc = jnp.concatenate([cv_ref[sg * nb + b, hd, :] for b in range(nb)], axis=1).astype(BF)
        vtn = vt_new[hd, sg * gr:(sg + 1) * gr].astype(BF)
        o = (lax.dot_general(ec, vtc, _NT, preferred_element_type=F32)
             + lax.dot_general(en, vtn, _NT, preferred_element_type=F32))
        heads.setdefault(sg, []).extend(o[h * gr:(h + 1) * gr] * invs[h] for h in range(GROUP))
        if g == N_KV_HEADS - 1:
            oa = jnp.concatenate(heads.pop(sg), axis=1)
            mix_s[rows_of(sg), 0:ATTN_W] = (oa * sga_s[rows_of(sg), :]).astype(BF)
            for b in range(sg * nb, (sg + 1) * nb):
                update_window(b)

    for n in range((chunk // nb) * N_KV_HEADS):
        values(n, *softmax(n, *scores(n)))

    @pl.when(step == last_step)
    def _():
        x1 = xs_buf[...] + _mm(mix_s[...], wout_ref[...])
        gate = _sigmoid(_mm(x1.astype(BF), wpg_ref[...]))
        x2 = x1 + gate * _mm(p_ref[...].astype(BF), wpp_ref[...])
        xs_buf[...] = x2

        @pl.when(layer == last_layer)
        def _():
            xo_ref[...] = _rmsnorm(x2, gfin_ref[...])


def _sample_trunk(x, p_all, cos, sin_s, cos_t, sin_t, st0, st1, sinks, g_norm, w_in, conv_w, w_out, w_pg, w_pp,
                  g_final, cache_kt, cache_vt, t_dec):
    rows = x.shape[0]
    dec_batch = rows // t_dec
    chunk = LANES // t_dec
    nb = SAMPLE_GROUP
    const = lambda l, s: (0, 0)
    layer = lambda l, s: (l, 0, 0)
    once = dict(pipeline_mode=pl.Buffered(1))
    full = lambda shape: pl.BlockSpec(shape, const, **once)
    per_layer = lambda *shape: pl.BlockSpec((None,) + shape, layer)
    cache = pl.BlockSpec((None, chunk, KV_W, WINDOW), lambda l, s: (l, s, 0, 0))
    return pl.pallas_call(
        functools.partial(_sample_kernel, rows=rows, t_dec=t_dec, chunk=chunk, nb=nb),
        name="sample_trunk",
        grid=(DEPTH, dec_batch // chunk),
        in_specs=[
            pl.BlockSpec(memory_space=pltpu.SMEM),
            full((rows, D_MODEL)),
            per_layer(rows, PLE_DIM),
            full((rows, LANES)),
            full((rows, LANES)),
            full((KV_W, rows)),
            full((KV_W, rows)),
            per_layer(dec_batch, CONV_W),
            per_layer(dec_batch, CONV_W),
            per_layer(1, D_MODEL),
            per_layer(D_MODEL, IN_W),
            per_layer(CONV_K, CONV_W),
            per_layer(D_MODEL, D_MODEL),
            per_layer(D_MODEL, D_MODEL),
            per_layer(PLE_DIM, D_MODEL),
            full((1, D_MODEL)),
            cache,
            cache,
        ],
        out_specs=[
            pl.BlockSpec((rows, D_MODEL), const),
            pl.BlockSpec((None, dec_batch, CONV_W), layer),
            pl.BlockSpec((None, dec_batch, CONV_W), layer),
            cache,
            cache,
        ],
        out_shape=[
            jax.ShapeDtypeStruct((rows, D_MODEL), F32),
            jax.ShapeDtypeStruct((DEPTH, dec_batch, CONV_W), F32),
            jax.ShapeDtypeStruct((DEPTH, dec_batch, CONV_W), F32),
            jax.ShapeDtypeStruct((DEPTH, dec_batch, KV_W, WINDOW), F32),
            jax.ShapeDtypeStruct((DEPTH, dec_batch, KV_W, WINDOW), F32),
        ],
        scratch_shapes=[
            pltpu.VMEM((rows, D_MODEL), F32),
            pltpu.VMEM((rows, ATTN_W), BF),
            pltpu.VMEM((KV_W, rows), F32),
            pltpu.VMEM((KV_W, rows), F32),
            pltpu.VMEM((rows, ATTN_W), F32),
            pltpu.VMEM((rows, D_MODEL), BF),
            pltpu.VMEM((CONV_W // LANES, rows + SUBLANES, LANES), F32),
            pltpu.VMEM((CONV_W // LANES, rows, LANES), F32),
            pltpu.VMEM((CONV_W // LANES, rows, LANES), F32),
        ],
        compiler_params=pltpu.CompilerParams(
            dimension_semantics=("arbitrary", "arbitrary"), vmem_limit_bytes=TRUNK_VMEM_LIMIT),
    )(sinks, x, p_all, cos, sin_s, cos_t, sin_t, st0, st1, g_norm, w_in, conv_w, w_out, w_pg, w_pp, g_final,
      cache_kt, cache_vt)


def _rope_angles(pos):
    inv = ROPE_THETA ** (-jnp.arange(0, HEAD_DIM, 2, dtype=F32) / HEAD_DIM)
    return pos[:, None] * inv[None, :]


def _rope_tables(pos):
    ang = _rope_angles(pos)
    cos = jnp.tile(jnp.cos(ang), (1, 2 * LANES // HEAD_DIM))
    sin = jnp.sin(ang)
    sin_signed = jnp.tile(jnp.concatenate([-sin, sin], axis=-1), (1, LANES // HEAD_DIM))
    return cos, sin_signed


def _rope_tables_t(pos):
    ang = _rope_angles(pos).T
    cos = jnp.tile(jnp.cos(ang), (2 * KV_W // HEAD_DIM, 1))
    sin = jnp.sin(ang)
    sin_signed = jnp.tile(jnp.concatenate([-sin, sin], axis=0), (KV_W // HEAD_DIM, 1))
    return cos, sin_signed


def kernel(x_prompt, x_sample, cache_k, cache_v, state_conv, p_prompt, p_sample, g_norm, w_in, sinks, conv_w,
           w_out, w_pg, w_pp, g_final):
    batch, seq, _ = x_prompt.shape
    dec_batch, t_dec, _ = x_sample.shape
    rows = dec_batch * t_dec

    cos_p, sin_p = _rope_tables(jnp.arange(seq, dtype=F32))
    pos_s = jnp.tile(PAST_LEN + jnp.arange(t_dec, dtype=F32), dec_batch)
    cos_s, sin_s = _rope_tables(pos_s)
    cos_st, sin_st = _rope_tables_t(pos_s)

    w_in_b, w_out_b, w_pg_b, w_pp_b = (w.astype(BF) for w in (w_in, w_out, w_pg, w_pp))
    g_norm3 = g_norm.reshape(DEPTH, 1, D_MODEL)
    g_fin2 = g_final.reshape(1, D_MODEL)
    to_t = lambda c: jnp.transpose(c, (0, 1, 3, 4, 2)).reshape(DEPTH, -1, KV_W, WINDOW)
    from_t = lambda c: jnp.transpose(c.reshape(DEPTH, -1, N_KV_HEADS, HEAD_DIM, WINDOW), (0, 1, 4, 2, 3))

    xp = x_prompt
    nkp, nvp, ncp = [], [], []
    for i in range(DEPTH):
        xp, kt, vt, ut = _prompt_layer(i, xp, p_prompt, cos_p, sin_p, sinks, g_norm3, w_in_b, conv_w, w_out_b,
                                       w_pg_b, w_pp_b, g_fin2, i == DEPTH - 1)
        nkp.append(kt)
        nvp.append(vt)
        ncp.append(ut)

    xs, nc0, nc1, nks, nvs = _sample_trunk(
        x_sample.reshape(rows, D_MODEL), p_sample.reshape(DEPTH, rows, PLE_DIM), cos_s, sin_s, cos_st, sin_st,
        state_conv[:, :, 0], state_conv[:, :, 1], sinks, g_norm3, w_in_b, conv_w, w_out_b, w_pg_b, w_pp_b,
        g_fin2, to_t(cache_k), to_t(cache_v), t_dec)

    return (xp, xs.reshape(dec_batch, t_dec, D_MODEL), from_t(jnp.stack(nkp)), from_t(jnp.stack(nvp)),
            jnp.stack(ncp), from_t(nks), from_t(nvs), jnp.stack([nc0, nc1], axis=2))
```

```python
import functools

import jax
import jax.numpy as jnp
from jax import lax
from jax.experimental import pallas as pl
from jax.experimental.pallas import tpu as pltpu

D_MODEL = 1024
DEPTH = 4
N_HEADS = 8
HEAD_DIM = 64
N_KV_HEADS = 2
GROUP = N_HEADS // N_KV_HEADS
ATTN_W = N_HEADS * HEAD_DIM
KV_W = N_KV_HEADS * HEAD_DIM
CONV_W = D_MODEL - ATTN_W
CONV_K = 3
WINDOW = 128
BLOCK = 128
PLE_DIM = 256
PAST_LEN = 8192
ROPE_THETA = 10000.0
EPS = 1e-6
SCALE = HEAD_DIM ** -0.5
NEG = -1e30
IN_W = 2 * ATTN_W + 2 * KV_W + 4 * CONV_W

C_Q = 0
C_K = ATTN_W
C_V = C_K + KV_W
C_GA = C_V + KV_W
C_BG = C_GA + ATTN_W
C_CG = C_BG + CONV_W
C_HC = C_CG + CONV_W
C_GC = C_HC + CONV_W

LANES = 128
SUBLANES = 8
ROPE_HALF = HEAD_DIM // 2

MXU_N = 256
PROMPT_TILE = 512
SAMPLE_GROUP = 8
VMEM_LIMIT = 56 * 1024 * 1024
TRUNK_VMEM_LIMIT = 60 * 1024 * 1024

BF = jnp.bfloat16
F32 = jnp.float32
_NT = (((1,), (1,)), ((), ()))


def _mm(a, w):
    return jnp.dot(a, w, preferred_element_type=F32)


def _sigmoid(x):
    return 1.0 / (1.0 + jnp.exp(-x))


def _silu(x):
    return x * _sigmoid(x)


def _rmsnorm(x, g):
    r = lax.rsqrt(jnp.mean(x * x, axis=-1, keepdims=True) + EPS)
    return (x * r) * g


def _rope(x, cos, sin_signed):
    rows = x.shape[0]
    lane = lax.broadcasted_iota(jnp.int32, (rows, LANES), 1)
    first_half = (lane % HEAD_DIM) < ROPE_HALF
    outs = []
    for c in range(x.shape[1] // LANES):
        xc = x[:, c * LANES:(c + 1) * LANES]
        rot = jnp.where(first_half, pltpu.roll(xc, LANES - ROPE_HALF, 1), pltpu.roll(xc, ROPE_HALF, 1))
        outs.append(xc * cos + rot * sin_signed)
    return outs[0] if len(outs) == 1 else jnp.concatenate(outs, axis=1)


def _conv_gate(u, u1, u2, cw_ref, bg, gc):
    yc = cw_ref[0:1, :] * u2 + cw_ref[1:2, :] * u1 + cw_ref[2:3, :] * u
    return bg * yc * _silu(gc)


def _prompt_kernel(sink_ref, xa_ref, xb_ref, p_ref, cos_ref, sin_ref, gn_ref, win_ref, cw_ref, wout_ref, wpg_ref,
                   wpp_ref, gfin_ref, xo_ref, kt_ref, vt_ref, ut_ref, kbuf, vbuf, ubuf, mix_buf, xn_buf, zc_buf,
                   q_buf, sga_buf, x1_buf, x1b_buf, *, tile, n_seq, final):
    i = pl.program_id(0)
    n_tiles = pl.num_programs(0) - 1
    first_of_seq = jnp.minimum(i, n_tiles - 1) % n_seq == 0

    @pl.when(i == 0)
    def _():
        mix_buf[...] = jnp.zeros(mix_buf.shape, BF)

    @pl.when(first_of_seq)
    def _():
        kbuf[0:BLOCK, :] = jnp.zeros((BLOCK, KV_W), BF)
        vbuf[:, 0:BLOCK] = jnp.zeros((KV_W, BLOCK), BF)
        ubuf[0:SUBLANES, :] = jnp.zeros((SUBLANES, CONV_W), F32)

    n_out = D_MODEL // MXU_N
    rows_per = tile // n_out
    for c in range(n_out):
        cs = slice(c * MXU_N, (c + 1) * MXU_N)
        x1 = (xb_ref[0, :, cs] + _mm(mix_buf[:, 0:ATTN_W], wout_ref[0:ATTN_W, cs])
              + _mm(mix_buf[:, ATTN_W:D_MODEL], wout_ref[ATTN_W:D_MODEL, cs]))
        x1_buf[:, cs] = x1
        x1b_buf[:, cs] = x1.astype(BF)
        rs = slice(c * rows_per, (c + 1) * rows_per)
        xn_buf[rs, :] = _rmsnorm(xa_ref[0, rs, :], gn_ref[...]).astype(BF)
    pb = p_ref[0].astype(BF)

    def gate_unit(c):
        cs = slice(c * MXU_N, (c + 1) * MXU_N)
        gate = _sigmoid(_mm(x1b_buf[...], wpg_ref[:, cs]))
        xo_ref[0, :, cs] = x1_buf[:, cs] + gate * _mm(pb, wpp_ref[:, cs])

    cos = cos_ref[...]
    sin_s = sin_ref[...]

    def proj(c0, width=MXU_N):
        return _mm(xn_buf[...], win_ref[:, c0:c0 + width])

    kv = proj(C_K, 2 * KV_W)
    k = _rope(kv[:, 0:KV_W], cos, sin_s)
    kbuf[BLOCK:BLOCK + tile, :] = k.astype(BF)
    vt = kv[:, KV_W:2 * KV_W].T
    vbuf[:, BLOCK:BLOCK + tile] = vt.astype(BF)
    kt_ref[0] = k[tile - BLOCK:, :].T
    vt_ref[0] = vt[:, tile - BLOCK:]
    for c in range(ATTN_W // MXU_N):
        cs = slice(c * MXU_N, (c + 1) * MXU_N)
        q_buf[:, cs] = (_rope(proj(C_Q + c * MXU_N), cos, sin_s) * SCALE).astype(BF)

    conv_cols = (C_CG, C_HC, C_BG, C_GC)
    n_planes = len(conv_cols)
    n_slabs = CONV_W // MXU_N
    n_gate = ATTN_W // MXU_N
    n_dense = n_gate + n_slabs * n_planes

    def dense_unit(n):
        if n < n_gate:
            sga_buf[:, n * MXU_N:(n + 1) * MXU_N] = _silu(proj(C_GA + n * MXU_N))
        else:
            half, plane = divmod(n - n_gate, n_planes)
            zc_buf[plane, :, half * MXU_N:(half + 1) * MXU_N] = proj(conv_cols[plane] + half * MXU_N)

    def conv_unit(half):
        cs = slice(half * MXU_N, (half + 1) * MXU_N)
        u = zc_buf[0, :, cs] * zc_buf[1, :, cs]
        ubuf[SUBLANES:SUBLANES + tile, cs] = u
        u1 = ubuf[SUBLANES - 1:SUBLANES - 1 + tile, cs]
        u2 = ubuf[SUBLANES - 2:SUBLANES - 2 + tile, cs]
        yc = cw_ref[0:1, cs] * u2 + cw_ref[1:2, cs] * u1 + cw_ref[2:3, cs] * u
        mix_buf[:, ATTN_W + half * MXU_N:ATTN_W + (half + 1) * MXU_N] = (
            zc_buf[2, :, cs] * yc * _silu(zc_buf[3, :, cs])).astype(BF)
        ubuf[0:SUBLANES, cs] = ubuf[tile:tile + SUBLANES, cs]
        ut_ref[0, :, cs] = u[tile - (CONV_K - 1):, :]

    row = lax.broadcasted_iota(jnp.int32, (BLOCK, BLOCK), 0)
    col = lax.broadcasted_iota(jnp.int32, (BLOCK, BLOCK), 1)
    own = row <= col
    no_prev = jnp.where(first_of_seq, NEG, 0.0).astype(F32)

    def scores(n):
        j, g = divmod(n, N_KV_HEADS)
        kcat = kbuf[j * BLOCK:(j + 2) * BLOCK, g * HEAD_DIM:(g + 1) * HEAD_DIM]
        qst = jnp.concatenate(
            [q_buf[j * BLOCK:(j + 1) * BLOCK, (GROUP * g + h) * HEAD_DIM:(GROUP * g + h + 1) * HEAD_DIM]
             for h in range(GROUP)], axis=0)
        return lax.dot_general(kcat, qst, _NT, preferred_element_type=F32)

    def softmax(n, sc):
        j, g = divmod(n, N_KV_HEADS)
        probs, invs = [], []
        for h in range(GROUP):
            sh = sc[:, h * BLOCK:(h + 1) * BLOCK]
            s_prev = sh[:BLOCK]
            if j == 0:
                s_prev = s_prev + no_prev
            sm = jnp.where(own, sh[BLOCK:], s_prev)
            sink = sink_ref[GROUP * g + h]
            m = jnp.maximum(jnp.max(sm, axis=0, keepdims=True), sink)
            e = jnp.exp(sm - m)
            den = jnp.sum(e, axis=0, keepdims=True) + jnp.exp(sink - m)
            invs.append(1.0 / den)
            eb = e.astype(BF)
            zero = jnp.zeros_like(eb)
            probs.append(jnp.concatenate([jnp.where(own, zero, eb), jnp.where(own, eb, zero)], axis=0))
        return jnp.concatenate(probs, axis=1), jnp.concatenate(invs, axis=1)

    heads = {}

    def values(n, probs, inv):
        j, g = divmod(n, N_KV_HEADS)
        vcat = vbuf[g * HEAD_DIM:(g + 1) * HEAD_DIM, j * BLOCK:(j + 2) * BLOCK]
        o = _mm(vcat, probs) * inv
        heads.setdefault(j, []).extend(o[:, h * BLOCK:(h + 1) * BLOCK] for h in range(GROUP))
        if g == N_KV_HEADS - 1:
            rows = slice(j * BLOCK, (j + 1) * BLOCK)
            oa = jnp.concatenate(heads.pop(j), axis=0).T
            mix_buf[rows, 0:ATTN_W] = (oa * sga_buf[rows, :]).astype(BF)

    n_units = (tile // BLOCK) * N_KV_HEADS
    conv_at = {n_gate + (h + 1) * n_planes: h for h in range(n_slabs)}
    n_steps = max(n_units + 2, max(conv_at) + 1)
    gate_at = {2 * c + 1: c for c in range(n_out)}
    sc_q, sm_q = {}, {}
    for t in range(n_steps):
        if t < n_units:
            sc_q[t] = scores(t)
        if t < n_dense:
            dense_unit(t)
        if 0 <= t - 2 < n_units:
            values(t - 2, *sm_q.pop(t - 2))
        if t in gate_at:
            gate_unit(gate_at[t])
        if 0 <= t - 1 < n_units:
            sm_q[t - 1] = softmax(t - 1, sc_q.pop(t - 1))
        if t in conv_at:
            conv_unit(conv_at[t])

    kbuf[0:BLOCK, :] = kbuf[tile:tile + BLOCK, :]
    vbuf[:, 0:BLOCK] = vbuf[:, tile:tile + BLOCK]

    if final:
        xo_ref[0] = _rmsnorm(xo_ref[0], gfin_ref[...])


def _prompt_layer(i, x, p_all, cos, sin_s, sinks, g_norm, w_in, conv_w, w_out, w_pg, w_pp, g_final, final):
    batch, seq, _ = x.shape
    tile = PROMPT_TILE
    n_seq = seq // tile
    n_tiles = batch * n_seq
    in_tile = lambda t: jnp.minimum(t, n_tiles - 1)
    out_tile = lambda t: jnp.maximum(t - 1, 0)
    const = lambda t: (0, 0)
    layer = lambda t: (i, 0, 0)
    resident = dict(pipeline_mode=pl.Buffered(1))
    return pl.pallas_call(
        functools.partial(_prompt_kernel, tile=tile, n_seq=n_seq, final=final),
        name=f"prompt_layer{i}",
        grid=(n_tiles + 1,),
        in_specs=[
            pl.BlockSpec(memory_space=pltpu.SMEM),
            pl.BlockSpec((1, tile, D_MODEL), lambda t: (in_tile(t) // n_seq, in_tile(t) % n_seq, 0)),
            pl.BlockSpec((1, tile, D_MODEL), lambda t: (out_tile(t) // n_seq, out_tile(t) % n_seq, 0)),
            pl.BlockSpec((None, 1, tile, PLE_DIM),
                         lambda t: (i, out_tile(t) // n_seq, out_tile(t) % n_seq, 0)),
            pl.BlockSpec((tile, LANES), lambda t: (in_tile(t) % n_seq, 0)),
            pl.BlockSpec((tile, LANES), lambda t: (in_tile(t) % n_seq, 0)),
            pl.BlockSpec((None, 1, D_MODEL), layer, **resident),
            pl.BlockSpec((None, D_MODEL, IN_W), layer, **resident),
            pl.BlockSpec((None, CONV_K, CONV_W), layer, **resident),
            pl.BlockSpec((None, D_MODEL, D_MODEL), layer, **resident),
            pl.BlockSpec((None, D_MODEL, D_MODEL), layer, **resident),
            pl.BlockSpec((None, PLE_DIM, D_MODEL), layer, **resident),
            pl.BlockSpec((1, D_MODEL), const, **resident),
        ],
        out_specs=[
            pl.BlockSpec((1, tile, D_MODEL), lambda t: (out_tile(t) // n_seq, out_tile(t) % n_seq, 0)),
            pl.BlockSpec((1, BLOCK, KV_W), lambda t: (in_tile(t) // n_seq, 0, 0)),
            pl.BlockSpec((1, BLOCK, KV_W), lambda t: (in_tile(t) // n_seq, 0, 0)),
            pl.BlockSpec((1, CONV_K - 1, CONV_W), lambda t: (in_tile(t) // n_seq, 0, 0)),
        ],
        out_shape=[
            jax.ShapeDtypeStruct((batch, seq, D_MODEL), F32),
            jax.ShapeDtypeStruct((batch, BLOCK, KV_W), F32),
            jax.ShapeDtypeStruct((batch, BLOCK, KV_W), F32),
            jax.ShapeDtypeStruct((batch, CONV_K - 1, CONV_W), F32),
        ],
        scratch_shapes=[
            pltpu.VMEM((tile + BLOCK, KV_W), BF),
            pltpu.VMEM((KV_W, tile + BLOCK), BF),
            pltpu.VMEM((tile + SUBLANES, CONV_W), F32),
            pltpu.VMEM((tile, D_MODEL), BF),
            pltpu.VMEM((tile, D_MODEL), BF),
            pltpu.VMEM((4, tile, CONV_W), F32),
            pltpu.VMEM((tile, ATTN_W), BF),
            pltpu.VMEM((tile, ATTN_W), F32),
            pltpu.VMEM((tile, D_MODEL), F32),
            pltpu.VMEM((tile, D_MODEL), BF),
        ],
        compiler_params=pltpu.CompilerParams(dimension_semantics=("arbitrary",), vmem_limit_bytes=VMEM_LIMIT),
    )(sinks[i], x, x, p_all, cos, sin_s, g_norm, w_in, conv_w, w_out, w_pg, w_pp, g_final)


def _rope_t(x, cos_t, sin_t):
    parts = [x[r:r + ROPE_HALF] for r in range(0, x.shape[0], ROPE_HALF)]
    swapped = jnp.concatenate([parts[n ^ 1] for n in range(len(parts))], axis=0)
    return x * cos_t + swapped * sin_t


def _sample_kernel(sink_ref, x_ref, p_ref, cos_ref, sin_ref, cost_ref, sint_ref, st0_ref, st1_ref, gn_ref, win_ref,
                   cw_ref, wout_ref, wpg_ref, wpp_ref, gfin_ref, ck_ref, cv_ref,
                   xo_ref, nc0_ref, nc1_ref, nk_ref, nv_ref,
                   xs_buf, q_s, kt_s, vt_s, sga_s, mix_s, ubuf, p1_buf, p2_buf, *, rows, t_dec, chunk, nb):
    layer = pl.program_id(0)
    step = pl.program_id(1)
    last_layer = pl.num_programs(0) - 1
    last_step = pl.num_programs(1) - 1
    dec_batch = rows // t_dec
    gr = nb * t_dec
    cols = chunk * t_dec

    @pl.when((layer == 0) & (step == 0))
    def _():
        xs_buf[...] = x_ref[...]

    @pl.when(step == 0)
    def _():
        xn = _rmsnorm(xs_buf[...], gn_ref[...]).astype(BF)
        q_s[...] = (_rope(_mm(xn, win_ref[:, C_Q:C_K]), cos_ref[...], sin_ref[...]) * SCALE).astype(BF)
        kvt = lax.dot_general(win_ref[:, C_K:C_GA], xn, (((0,), (1,)), ((), ())),
                              preferred_element_type=F32)
        kt_s[...] = _rope_t(kvt[0:KV_W], cost_ref[...], sint_ref[...])
        vt_s[...] = kvt[KV_W:2 * KV_W]
        sga_s[...] = _silu(_mm(xn, win_ref[:, C_GA:C_BG]))
        u = _mm(xn, win_ref[:, C_CG:C_HC]) * _mm(xn, win_ref[:, C_HC:C_GC])
        per_batch = lambda first: pl.ds(first, dec_batch, stride=t_dec)
        for c in range(CONV_W // LANES):
            cs = slice(c * LANES, (c + 1) * LANES)
            ubuf[c, 0:SUBLANES, :] = jnp.zeros((SUBLANES, LANES), F32)
            ubuf[c, SUBLANES:SUBLANES + rows, :] = u[:, cs]
            p1_buf[c] = ubuf[c, SUBLANES - 1:SUBLANES - 1 + rows, :]
            p2_buf[c] = ubuf[c, SUBLANES - 2:SUBLANES - 2 + rows, :]
            p1_buf[c, per_batch(0), :] = st1_ref[:, cs]
            p2_buf[c, per_batch(0), :] = st0_ref[:, cs]
            p2_buf[c, per_batch(1), :] = st1_ref[:, cs]
            nc0_ref[:, cs] = ubuf[c, per_batch(SUBLANES + t_dec - 2), :]
            nc1_ref[:, cs] = ubuf[c, per_batch(SUBLANES + t_dec - 1), :]
        u1 = jnp.concatenate([p1_buf[c] for c in range(CONV_W // LANES)], axis=1)
        u2 = jnp.concatenate([p2_buf[c] for c in range(CONV_W // LANES)], axis=1)
        bg = _mm(xn, win_ref[:, C_BG:C_CG])
        gc = _mm(xn, win_ref[:, C_GC:IN_W])
        mix_s[:, ATTN_W:D_MODEL] = _conv_gate(u, u1, u2, cw_ref, bg, gc).astype(BF)

    c0 = pl.multiple_of(step * cols, cols)
    kt_new = kt_s[:, pl.ds(c0, cols)]
    vt_new = vt_s[:, pl.ds(c0, cols)]

    qrow = lax.broadcasted_iota(jnp.int32, (gr, nb * WINDOW), 0)
    ccol = lax.broadcasted_iota(jnp.int32, (gr, nb * WINDOW), 1)
    valid_c = ((ccol // WINDOW) == (qrow // t_dec)) & ((ccol % WINDOW) > (qrow % t_dec))
    qrow_n = lax.broadcasted_iota(jnp.int32, (gr, gr), 0)
    ncol = lax.broadcasted_iota(jnp.int32, (gr, gr), 1)
    valid_n = ((ncol // t_dec) == (qrow_n // t_dec)) & ((ncol % t_dec) <= (qrow_n % t_dec))
    lane = lax.broadcasted_iota(jnp.int32, (KV_W, WINDOW), 1)
    kept = lane < WINDOW - t_dec

    def update_window(b):
        place = (WINDOW - t_dec - t_dec * b) % cols
        k_in = pltpu.roll(kt_new, place, 1) if place else kt_new
        v_in = pltpu.roll(vt_new, place, 1) if place else vt_new
        nk_ref[b] = jnp.where(kept, pltpu.roll(ck_ref[b], WINDOW - t_dec, 1), k_in)
        nv_ref[b] = jnp.where(kept, pltpu.roll(cv_ref[b], WINDOW - t_dec, 1), v_in)

    def rows_of(sg):
        return pl.ds(pl.multiple_of(step * cols + sg * gr, gr), gr)

    def scores(n):
        sg, g = divmod(n, N_KV_HEADS)
        hd = slice(g * HEAD_DIM, (g + 1) * HEAD_DIM)
        qg = q_s[rows_of(sg), :]
        ktc = jnp.concatenate([ck_ref[sg * nb + b, hd, :] for b in range(nb)], axis=1).astype(BF)
        ktn = kt_new[hd, sg * gr:(sg + 1) * gr].astype(BF)
        qst = jnp.concatenate(
            [qg[:, (GROUP * g + h) * HEAD_DIM:(GROUP * g + h + 1) * HEAD_DIM] for h in range(GROUP)], axis=0)
        return _mm(qst, ktc), _mm(qst, ktn)

    def softmax(n, sc, sn):
        g = n % N_KV_HEADS
        ecs, ens, invs = [], [], []
        for h in range(GROUP):
            scm = jnp.where(valid_c, sc[h * gr:(h + 1) * gr], NEG)
            snm = jnp.where(valid_n, sn[h * gr:(h + 1) * gr], NEG)
            sink = sink_ref[layer, GROUP * g + h]
            m = jnp.maximum(jnp.maximum(jnp.max(scm, axis=-1, keepdims=True),
                                        jnp.max(snm, axis=-1, keepdims=True)), sink)
            ec = jnp.exp(scm - m)
            en = jnp.exp(snm - m)
            den = (jnp.sum(ec, axis=-1, keepdims=True) + jnp.sum(en, axis=-1, keepdims=True)
                   + jnp.exp(sink - m))
            invs.append(1.0 / den)
            ecs.append(ec.astype(BF))
            ens.append(en.astype(BF))
        return jnp.concatenate(ecs, axis=0), jnp.concatenate(ens, axis=0), invs

    heads = {}

    def values(n, ec, en, invs):
        sg, g = divmod(n, N_KV_HEADS)
        hd = slice(g * HEAD_DIM, (g + 1) * HEAD_DIM)
        vtc = jnp.concatenate([cv_ref[sg * nb + b, hd, :] for b in range(nb)], axis=1).astype(BF)
        vtn = vt_new[hd, sg * gr:(sg + 1) * gr].astype(BF)
        o = (lax.dot_general(ec, vtc, _NT, preferred_element_type=F32)
             + lax.dot_general(en, vtn, _NT, preferred_element_type=F32))
        heads.setdefault(sg, []).extend(o[h * gr:(h + 1) * gr] * invs[h] for h in range(GROUP))
        if g == N_KV_HEADS - 1:
            oa = jnp.concatenate(heads.pop(sg), axis=1)
            mix_s[rows_of(sg), 0:ATTN_W] = (oa * sga_s[rows_of(sg), :]).astype(BF)
            for b in range(sg * nb, (sg + 1) * nb):
                update_window(b)

    for n in range((chunk // nb) * N_KV_HEADS):
        values(n, *softmax(n, *scores(n)))

    @pl.when(step == last_step)
    def _():
        x1 = xs_buf[...] + _mm(mix_s[...], wout_ref[...])
        gate = _sigmoid(_mm(x1.astype(BF), wpg_ref[...]))
        x2 = x1 + gate * _mm(p_ref[...].astype(BF), wpp_ref[...])
        xs_buf[...] = x2

        @pl.when(layer == last_layer)
        def _():
            xo_ref[...] = _rmsnorm(x2, gfin_ref[...])


def _sample_trunk(x, p_all, cos, sin_s, cos_t, sin_t, st0, st1, sinks, g_norm, w_in, conv_w, w_out, w_pg, w_pp,
                  g_final, cache_kt, cache_vt, t_dec):
    rows = x.shape[0]
    dec_batch = rows // t_dec
    chunk = LANES // t_dec
    nb = SAMPLE_GROUP
    const = lambda l, s: (0, 0)
    layer = lambda l, s: (l, 0, 0)
    once = dict(pipeline_mode=pl.Buffered(1))
    full = lambda shape: pl.BlockSpec(shape, const, **once)
    per_layer = lambda *shape: pl.BlockSpec((None,) + shape, layer)
    cache = pl.BlockSpec((None, chunk, KV_W, WINDOW), lambda l, s: (l, s, 0, 0))
    return pl.pallas_call(
        functools.partial(_sample_kernel, rows=rows, t_dec=t_dec, chunk=chunk, nb=nb),
        name="sample_trunk",
        grid=(DEPTH, dec_batch // chunk),
        in_specs=[
            pl.BlockSpec(memory_space=pltpu.SMEM),
            full((rows, D_MODEL)),
            per_layer(rows, PLE_DIM),
            full((rows, LANES)),
            full((rows, LANES)),
            full((KV_W, rows)),
            full((KV_W, rows)),
            per_layer(dec_batch, CONV_W),
            per_layer(dec_batch, CONV_W),
            per_layer(1, D_MODEL),
            per_layer(D_MODEL, IN_W),
            per_layer(CONV_K, CONV_W),
            per_layer(D_MODEL, D_MODEL),
            per_layer(D_MODEL, D_MODEL),
            per_layer(PLE_DIM, D_MODEL),
            full((1, D_MODEL)),
            cache,
            cache,
        ],
        out_specs=[
            pl.BlockSpec((rows, D_MODEL), const),
            pl.BlockSpec((None, dec_batch, CONV_W), layer),
            pl.BlockSpec((None, dec_batch, CONV_W), layer),
            cache,
            cache,
        ],
        out_shape=[
            jax.ShapeDtypeStruct((rows, D_MODEL), F32),
            jax.ShapeDtypeStruct((DEPTH, dec_batch, CONV_W), F32),
            jax.ShapeDtypeStruct((DEPTH, dec_batch, CONV_W), F32),
            jax.ShapeDtypeStruct((DEPTH, dec_batch, KV_W, WINDOW), F32),
            jax.ShapeDtypeStruct((DEPTH, dec_batch, KV_W, WINDOW), F32),
        ],
        scratch_shapes=[
            pltpu.VMEM((rows, D_MODEL), F32),
            pltpu.VMEM((rows, ATTN_W), BF),
            pltpu.VMEM((KV_W, rows), F32),
            pltpu.VMEM((KV_W, rows), F32),
            pltpu.VMEM((rows, ATTN_W), F32),
            pltpu.VMEM((rows, D_MODEL), BF),
            pltpu.VMEM((CONV_W // LANES, rows + SUBLANES, LANES), F32),
            pltpu.VMEM((CONV_W // LANES, rows, LANES), F32),
            pltpu.VMEM((CONV_W // LANES, rows, LANES), F32),
        ],
        compiler_params=pltpu.CompilerParams(
            dimension_semantics=("arbitrary", "arbitrary"), vmem_limit_bytes=TRUNK_VMEM_LIMIT),
    )(sinks, x, p_all, cos, sin_s, cos_t, sin_t, st0, st1, g_norm, w_in, conv_w, w_out, w_pg, w_pp, g_final,
      cache_kt, cache_vt)


def _rope_angles(pos):
    inv = ROPE_THETA ** (-jnp.arange(0, HEAD_DIM, 2, dtype=F32) / HEAD_DIM)
    return pos[:, None] * inv[None, :]


def _rope_tables(pos):
    ang = _rope_angles(pos)
    cos = jnp.tile(jnp.cos(ang), (1, 2 * LANES // HEAD_DIM))
    sin = jnp.sin(ang)
    sin_signed = jnp.tile(jnp.concatenate([-sin, sin], axis=-1), (1, LANES // HEAD_DIM))
    return cos, sin_signed


def _rope_tables_t(pos):
    ang = _rope_angles(pos).T
    cos = jnp.tile(jnp.cos(ang), (2 * KV_W // HEAD_DIM, 1))
    sin = jnp.sin(ang)
    sin_signed = jnp.tile(jnp.concatenate([-sin, sin], axis=0), (KV_W // HEAD_DIM, 1))
    return cos, sin_signed


def kernel(x_prompt, x_sample, cache_k, cache_v, state_conv, p_prompt, p_sample, g_norm, w_in, sinks, conv_w,
           w_out, w_pg, w_pp, g_final):
    batch, seq, _ = x_prompt.shape
    dec_batch, t_dec, _ = x_sample.shape
    rows = dec_batch * t_dec

    cos_p, sin_p = _rope_tables(jnp.arange(seq, dtype=F32))
    pos_s = jnp.tile(PAST_LEN + jnp.arange(t_dec, dtype=F32), dec_batch)
    cos_s, sin_s = _rope_tables(pos_s)
    cos_st, sin_st = _rope_tables_t(pos_s)

    w_in_b, w_out_b, w_pg_b, w_pp_b = (w.astype(BF) for w in (w_in, w_out, w_pg, w_pp))
    g_norm3 = g_norm.reshape(DEPTH, 1, D_MODEL)
    g_fin2 = g_final.reshape(1, D_MODEL)
    to_t = lambda c: jnp.transpose(c, (0, 1, 3, 4, 2)).reshape(DEPTH, -1, KV_W, WINDOW)
    from_t = lambda c: jnp.transpose(c.reshape(DEPTH, -1, N_KV_HEADS, HEAD_DIM, WINDOW), (0, 1, 4, 2, 3))

    xp = x_prompt
    nkp, nvp, ncp = [], [], []
    for i in range(DEPTH):
        xp, kt, vt, ut = _prompt_layer(i, xp, p_prompt, cos_p, sin_p, sinks, g_norm3, w_in_b, conv_w, w_out_b,
                                       w_pg_b, w_pp_b, g_fin2, i == DEPTH - 1)
        nkp.append(kt)
        nvp.append(vt)
        ncp.append(ut)

    xs, nc0, nc1, nks, nvs = _sample_trunk(
        x_sample.reshape(rows, D_MODEL), p_sample.reshape(DEPTH, rows, PLE_DIM), cos_s, sin_s, cos_st, sin_st,
        state_conv[:, :, 0], state_conv[:, :, 1], sinks, g_norm3, w_in_b, conv_w, w_out_b, w_pg_b, w_pp_b,
        g_fin2, to_t(cache_k), to_t(cache_v), t_dec)

    return (xp, xs.reshape(dec_batch, t_dec, D_MODEL), from_t(jnp.stack(nkp)), from_t(jnp.stack(nvp)),
            jnp.stack(ncp), from_t(nks), from_t(nvs), jnp.stack([nc0, nc1], axis=2))
```

```python
import functools

import jax
import jax.numpy as jnp
from jax import lax
from jax.experimental import pallas as pl
from jax.experimental.pallas import tpu as pltpu

D_MODEL = 1024
DEPTH = 4
N_HEADS = 8
HEAD_DIM = 64
N_KV_HEADS = 2
GROUP = N_HEADS // N_KV_HEADS
ATTN_W = N_HEADS * HEAD_DIM
KV_W = N_KV_HEADS * HEAD_DIM
CONV_W = D_MODEL - ATTN_W
CONV_K = 3
WINDOW = 128
BLOCK = 128
PLE_DIM = 256
PAST_LEN = 8192
ROPE_THETA = 10000.0
EPS = 1e-6
SCALE = HEAD_DIM ** -0.5
NEG = -1e30
IN_W = 2 * ATTN_W + 2 * KV_W + 4 * CONV_W

C_Q = 0
C_K = ATTN_W
C_V = C_K + KV_W
C_GA = C_V + KV_W
C_BG = C_GA + ATTN_W
C_CG = C_BG + CONV_W
C_HC = C_CG + CONV_W
C_GC = C_HC + CONV_W

LANES = 128
SUBLANES = 8
ROPE_HALF = HEAD_DIM // 2

MXU_N = 256
PROMPT_TILE = 512
SAMPLE_GROUP = 8
VMEM_LIMIT = 56 * 1024 * 1024
TRUNK_VMEM_LIMIT = 60 * 1024 * 1024

BF = jnp.bfloat16
F32 = jnp.float32
_NT = (((1,), (1,)), ((), ()))


def _mm(a, w):
    return jnp.dot(a, w, preferred_element_type=F32)


def _sigmoid(x):
    return 1.0 / (1.0 + jnp.exp(-x))


def _silu(x):
    return x * _sigmoid(x)


def _rmsnorm(x, g):
    r = lax.rsqrt(jnp.mean(x * x, axis=-1, keepdims=True) + EPS)
    return (x * r) * g


def _rope(x, cos, sin_signed):
    rows = x.shape[0]
    lane = lax.broadcasted_iota(jnp.int32, (rows, LANES), 1)
    first_half = (lane % HEAD_DIM) < ROPE_HALF
    outs = []
    for c in range(x.shape[1] // LANES):
        xc = x[:, c * LANES:(c + 1) * LANES]
        rot = jnp.where(first_half, pltpu.roll(xc, LANES - ROPE_HALF, 1), pltpu.roll(xc, ROPE_HALF, 1))
        outs.append(xc * cos + rot * sin_signed)
    return outs[0] if len(outs) == 1 else jnp.concatenate(outs, axis=1)


def _conv_gate(u, u1, u2, cw_ref, bg, gc):
    yc = cw_ref[0:1, :] * u2 + cw_ref[1:2, :] * u1 + cw_ref[2:3, :] * u
    return bg * yc * _silu(gc)


def _prompt_kernel(sink_ref, xa_ref, xb_ref, p_ref, cos_ref, sin_ref, gn_ref, win_ref, cw_ref, wout_ref, wpg_ref,
                   wpp_ref, gfin_ref, xo_ref, kt_ref, vt_ref, ut_ref, kbuf, vbuf, ubuf, mix_buf, xn_buf, zc_buf,
                   q_buf, sga_buf, x1_buf, x1b_buf, *, tile, n_seq, final):
    i = pl.program_id(0)
    n_tiles = pl.num_programs(0) - 1
    first_of_seq = jnp.minimum(i, n_tiles - 1) % n_seq == 0

    @pl.when(i == 0)
    def _():
        mix_buf[...] = jnp.zeros(mix_buf.shape, BF)

    @pl.when(first_of_seq)
    def _():
        kbuf[0:BLOCK, :] = jnp.zeros((BLOCK, KV_W), BF)
        vbuf[:, 0:BLOCK] = jnp.zeros((KV_W, BLOCK), BF)
        ubuf[0:SUBLANES, :] = jnp.zeros((SUBLANES, CONV_W), F32)

    n_out = D_MODEL // MXU_N
    rows_per = tile // n_out
    for c in range(n_out):
        cs = slice(c * MXU_N, (c + 1) * MXU_N)
        x1 = (xb_ref[0, :, cs] + _mm(mix_buf[:, 0:ATTN_W], wout_ref[0:ATTN_W, cs])
              + _mm(mix_buf[:, ATTN_W:D_MODEL], wout_ref[ATTN_W:D_MODEL, cs]))
        x1_buf[:, cs] = x1
        x1b_buf[:, cs] = x1.astype(BF)
        rs = slice(c * rows_per, (c + 1) * rows_per)
        xn_buf[rs, :] = _rmsnorm(xa_ref[0, rs, :], gn_ref[...]).astype(BF)
    pb = p_ref[0].astype(BF)

    def gate_unit(c):
        cs = slice(c * MXU_N, (c + 1) * MXU_N)
        gate = _sigmoid(_mm(x1b_buf[...], wpg_ref[:, cs]))
        xo_ref[0, :, cs] = x1_buf[:, cs] + gate * _mm(pb, wpp_ref[:, cs])

    cos = cos_ref[...]
    sin_s = sin_ref[...]

    def proj(c0, width=MXU_N):
        return _mm(xn_buf[...], win_ref[:, c0:c0 + width])

    kv = proj(C_K, 2 * KV_W)
    k = _rope(kv[:, 0:KV_W], cos, sin_s)
    kbuf[BLOCK:BLOCK + tile, :] = k.astype(BF)
    vt = kv[:, KV_W:2 * KV_W].T
    vbuf[:, BLOCK:BLOCK + tile] = vt.astype(BF)
    kt_ref[0] = k[tile - BLOCK:, :].T
    vt_ref[0] = vt[:, tile - BLOCK:]
    for c in range(ATTN_W // MXU_N):
        cs = slice(c * MXU_N, (c + 1) * MXU_N)
        q_buf[:, cs] = (_rope(proj(C_Q + c * MXU_N), cos, sin_s) * SCALE).astype(BF)

    conv_cols = (C_CG, C_HC, C_BG, C_GC)
    n_planes = len(conv_cols)
    n_slabs = CONV_W // MXU_N
    n_gate = ATTN_W // MXU_N
    n_dense = n_gate + n_slabs * n_planes

    def dense_unit(n):
        if n < n_gate:
            sga_buf[:, n * MXU_N:(n + 1) * MXU_N] = _silu(proj(C_GA + n * MXU_N))
        else:
            half, plane = divmod(n - n_gate, n_planes)
            zc_buf[plane, :, half * MXU_N:(half + 1) * MXU_N] = proj(conv_cols[plane] + half * MXU_N)

    def conv_unit(half):
        cs = slice(half * MXU_N, (half + 1) * MXU_N)
        u = zc_buf[0, :, cs] * zc_buf[1, :, cs]
        ubuf[SUBLANES:SUBLANES + tile, cs] = u
        u1 = ubuf[SUBLANES - 1:SUBLANES - 1 + tile, cs]
        u2 = ubuf[SUBLANES - 2:SUBLANES - 2 + tile, cs]
        yc = cw_ref[0:1, cs] * u2 + cw_ref[1:2, cs] * u1 + cw_ref[2:3, cs] * u
        mix_buf[:, ATTN_W + half * MXU_N:ATTN_W + (half + 1) * MXU_N] = (
            zc_buf[2, :, cs] * yc * _silu(zc_buf[3, :, cs])).astype(BF)
        ubuf[0:SUBLANES, cs] = ubuf[tile:tile + SUBLANES, cs]
        ut_ref[0, :, cs] = u[tile - (CONV_K - 1):, :]

    row = lax.broadcasted_iota(jnp.int32, (BLOCK, BLOCK), 0)
    col = lax.broadcasted_iota(jnp.int32, (BLOCK, BLOCK), 1)
    own = row <= col
    no_prev = jnp.where(first_of_seq, NEG, 0.0).astype(F32)

    def scores(n):
        j, g = divmod(n, N_KV_HEADS)
        kcat = kbuf[j * BLOCK:(j + 2) * BLOCK, g * HEAD_DIM:(g + 1) * HEAD_DIM]
        qst = jnp.concatenate(
            [q_buf[j * BLOCK:(j + 1) * BLOCK, (GROUP * g + h) * HEAD_DIM:(GROUP * g + h + 1) * HEAD_DIM]
             for h in range(GROUP)], axis=0)
        return lax.dot_general(kcat, qst, _NT, preferred_element_type=F32)

    def softmax(n, sc):
        j, g = divmod(n, N_KV_HEADS)
        probs, invs = [], []
        for h in range(GROUP):
            sh = sc[:, h * BLOCK:(h + 1) * BLOCK]
            s_prev = sh[:BLOCK]
            if j == 0:
                s_prev = s_prev + no_prev
            sm = jnp.where(own, sh[BLOCK:], s_prev)
            sink = sink_ref[GROUP * g + h]
            m = jnp.maximum(jnp.max(sm, axis=0, keepdims=True), sink)
            e = jnp.exp(sm - m)
            den = jnp.sum(e, axis=0, keepdims=True) + jnp.exp(sink - m)
            invs.append(1.0 / den)
            eb = e.astype(BF)
            zero = jnp.zeros_like(eb)
            probs.append(jnp.concatenate([jnp.where(own, zero, eb), jnp.where(own, eb, zero)], axis=0))
        return jnp.concatenate(probs, axis=1), jnp.concatenate(invs, axis=1)

    heads = {}

    def values(n, probs, inv):
        j, g = divmod(n, N_KV_HEADS)
        vcat = vbuf[g * HEAD_DIM:(g + 1) * HEAD_DIM, j * BLOCK:(j + 2) * BLOCK]
        o = _mm(vcat, probs) * inv
        heads.setdefault(j, []).extend(o[:, h * BLOCK:(h + 1) * BLOCK] for h in range(GROUP))
        if g == N_KV_HEADS - 1:
            rows = slice(j * BLOCK, (j + 1) * BLOCK)
            oa = jnp.concatenate(heads.pop(j), axis=0).T
            mix_buf[rows, 0:ATTN_W] = (oa * sga_buf[rows, :]).astype(BF)

    n_units = (tile // BLOCK) * N_KV_HEADS
    conv_at = {n_gate + (h + 1) * n_planes: h for h in range(n_slabs)}
    n_steps = max(n_units + 2, max(conv_at) + 1)
    gate_at = {2 * c + 1: c for c in range(n_out)}
    sc_q, sm_q = {}, {}
    for t in range(n_steps):
        if t < n_units:
            sc_q[t] = scores(t)
        if t < n_dense:
            dense_unit(t)
        if 0 <= t - 2 < n_units:
            values(t - 2, *sm_q.pop(t - 2))
        if t in gate_at:
            gate_unit(gate_at[t])
        if 0 <= t - 1 < n_units:
            sm_q[t - 1] = softmax(t - 1, sc_q.pop(t - 1))
        if t in conv_at:
            conv_unit(conv_at[t])

    kbuf[0:BLOCK, :] = kbuf[tile:tile + BLOCK, :]
    vbuf[:, 0:BLOCK] = vbuf[:, tile:tile + BLOCK]

    if final:
        xo_ref[0] = _rmsnorm(xo_ref[0], gfin_ref[...])


def _prompt_layer(i, x, p_all, cos, sin_s, sinks, g_norm, w_in, conv_w, w_out, w_pg, w_pp, g_final, final):
    batch, seq, _ = x.shape
    tile = PROMPT_TILE
    n_seq = seq // tile
    n_tiles = batch * n_seq
    in_tile = lambda t: jnp.minimum(t, n_tiles - 1)
    out_tile = lambda t: jnp.maximum(t - 1, 0)
    const = lambda t: (0, 0)
    layer = lambda t: (i, 0, 0)
    resident = dict(pipeline_mode=pl.Buffered(1))
    return pl.pallas_call(
        functools.partial(_prompt_kernel, tile=tile, n_seq=n_seq, final=final),
        name=f"prompt_layer{i}",
        grid=(n_tiles + 1,),
        in_specs=[
            pl.BlockSpec(memory_space=pltpu.SMEM),
            pl.BlockSpec((1, tile, D_MODEL), lambda t: (in_tile(t) // n_seq, in_tile(t) % n_seq, 0)),
            pl.BlockSpec((1, tile, D_MODEL), lambda t: (out_tile(t) // n_seq, out_tile(t) % n_seq, 0)),
            pl.BlockSpec((None, 1, tile, PLE_DIM),
                         lambda t: (i, out_tile(t) // n_seq, out_tile(t) % n_seq, 0)),
            pl.BlockSpec((tile, LANES), lambda t: (in_tile(t) % n_seq, 0)),
            pl.BlockSpec((tile, LANES), lambda t: (in_tile(t) % n_seq, 0)),
            pl.BlockSpec((None, 1, D_MODEL), layer, **resident),
            pl.BlockSpec((None, D_MODEL, IN_W), layer, **resident),
            pl.BlockSpec((None, CONV_K, CONV_W), layer, **resident),
            pl.BlockSpec((None, D_MODEL, D_MODEL), layer, **resident),
            pl.BlockSpec((None, D_MODEL, D_MODEL), layer, **resident),
            pl.BlockSpec((None, PLE_DIM, D_MODEL), layer, **resident),
            pl.BlockSpec((1, D_MODEL), const, **resident),
        ],
        out_specs=[
            pl.BlockSpec((1, tile, D_MODEL), lambda t: (out_tile(t) // n_seq, out_tile(t) % n_seq, 0)),
            pl.BlockSpec((1, BLOCK, KV_W), lambda t: (in_tile(t) // n_seq, 0, 0)),
            pl.BlockSpec((1, BLOCK, KV_W), lambda t: (in_tile(t) // n_seq, 0, 0)),
            pl.BlockSpec((1, CONV_K - 1, CONV_W), lambda t: (in_tile(t) // n_seq, 0, 0)),
        ],
        out_shape=[
            jax.ShapeDtypeStruct((batch, seq, D_MODEL), F32),
            jax.ShapeDtypeStruct((batch, BLOCK, KV_W), F32),
            jax.ShapeDtypeStruct((batch, BLOCK, KV_W), F32),
            jax.ShapeDtypeStruct((batch, CONV_K - 1, CONV_W), F32),
        ],
        scratch_shapes=[
            pltpu.VMEM((tile + BLOCK, KV_W), BF),
            pltpu.VMEM((KV_W, tile + BLOCK), BF),
            pltpu.VMEM((tile + SUBLANES, CONV_W), F32),
            pltpu.VMEM((tile, D_MODEL), BF),
            pltpu.VMEM((tile, D_MODEL), BF),
            pltpu.VMEM((4, tile, CONV_W), F32),
            pltpu.VMEM((tile, ATTN_W), BF),
            pltpu.VMEM((tile, ATTN_W), F32),
            pltpu.VMEM((tile, D_MODEL), F32),
            pltpu.VMEM((tile, D_MODEL), BF),
        ],
        compiler_params=pltpu.CompilerParams(dimension_semantics=("arbitrary",), vmem_limit_bytes=VMEM_LIMIT),
    )(sinks[i], x, x, p_all, cos, sin_s, g_norm, w_in, conv_w, w_out, w_pg, w_pp, g_final)


def _rope_t(x, cos_t, sin_t):
    parts = [x[r:r + ROPE_HALF] for r in range(0, x.shape[0], ROPE_HALF)]
    swapped = jnp.concatenate([parts[n ^ 1] for n in range(len(parts))], axis=0)
    return x * cos_t + swapped * sin_t


def _sample_kernel(sink_ref, x_ref, p_ref, cos_ref, sin_ref, cost_ref, sint_ref, st0_ref, st1_ref, gn_ref, win_ref,
                   cw_ref, wout_ref, wpg_ref, wpp_ref, gfin_ref, ck_ref, cv_ref,
                   xo_ref, nc0_ref, nc1_ref, nk_ref, nv_ref,
                   xs_buf, q_s, kt_s, vt_s, sga_s, mix_s, ubuf, p1_buf, p2_buf, *, rows, t_dec, chunk, nb):
    layer = pl.program_id(0)
    step = pl.program_id(1)
    last_layer = pl.num_programs(0) - 1
    last_step = pl.num_programs(1) - 1
    dec_batch = rows // t_dec
    gr = nb * t_dec
    cols = chunk * t_dec

    @pl.when((layer == 0) & (step == 0))
    def _():
        xs_buf[...] = x_ref[...]

    @pl.when(step == 0)
    def _():
        xn = _rmsnorm(xs_buf[...], gn_ref[...]).astype(BF)
        q_s[...] = (_rope(_mm(xn, win_ref[:, C_Q:C_K]), cos_ref[...], sin_ref[...]) * SCALE).astype(BF)
        kvt = lax.dot_general(win_ref[:, C_K:C_GA], xn, (((0,), (1,)), ((), ())),
                              preferred_element_type=F32)
        kt_s[...] = _rope_t(kvt[0:KV_W], cost_ref[...], sint_ref[...])
        vt_s[...] = kvt[KV_W:2 * KV_W]
        sga_s[...] = _silu(_mm(xn, win_ref[:, C_GA:C_BG]))
        u = _mm(xn, win_ref[:, C_CG:C_HC]) * _mm(xn, win_ref[:, C_HC:C_GC])
        per_batch = lambda first: pl.ds(first, dec_batch, stride=t_dec)
        for c in range(CONV_W // LANES):
            cs = slice(c * LANES, (c + 1) * LANES)
            ubuf[c, 0:SUBLANES, :] = jnp.zeros((SUBLANES, LANES), F32)
            ubuf[c, SUBLANES:SUBLANES + rows, :] = u[:, cs]
            p1_buf[c] = ubuf[c, SUBLANES - 1:SUBLANES - 1 + rows, :]
            p2_buf[c] = ubuf[c, SUBLANES - 2:SUBLANES - 2 + rows, :]
            p1_buf[c, per_batch(0), :] = st1_ref[:, cs]
            p2_buf[c, per_batch(0), :] = st0_ref[:, cs]
            p2_buf[c, per_batch(1), :] = st1_ref[:, cs]
            nc0_ref[:, cs] = ubuf[c, per_batch(SUBLANES + t_dec - 2), :]
            nc1_ref[:, cs] = ubuf[c, per_batch(SUBLANES + t_dec - 1), :]
        u1 = jnp.concatenate([p1_buf[c] for c in range(CONV_W // LANES)], axis=1)
        u2 = jnp.concatenate([p2_buf[c] for c in range(CONV_W // LANES)], axis=1)
        bg = _mm(xn, win_ref[:, C_BG:C_CG])
        gc = _mm(xn, win_ref[:, C_GC:IN_W])
        mix_s[:, ATTN_W:D_MODEL] = _conv_gate(u, u1, u2, cw_ref, bg, gc).astype(BF)

    c0 = pl.multiple_of(step * cols, cols)
    kt_new = kt_s[:, pl.ds(c0, cols)]
    vt_new = vt_s[:, pl.ds(c0, cols)]

    qrow = lax.broadcasted_iota(jnp.int32, (gr, nb * WINDOW), 0)
    ccol = lax.broadcasted_iota(jnp.int32, (gr, nb * WINDOW), 1)
    valid_c = ((ccol // WINDOW) == (qrow // t_dec)) & ((ccol % WINDOW) > (qrow % t_dec))
    qrow_n = lax.broadcasted_iota(jnp.int32, (gr, gr), 0)
    ncol = lax.broadcasted_iota(jnp.int32, (gr, gr), 1)
    valid_n = ((ncol // t_dec) == (qrow_n // t_dec)) & ((ncol % t_dec) <= (qrow_n % t_dec))
    lane = lax.broadcasted_iota(jnp.int32, (KV_W, WINDOW), 1)
    kept = lane < WINDOW - t_dec

    def update_window(b):
        place = (WINDOW - t_dec - t_dec * b) % cols
        k_in = pltpu.roll(kt_new, place, 1) if place else kt_new
        v_in = pltpu.roll(vt_new, place, 1) if place else vt_new
        nk_ref[b] = jnp.where(kept, pltpu.roll(ck_ref[b], WINDOW - t_dec, 1), k_in)
        nv_ref[b] = jnp.where(kept, pltpu.roll(cv_ref[b], WINDOW - t_dec, 1), v_in)

    def rows_of(sg):
        return pl.ds(pl.multiple_of(step * cols + sg * gr, gr), gr)

    def scores(n):
        sg, g = divmod(n, N_KV_HEADS)
        hd = slice(g * HEAD_DIM, (g + 1) * HEAD_DIM)
        qg = q_s[rows_of(sg), :]
        ktc = jnp.concatenate([ck_ref[sg * nb + b, hd, :] for b in range(nb)], axis=1).astype(BF)
        ktn = kt_new[hd, sg * gr:(sg + 1) * gr].astype(BF)
        qst = jnp.concatenate(
            [qg[:, (GROUP * g + h) * HEAD_DIM:(GROUP * g + h + 1) * HEAD_DIM] for h in range(GROUP)], axis=0)
        return _mm(qst, ktc), _mm(qst, ktn)

    def softmax(n, sc, sn):
        g = n % N_KV_HEADS
        ecs, ens, invs = [], [], []
        for h in range(GROUP):
            scm = jnp.where(valid_c, sc[h * gr:(h + 1) * gr], NEG)
            snm = jnp.where(valid_n, sn[h * gr:(h + 1) * gr], NEG)
            sink = sink_ref[layer, GROUP * g + h]
            m = jnp.maximum(jnp.maximum(jnp.max(scm, axis=-1, keepdims=True),
                                        jnp.max(snm, axis=-1, keepdims=True)), sink)
            ec = jnp.exp(scm - m)
            en = jnp.exp(snm - m)
            den = (jnp.sum(ec, axis=-1, keepdims=True) + jnp.sum(en, axis=-1, keepdims=True)
                   + jnp.exp(sink - m))
            invs.append(1.0 / den)
            ecs.append(ec.astype(BF))
            ens.append(en.astype(BF))
        return jnp.concatenate(ecs, axis=0), jnp.concatenate(ens, axis=0), invs

    heads = {}

    def values(n, ec, en, invs):
        sg, g = divmod(n, N_KV_HEADS)
        hd = slice(g * HEAD_DIM, (g + 1) * HEAD_DIM)
        vtc = jnp.concatenate([cv_ref[sg * nb + b, hd, :] for b in range(nb)], axis=1).astype(BF)
        vtn = vt_new[hd, sg * gr:(sg + 1) * gr].astype(BF)
        o = (lax.dot_general(ec, vtc, _NT, preferred_element_type=F32)
             + lax.dot_general(en, vtn, _NT, preferred_element_type=F32))
        heads.setdefault(sg, []).extend(o[h * gr:(h + 1) * gr] * invs[h] for h in range(GROUP))
        if g == N_KV_HEADS - 1:
            oa = jnp.concatenate(heads.pop(sg), axis=1)
            mix_s[rows_of(sg), 0:ATTN_W] = (oa * sga_s[rows_of(sg), :]).astype(BF)
            for b in range(sg * nb, (sg + 1) * nb):
                update_window(b)

    for n in range((chunk // nb) * N_KV_HEADS):
        values(n, *softmax(n, *scores(n)))

    @pl.when(step == last_step)
    def _():
        x1 = xs_buf[...] + _mm(mix_s[...], wout_ref[...])
        gate = _sigmoid(_mm(x1.astype(BF), wpg_ref[...]))
        x2 = x1 + gate * _mm(p_ref[...].astype(BF), wpp_ref[...])
        xs_buf[...] = x2

        @pl.when(layer == last_layer)
        def _():
            xo_ref[...] = _rmsnorm(x2, gfin_ref[...])


def _sample_trunk(x, p_all, cos, sin_s, cos_t, sin_t, st0, st1, sinks, g_norm, w_in, conv_w, w_out, w_pg, w_pp,
                  g_final, cache_kt, cache_vt, t_dec):
    rows = x.shape[0]
    dec_batch = rows // t_dec
    chunk = LANES // t_dec
    nb = SAMPLE_GROUP
    const = lambda l, s: (0, 0)
    layer = lambda l, s: (l, 0, 0)
    once = dict(pipeline_mode=pl.Buffered(1))
    full = lambda shape: pl.BlockSpec(shape, const, **once)
    per_layer = lambda *shape: pl.BlockSpec((None,) + shape, layer)
    cache = pl.BlockSpec((None, chunk, KV_W, WINDOW), lambda l, s: (l, s, 0, 0))
    return pl.pallas_call(
        functools.partial(_sample_kernel, rows=rows, t_dec=t_dec, chunk=chunk, nb=nb),
        name="sample_trunk",
        grid=(DEPTH, dec_batch // chunk),
        in_specs=[
            pl.BlockSpec(memory_space=pltpu.SMEM),
            full((rows, D_MODEL)),
            per_layer(rows, PLE_DIM),
            full((rows, LANES)),
            full((rows, LANES)),
            full((KV_W, rows)),
            full((KV_W, rows)),
            per_layer(dec_batch, CONV_W),
            per_layer(dec_batch, CONV_W),
            per_layer(1, D_MODEL),
            per_layer(D_MODEL, IN_W),
            per_layer(CONV_K, CONV_W),
            per_layer(D_MODEL, D_MODEL),
            per_layer(D_MODEL, D_MODEL),
            per_layer(PLE_DIM, D_MODEL),
            full((1, D_MODEL)),
            cache,
            cache,
        ],
        out_specs=[
            pl.BlockSpec((rows, D_MODEL), const),
            pl.BlockSpec((None, dec_batch, CONV_W), layer),
            pl.BlockSpec((None, dec_batch, CONV_W), layer),
            cache,
            cache,
        ],
        out_shape=[
            jax.ShapeDtypeStruct((rows, D_MODEL), F32),
            jax.ShapeDtypeStruct((DEPTH, dec_batch, CONV_W), F32),
            jax.ShapeDtypeStruct((DEPTH, dec_batch, CONV_W), F32),
            jax.ShapeDtypeStruct((DEPTH, dec_batch, KV_W, WINDOW), F32),
            jax.ShapeDtypeStruct((DEPTH, dec_batch, KV_W, WINDOW), F32),
        ],
        scratch_shapes=[
            pltpu.VMEM((rows, D_MODEL), F32),
            pltpu.VMEM((rows, ATTN_W), BF),
            pltpu.VMEM((KV_W, rows), F32),
            pltpu.VMEM((KV_W, rows), F32),
            pltpu.VMEM((rows, ATTN_W), F32),
            pltpu.VMEM((rows, D_MODEL), BF),
            pltpu.VMEM((CONV_W // LANES, rows + SUBLANES, LANES), F32),
            pltpu.VMEM((CONV_W // LANES, rows, LANES), F32),
            pltpu.VMEM((CONV_W // LANES, rows, LANES), F32),
        ],
        compiler_params=pltpu.CompilerParams(
            dimension_semantics=("arbitrary", "arbitrary"), vmem_limit_bytes=TRUNK_VMEM_LIMIT),
    )(sinks, x, p_all, cos, sin_s, cos_t, sin_t, st0, st1, g_norm, w_in, conv_w, w_out, w_pg, w_pp, g_final,
      cache_kt, cache_vt)


def _rope_angles(pos):
    inv = ROPE_THETA ** (-jnp.arange(0, HEAD_DIM, 2, dtype=F32) / HEAD_DIM)
    return pos[:, None] * inv[None, :]


def _rope_tables(pos):
    ang = _rope_angles(pos)
    cos = jnp.tile(jnp.cos(ang), (1, 2 * LANES // HEAD_DIM))
    sin = jnp.sin(ang)
    sin_signed = jnp.tile(jnp.concatenate([-sin, sin], axis=-1), (1, LANES // HEAD_DIM))
    return cos, sin_signed


def _rope_tables_t(pos):
    ang = _rope_angles(pos).T
    cos = jnp.tile(jnp.cos(ang), (2 * KV_W // HEAD_DIM, 1))
    sin = jnp.sin(ang)
    sin_signed = jnp.tile(jnp.concatenate([-sin, sin], axis=0), (KV_W // HEAD_DIM, 1))
    return cos, sin_signed


def kernel(x_prompt, x_sample, cache_k, cache_v, state_conv, p_prompt, p_sample, g_norm, w_in, sinks, conv_w,
           w_out, w_pg, w_pp, g_final):
    batch, seq, _ = x_prompt.shape
    dec_batch, t_dec, _ = x_sample.shape
    rows = dec_batch * t_dec

    cos_p, sin_p = _rope_tables(jnp.arange(seq, dtype=F32))
    pos_s = jnp.tile(PAST_LEN + jnp.arange(t_dec, dtype=F32), dec_batch)
    cos_s, sin_s = _rope_tables(pos_s)
    cos_st, sin_st = _rope_tables_t(pos_s)

    w_in_b, w_out_b, w_pg_b, w_pp_b = (w.astype(BF) for w in (w_in, w_out, w_pg, w_pp))
    g_norm3 = g_norm.reshape(DEPTH, 1, D_MODEL)
    g_fin2 = g_final.reshape(1, D_MODEL)
    to_t = lambda c: jnp.transpose(c, (0, 1, 3, 4, 2)).reshape(DEPTH, -1, KV_W, WINDOW)
    from_t = lambda c: jnp.transpose(c.reshape(DEPTH, -1, N_KV_HEADS, HEAD_DIM, WINDOW), (0, 1, 4, 2, 3))

    xs, nc0, nc1, nks, nvs = _sample_trunk(
        x_sample.reshape(rows, D_MODEL), p_sample.reshape(DEPTH, rows, PLE_DIM), cos_s, sin_s, cos_st, sin_st,
        state_conv[:, :, 0], state_conv[:, :, 1], sinks, g_norm3, w_in_b, conv_w, w_out_b, w_pg_b, w_pp_b,
        g_fin2, to_t(cache_k), to_t(cache_v), t_dec)

    xp = x_prompt
    nkp, nvp, ncp = [], [], []
    for i in range(DEPTH):
        xp, kt, vt, ut = _prompt_layer(i, xp, p_prompt, cos_p, sin_p, sinks, g_norm3, w_in_b, conv_w, w_out_b,
                                       w_pg_b, w_pp_b, g_fin2, i == DEPTH - 1)
        nkp.append(kt)
        nvp.append(vt)
        ncp.append(ut)

    return (xp, xs.reshape(dec_batch, t_dec, D_MODEL), from_t(jnp.stack(nkp)), from_t(jnp.stack(nvp)),
            jnp.stack(ncp), from_t(nks), from_t(nvs), jnp.stack([nc0, nc1], axis=2))
```

```python
import functools

import jax
import jax.numpy as jnp
from jax import lax
from jax.experimental import pallas as pl
from jax.experimental.pallas import tpu as pltpu

D_MODEL = 1024
DEPTH = 4
N_HEADS = 8
HEAD_DIM = 64
N_KV_HEADS = 2
GROUP = N_HEADS // N_KV_HEADS
ATTN_W = N_HEADS * HEAD_DIM
KV_W = N_KV_HEADS * HEAD_DIM
CONV_W = D_MODEL - ATTN_W
CONV_K = 3
WINDOW = 128
BLOCK = 128
PLE_DIM = 256
PAST_LEN = 8192
ROPE_THETA = 10000.0
EPS = 1e-6
SCALE = HEAD_DIM ** -0.5
NEG = -1e30
IN_W = 2 * ATTN_W + 2 * KV_W + 4 * CONV_W

C_Q = 0
C_K = ATTN_W
C_V = C_K + KV_W
C_GA = C_V + KV_W
C_BG = C_GA + ATTN_W
C_CG = C_BG + CONV_W
C_HC = C_CG + CONV_W
C_GC = C_HC + CONV_W

LANES = 128
SUBLANES = 8
ROPE_HALF = HEAD_DIM // 2

MXU_N = 256
PROMPT_TILE = 512
SAMPLE_GROUP = 16
VMEM_LIMIT = 56 * 1024 * 1024
TRUNK_VMEM_LIMIT = 60 * 1024 * 1024

BF = jnp.bfloat16
F32 = jnp.float32
_NT = (((1,), (1,)), ((), ()))


def _mm(a, w):
    return jnp.dot(a, w, preferred_element_type=F32)


def _sigmoid(x):
    return 1.0 / (1.0 + jnp.exp(-x))


def _silu(x):
    return x * _sigmoid(x)


def _rmsnorm(x, g):
    r = lax.rsqrt(jnp.mean(x * x, axis=-1, keepdims=True) + EPS)
    return (x * r) * g


def _rope(x, cos, sin_signed):
    rows = x.shape[0]
    lane = lax.broadcasted_iota(jnp.int32, (rows, LANES), 1)
    first_half = (lane % HEAD_DIM) < ROPE_HALF
    outs = []
    for c in range(x.shape[1] // LANES):
        xc = x[:, c * LANES:(c + 1) * LANES]
        rot = jnp.where(first_half, pltpu.roll(xc, LANES - ROPE_HALF, 1), pltpu.roll(xc, ROPE_HALF, 1))
        outs.append(xc * cos + rot * sin_signed)
    return outs[0] if len(outs) == 1 else jnp.concatenate(outs, axis=1)


def _conv_gate(u, u1, u2, cw_ref, bg, gc):
    yc = cw_ref[0:1, :] * u2 + cw_ref[1:2, :] * u1 + cw_ref[2:3, :] * u
    return bg * yc * _silu(gc)


def _prompt_kernel(sink_ref, xa_ref, xb_ref, p_ref, cos_ref, sin_ref, gn_ref, win_ref, cw_ref, wout_ref, wpg_ref,
                   wpp_ref, gfin_ref, xo_ref, kt_ref, vt_ref, ut_ref, kbuf, vbuf, ubuf, mix_buf, xn_buf, zc_buf,
                   q_buf, sga_buf, x1_buf, x1b_buf, *, tile, n_seq, final):
    i = pl.program_id(0)
    n_tiles = pl.num_programs(0) - 1
    first_of_seq = jnp.minimum(i, n_tiles - 1) % n_seq == 0

    @pl.when(i == 0)
    def _():
        mix_buf[...] = jnp.zeros(mix_buf.shape, BF)

    @pl.when(first_of_seq)
    def _():
        kbuf[0:BLOCK, :] = jnp.zeros((BLOCK, KV_W), BF)
        vbuf[:, 0:BLOCK] = jnp.zeros((KV_W, BLOCK), BF)
        ubuf[0:SUBLANES, :] = jnp.zeros((SUBLANES, CONV_W), F32)

    n_out = D_MODEL // MXU_N
    rows_per = tile // n_out
    for c in range(n_out):
        cs = slice(c * MXU_N, (c + 1) * MXU_N)
        x1 = (xb_ref[0, :, cs] + _mm(mix_buf[:, 0:ATTN_W], wout_ref[0:ATTN_W, cs])
              + _mm(mix_buf[:, ATTN_W:D_MODEL], wout_ref[ATTN_W:D_MODEL, cs]))
        x1_buf[:, cs] = x1
        x1b_buf[:, cs] = x1.astype(BF)
        rs = slice(c * rows_per, (c + 1) * rows_per)
        xn_buf[rs, :] = _rmsnorm(xa_ref[0, rs, :], gn_ref[...]).astype(BF)
    pb = p_ref[0].astype(BF)

    def gate_unit(c):
        cs = slice(c * MXU_N, (c + 1) * MXU_N)
        gate = _sigmoid(_mm(x1b_buf[...], wpg_ref[:, cs]))
        xo_ref[0, :, cs] = x1_buf[:, cs] + gate * _mm(pb, wpp_ref[:, cs])

    cos = cos_ref[...]
    sin_s = sin_ref[...]

    def proj(c0, width=MXU_N):
        return _mm(xn_buf[...], win_ref[:, c0:c0 + width])

    kv = proj(C_K, 2 * KV_W)
    k = _rope(kv[:, 0:KV_W], cos, sin_s)
    kbuf[BLOCK:BLOCK + tile, :] = k.astype(BF)
    vt = kv[:, KV_W:2 * KV_W].T
    vbuf[:, BLOCK:BLOCK + tile] = vt.astype(BF)
    kt_ref[0] = k[tile - BLOCK:, :].T
    vt_ref[0] = vt[:, tile - BLOCK:]
    for c in range(ATTN_W // MXU_N):
        cs = slice(c * MXU_N, (c + 1) * MXU_N)
        q_buf[:, cs] = (_rope(proj(C_Q + c * MXU_N), cos, sin_s) * SCALE).astype(BF)

    conv_cols = (C_CG, C_HC, C_BG, C_GC)
    n_planes = len(conv_cols)
    n_slabs = CONV_W // MXU_N
    n_gate = ATTN_W // MXU_N
    n_dense = n_gate + n_slabs * n_planes

    def dense_unit(n):
        if n < n_gate:
            sga_buf[:, n * MXU_N:(n + 1) * MXU_N] = _silu(proj(C_GA + n * MXU_N))
        else:
            half, plane = divmod(n - n_gate, n_planes)
            zc_buf[plane, :, half * MXU_N:(half + 1) * MXU_N] = proj(conv_cols[plane] + half * MXU_N)

    def conv_unit(half):
        cs = slice(half * MXU_N, (half + 1) * MXU_N)
        u = zc_buf[0, :, cs] * zc_buf[1, :, cs]
        ubuf[SUBLANES:SUBLANES + tile, cs] = u
        u1 = ubuf[SUBLANES - 1:SUBLANES - 1 + tile, cs]
        u2 = ubuf[SUBLANES - 2:SUBLANES - 2 + tile, cs]
        yc = cw_ref[0:1, cs] * u2 + cw_ref[1:2, cs] * u1 + cw_ref[2:3, cs] * u
        mix_buf[:, ATTN_W + half * MXU_N:ATTN_W + (half + 1) * MXU_N] = (
            zc_buf[2, :, cs] * yc * _silu(zc_buf[3, :, cs])).astype(BF)
        ubuf[0:SUBLANES, cs] = ubuf[tile:tile + SUBLANES, cs]
        ut_ref[0, :, cs] = u[tile - (CONV_K - 1):, :]

    row = lax.broadcasted_iota(jnp.int32, (BLOCK, BLOCK), 0)
    col = lax.broadcasted_iota(jnp.int32, (BLOCK, BLOCK), 1)
    own = row <= col
    no_prev = jnp.where(first_of_seq, NEG, 0.0).astype(F32)

    def scores(n):
        j, g = divmod(n, N_KV_HEADS)
        kcat = kbuf[j * BLOCK:(j + 2) * BLOCK, g * HEAD_DIM:(g + 1) * HEAD_DIM]
        qst = jnp.concatenate(
            [q_buf[j * BLOCK:(j + 1) * BLOCK, (GROUP * g + h) * HEAD_DIM:(GROUP * g + h + 1) * HEAD_DIM]
             for h in range(GROUP)], axis=0)
        return lax.dot_general(kcat, qst, _NT, preferred_element_type=F32)

    def softmax(n, sc):
        j, g = divmod(n, N_KV_HEADS)
        probs, invs = [], []
        for h in range(GROUP):
            sh = sc[:, h * BLOCK:(h + 1) * BLOCK]
            s_prev = sh[:BLOCK]
            if j == 0:
                s_prev = s_prev + no_prev
            sm = jnp.where(own, sh[BLOCK:], s_prev)
            sink = sink_ref[GROUP * g + h]
            m = jnp.maximum(jnp.max(sm, axis=0, keepdims=True), sink)
            e = jnp.exp(sm - m)
            den = jnp.sum(e, axis=0, keepdims=True) + jnp.exp(sink - m)
            invs.append(1.0 / den)
            eb = e.astype(BF)
            zero = jnp.zeros_like(eb)
            probs.append(jnp.concatenate([jnp.where(own, zero, eb), jnp.where(own, eb, zero)], axis=0))
        return jnp.concatenate(probs, axis=1), jnp.concatenate(invs, axis=1)

    heads = {}

    def values(n, probs, inv):
        j, g = divmod(n, N_KV_HEADS)
        vcat = vbuf[g * HEAD_DIM:(g + 1) * HEAD_DIM, j * BLOCK:(j + 2) * BLOCK]
        o = _mm(vcat, probs) * inv
        heads.setdefault(j, []).extend(o[:, h * BLOCK:(h + 1) * BLOCK] for h in range(GROUP))
        if g == N_KV_HEADS - 1:
            rows = slice(j * BLOCK, (j + 1) * BLOCK)
            oa = jnp.concatenate(heads.pop(j), axis=0).T
            mix_buf[rows, 0:ATTN_W] = (oa * sga_buf[rows, :]).astype(BF)

    n_units = (tile // BLOCK) * N_KV_HEADS
    conv_at = {n_gate + (h + 1) * n_planes: h for h in range(n_slabs)}
    n_steps = max(n_units + 2, max(conv_at) + 1)
    gate_at = {2 * c + 1: c for c in range(n_out)}
    sc_q, sm_q = {}, {}
    for t in range(n_steps):
        if t < n_units:
            sc_q[t] = scores(t)
        if t < n_dense:
            dense_unit(t)
        if 0 <= t - 2 < n_units:
            values(t - 2, *sm_q.pop(t - 2))
        if t in gate_at:
            gate_unit(gate_at[t])
        if 0 <= t - 1 < n_units:
            sm_q[t - 1] = softmax(t - 1, sc_q.pop(t - 1))
        if t in conv_at:
            conv_unit(conv_at[t])

    kbuf[0:BLOCK, :] = kbuf[tile:tile + BLOCK, :]
    vbuf[:, 0:BLOCK] = vbuf[:, tile:tile + BLOCK]

    if final:
        xo_ref[0] = _rmsnorm(xo_ref[0], gfin_ref[...])


def _prompt_layer(i, x, p_all, cos, sin_s, sinks, g_norm, w_in, conv_w, w_out, w_pg, w_pp, g_final, final):
    batch, seq, _ = x.shape
    tile = PROMPT_TILE
    n_seq = seq // tile
    n_tiles = batch * n_seq
    in_tile = lambda t: jnp.minimum(t, n_tiles - 1)
    out_tile = lambda t: jnp.maximum(t - 1, 0)
    const = lambda t: (0, 0)
    layer = lambda t: (i, 0, 0)
    resident = dict(pipeline_mode=pl.Buffered(1))
    return pl.pallas_call(
        functools.partial(_prompt_kernel, tile=tile, n_seq=n_seq, final=final),
        name=f"prompt_layer{i}",
        grid=(n_tiles + 1,),
        in_specs=[
            pl.BlockSpec(memory_space=pltpu.SMEM),
            pl.BlockSpec((1, tile, D_MODEL), lambda t: (in_tile(t) // n_seq, in_tile(t) % n_seq, 0)),
            pl.BlockSpec((1, tile, D_MODEL), lambda t: (out_tile(t) // n_seq, out_tile(t) % n_seq, 0)),
            pl.BlockSpec((None, 1, tile, PLE_DIM),
                         lambda t: (i, out_tile(t) // n_seq, out_tile(t) % n_seq, 0)),
            pl.BlockSpec((tile, LANES), lambda t: (in_tile(t) % n_seq, 0)),
            pl.BlockSpec((tile, LANES), lambda t: (in_tile(t) % n_seq, 0)),
            pl.BlockSpec((None, 1, D_MODEL), layer, **resident),
            pl.BlockSpec((None, D_MODEL, IN_W), layer, **resident),
            pl.BlockSpec((None, CONV_K, CONV_W), layer, **resident),
            pl.BlockSpec((None, D_MODEL, D_MODEL), layer, **resident),
            pl.BlockSpec((None, D_MODEL, D_MODEL), layer, **resident),
            pl.BlockSpec((None, PLE_DIM, D_MODEL), layer, **resident),
            pl.BlockSpec((1, D_MODEL), const, **resident),
        ],
        out_specs=[
            pl.BlockSpec((1, tile, D_MODEL), lambda t: (out_tile(t) // n_seq, out_tile(t) % n_seq, 0)),
            pl.BlockSpec((1, BLOCK, KV_W), lambda t: (in_tile(t) // n_seq, 0, 0)),
            pl.BlockSpec((1, BLOCK, KV_W), lambda t: (in_tile(t) // n_seq, 0, 0)),
            pl.BlockSpec((1, CONV_K - 1, CONV_W), lambda t: (in_tile(t) // n_seq, 0, 0)),
        ],
        out_shape=[
            jax.ShapeDtypeStruct((batch, seq, D_MODEL), F32),
            jax.ShapeDtypeStruct((batch, BLOCK, KV_W), F32),
            jax.ShapeDtypeStruct((batch, BLOCK, KV_W), F32),
            jax.ShapeDtypeStruct((batch, CONV_K - 1, CONV_W), F32),
        ],
        scratch_shapes=[
            pltpu.VMEM((tile + BLOCK, KV_W), BF),
            pltpu.VMEM((KV_W, tile + BLOCK), BF),
            pltpu.VMEM((tile + SUBLANES, CONV_W), F32),
            pltpu.VMEM((tile, D_MODEL), BF),
            pltpu.VMEM((tile, D_MODEL), BF),
            pltpu.VMEM((4, tile, CONV_W), F32),
            pltpu.VMEM((tile, ATTN_W), BF),
            pltpu.VMEM((tile, ATTN_W), F32),
            pltpu.VMEM((tile, D_MODEL), F32),
            pltpu.VMEM((tile, D_MODEL), BF),
        ],
        compiler_params=pltpu.CompilerParams(dimension_semantics=("arbitrary",), vmem_limit_bytes=VMEM_LIMIT),
    )(sinks[i], x, x, p_all, cos, sin_s, g_norm, w_in, conv_w, w_out, w_pg, w_pp, g_final)


def _rope_t(x, cos_t, sin_t):
    parts = [x[r:r + ROPE_HALF] for r in range(0, x.shape[0], ROPE_HALF)]
    swapped = jnp.concatenate([parts[n ^ 1] for n in range(len(parts))], axis=0)
    return x * cos_t + swapped * sin_t


def _sample_kernel(sink_ref, x_ref, p_ref, cos_ref, sin_ref, cost_ref, sint_ref, st0_ref, st1_ref, gn_ref, win_ref,
                   cw_ref, wout_ref, wpg_ref, wpp_ref, gfin_ref, ck_ref, cv_ref,
                   xo_ref, nc0_ref, nc1_ref, nk_ref, nv_ref,
                   xs_buf, q_s, kt_s, vt_s, sga_s, mix_s, ubuf, p1_buf, p2_buf, *, rows, t_dec, chunk, nb):
    layer = pl.program_id(0)
    step = pl.program_id(1)
    last_layer = pl.num_programs(0) - 1
    last_step = pl.num_programs(1) - 1
    dec_batch = rows // t_dec
    gr = nb * t_dec
    cols = chunk * t_dec

    @pl.when((layer == 0) & (step == 0))
    def _():
        xs_buf[...] = x_ref[...]

    @pl.when(step == 0)
    def _():
        xn = _rmsnorm(xs_buf[...], gn_ref[...]).astype(BF)
        q_s[...] = (_rope(_mm(xn, win_ref[:, C_Q:C_K]), cos_ref[...], sin_ref[...]) * SCALE).astype(BF)
        kvt = lax.dot_general(win_ref[:, C_K:C_GA], xn, (((0,), (1,)), ((), ())),
                              preferred_element_type=F32)
        kt_s[...] = _rope_t(kvt[0:KV_W], cost_ref[...], sint_ref[...])
        vt_s[...] = kvt[KV_W:2 * KV_W]
        sga_s[...] = _silu(_mm(xn, win_ref[:, C_GA:C_BG]))
        u = _mm(xn, win_ref[:, C_CG:C_HC]) * _mm(xn, win_ref[:, C_HC:C_GC])
        per_batch = lambda first: pl.ds(first, dec_batch, stride=t_dec)
        for c in range(CONV_W // LANES):
            cs = slice(c * LANES, (c + 1) * LANES)
            ubuf[c, 0:SUBLANES, :] = jnp.zeros((SUBLANES, LANES), F32)
            ubuf[c, SUBLANES:SUBLANES + rows, :] = u[:, cs]
            p1_buf[c] = ubuf[c, SUBLANES - 1:SUBLANES - 1 + rows, :]
            p2_buf[c] = ubuf[c, SUBLANES - 2:SUBLANES - 2 + rows, :]
            p1_buf[c, per_batch(0), :] = st1_ref[:, cs]
            p2_buf[c, per_batch(0), :] = st0_ref[:, cs]
            p2_buf[c, per_batch(1), :] = st1_ref[:, cs]
            nc0_ref[:, cs] = ubuf[c, per_batch(SUBLANES + t_dec - 2), :]
            nc1_ref[:, cs] = ubuf[c, per_batch(SUBLANES + t_dec - 1), :]
        u1 = jnp.concatenate([p1_buf[c] for c in range(CONV_W // LANES)], axis=1)
        u2 = jnp.concatenate([p2_buf[c] for c in range(CONV_W // LANES)], axis=1)
        bg = _mm(xn, win_ref[:, C_BG:C_CG])
        gc = _mm(xn, win_ref[:, C_GC:IN_W])
        mix_s[:, ATTN_W:D_MODEL] = _conv_gate(u, u1, u2, cw_ref, bg, gc).astype(BF)

    c0 = pl.multiple_of(step * cols, cols)
    kt_new = kt_s[:, pl.ds(c0, cols)]
    vt_new = vt_s[:, pl.ds(c0, cols)]

    qrow = lax.broadcasted_iota(jnp.int32, (gr, nb * WINDOW), 0)
    ccol = lax.broadcasted_iota(jnp.int32, (gr, nb * WINDOW), 1)
    valid_c = ((ccol // WINDOW) == (qrow // t_dec)) & ((ccol % WINDOW) > (qrow % t_dec))
    qrow_n = lax.broadcasted_iota(jnp.int32, (gr, gr), 0)
    ncol = lax.broadcasted_iota(jnp.int32, (gr, gr), 1)
    valid_n = ((ncol // t_dec) == (qrow_n // t_dec)) & ((ncol % t_dec) <= (qrow_n % t_dec))
    lane = lax.broadcasted_iota(jnp.int32, (KV_W, WINDOW), 1)
    kept = lane < WINDOW - t_dec

    def update_window(b):
        place = (WINDOW - t_dec - t_dec * b) % cols
        k_in = pltpu.roll(kt_new, place, 1) if place else kt_new
        v_in = pltpu.roll(vt_new, place, 1) if place else vt_new
        nk_ref[b] = jnp.where(kept, pltpu.roll(ck_ref[b], WINDOW - t_dec, 1), k_in)
        nv_ref[b] = jnp.where(kept, pltpu.roll(cv_ref[b], WINDOW - t_dec, 1), v_in)

    def rows_of(sg):
        return pl.ds(pl.multiple_of(step * cols + sg * gr, gr), gr)

    def scores(n):
        sg, g = divmod(n, N_KV_HEADS)
        hd = slice(g * HEAD_DIM, (g + 1) * HEAD_DIM)
        qg = q_s[rows_of(sg), :]
        ktc = jnp.concatenate([ck_ref[sg * nb + b, hd, :] for b in range(nb)], axis=1).astype(BF)
        ktn = kt_new[hd, sg * gr:(sg + 1) * gr].astype(BF)
        qst = jnp.concatenate(
            [qg[:, (GROUP * g + h) * HEAD_DIM:(GROUP * g + h + 1) * HEAD_DIM] for h in range(GROUP)], axis=0)
        return _mm(qst, ktc), _mm(qst, ktn)

    def softmax(n, sc, sn):
        g = n % N_KV_HEADS
        ecs, ens, invs = [], [], []
        for h in range(GROUP):
            scm = jnp.where(valid_c, sc[h * gr:(h + 1) * gr], NEG)
            snm = jnp.where(valid_n, sn[h * gr:(h + 1) * gr], NEG)
            sink = sink_ref[layer, GROUP * g + h]
            m = jnp.maximum(jnp.maximum(jnp.max(scm, axis=-1, keepdims=True),
                                        jnp.max(snm, axis=-1, keepdims=True)), sink)
            ec = jnp.exp(scm - m)
            en = jnp.exp(snm - m)
            den = (jnp.sum(ec, axis=-1, keepdims=True) + jnp.sum(en, axis=-1, keepdims=True)
                   + jnp.exp(sink - m))
            invs.append(1.0 / den)
            ecs.append(ec.astype(BF))
            ens.append(en.astype(BF))
        return jnp.concatenate(ecs, axis=0), jnp.concatenate(ens, axis=0), invs

    heads = {}

    def values(n, ec, en, invs):
        sg, g = divmod(n, N_KV_HEADS)
        hd = slice(g * HEAD_DIM, (g + 1) * HEAD_DIM)
        vtc = jnp.concatenate([cv_ref[sg * nb + b, hd, :] for b in range(nb)], axis=1).astype(BF)
        vtn = vt_new[hd, sg * gr:(sg + 1) * gr].astype(BF)
        o = (lax.dot_general(ec, vtc, _NT, preferred_element_type=F32)
             + lax.dot_general(en, vtn, _NT, preferred_element_type=F32))
        heads.setdefault(sg, []).extend(o[h * gr:(h + 1) * gr] * invs[h] for h in range(GROUP))
        if g == N_KV_HEADS - 1:
            oa = jnp.concatenate(heads.pop(sg), axis=1)
            mix_s[rows_of(sg), 0:ATTN_W] = (oa * sga_s[rows_of(sg), :]).astype(BF)
            for b in range(sg * nb, (sg + 1) * nb):
                update_window(b)

    for n in range((chunk // nb) * N_KV_HEADS):
        values(n, *softmax(n, *scores(n)))

    @pl.when(step == last_step)
    def _():
        x1 = xs_buf[...] + _mm(mix_s[...], wout_ref[...])
        gate = _sigmoid(_mm(x1.astype(BF), wpg_ref[...]))
        x2 = x1 + gate * _mm(p_ref[...].astype(BF), wpp_ref[...])
        xs_buf[...] = x2

        @pl.when(layer == last_layer)
        def _():
            xo_ref[...] = _rmsnorm(x2, gfin_ref[...])


def _sample_trunk(x, p_all, cos, sin_s, cos_t, sin_t, st0, st1, sinks, g_norm, w_in, conv_w, w_out, w_pg, w_pp,
                  g_final, cache_kt, cache_vt, t_dec):
    rows = x.shape[0]
    dec_batch = rows // t_dec
    chunk = LANES // t_dec
    nb = SAMPLE_GROUP
    const = lambda l, s: (0, 0)
    layer = lambda l, s: (l, 0, 0)
    once = dict(pipeline_mode=pl.Buffered(1))
    full = lambda shape: pl.BlockSpec(shape, const, **once)
    per_layer = lambda *shape: pl.BlockSpec((None,) + shape, layer)
    cache = pl.BlockSpec((None, chunk, KV_W, WINDOW), lambda l, s: (l, s, 0, 0))
    return pl.pallas_call(
        functools.partial(_sample_kernel, rows=rows, t_dec=t_dec, chunk=chunk, nb=nb),
        name="sample_trunk",
        grid=(DEPTH, dec_batch // chunk),
        in_specs=[
            pl.BlockSpec(memory_space=pltpu.SMEM),
            full((rows, D_MODEL)),
            per_layer(rows, PLE_DIM),
            full((rows, LANES)),
            full((rows, LANES)),
            full((KV_W, rows)),
            full((KV_W, rows)),
            per_layer(dec_batch, CONV_W),
            per_layer(dec_batch, CONV_W),
            per_layer(1, D_MODEL),
            per_layer(D_MODEL, IN_W),
            per_layer(CONV_K, CONV_W),
            per_layer(D_MODEL, D_MODEL),
            per_layer(D_MODEL, D_MODEL),
            per_layer(PLE_DIM, D_MODEL),
            full((1, D_MODEL)),
            cache,
            cache,
        ],
        out_specs=[
            pl.BlockSpec((rows, D_MODEL), const),
            pl.BlockSpec((None, dec_batch, CONV_W), layer),
            pl.BlockSpec((None, dec_batch, CONV_W), layer),
            cache,
            cache,
        ],
        out_shape=[
            jax.ShapeDtypeStruct((rows, D_MODEL), F32),
            jax.ShapeDtypeStruct((DEPTH, dec_batch, CONV_W), F32),
            jax.ShapeDtypeStruct((DEPTH, dec_batch, CONV_W), F32),
            jax.ShapeDtypeStruct((DEPTH, dec_batch, KV_W, WINDOW), F32),
            jax.ShapeDtypeStruct((DEPTH, dec_batch, KV_W, WINDOW), F32),
        ],
        scratch_shapes=[
            pltpu.VMEM((rows, D_MODEL), F32),
            pltpu.VMEM((rows, ATTN_W), BF),
            pltpu.VMEM((KV_W, rows), F32),
            pltpu.VMEM((KV_W, rows), F32),
            pltpu.VMEM((rows, ATTN_W), F32),
            pltpu.VMEM((rows, D_MODEL), BF),
            pltpu.VMEM((CONV_W // LANES, rows + SUBLANES, LANES), F32),
            pltpu.VMEM((CONV_W // LANES, rows, LANES), F32),
            pltpu.VMEM((CONV_W // LANES, rows, LANES), F32),
        ],
        compiler_params=pltpu.CompilerParams(
            dimension_semantics=("arbitrary", "arbitrary"), vmem_limit_bytes=TRUNK_VMEM_LIMIT),
    )(sinks, x, p_all, cos, sin_s, cos_t, sin_t, st0, st1, g_norm, w_in, conv_w, w_out, w_pg, w_pp, g_final,
      cache_kt, cache_vt)


def _rope_angles(pos):
    inv = ROPE_THETA ** (-jnp.arange(0, HEAD_DIM, 2, dtype=F32) / HEAD_DIM)
    return pos[:, None] * inv[None, :]


def _rope_tables(pos):
    ang = _rope_angles(pos)
    cos = jnp.tile(jnp.cos(ang), (1, 2 * LANES // HEAD_DIM))
    sin = jnp.sin(ang)
    sin_signed = jnp.tile(jnp.concatenate([-sin, sin], axis=-1), (1, LANES // HEAD_DIM))
    return cos, sin_signed


def _rope_tables_t(pos):
    ang = _rope_angles(pos).T
    cos = jnp.tile(jnp.cos(ang), (2 * KV_W // HEAD_DIM, 1))
    sin = jnp.sin(ang)
    sin_signed = jnp.tile(jnp.concatenate([-sin, sin], axis=0), (KV_W // HEAD_DIM, 1))
    return cos, sin_signed


def kernel(x_prompt, x_sample, cache_k, cache_v, state_conv, p_prompt, p_sample, g_norm, w_in, sinks, conv_w,
           w_out, w_pg, w_pp, g_final):
    batch, seq, _ = x_prompt.shape
    dec_batch, t_dec, _ = x_sample.shape
    rows = dec_batch * t_dec

    cos_p, sin_p = _rope_tables(jnp.arange(seq, dtype=F32))
    pos_s = jnp.tile(PAST_LEN + jnp.arange(t_dec, dtype=F32), dec_batch)
    cos_s, sin_s = _rope_tables(pos_s)
    cos_st, sin_st = _rope_tables_t(pos_s)

    w_in_b, w_out_b, w_pg_b, w_pp_b = (w.astype(BF) for w in (w_in, w_out, w_pg, w_pp))
    g_norm3 = g_norm.reshape(DEPTH, 1, D_MODEL)
    g_fin2 = g_final.reshape(1, D_MODEL)
    to_t = lambda c: jnp.transpose(c, (0, 1, 3, 4, 2)).reshape(DEPTH, -1, KV_W, WINDOW)
    from_t = lambda c: jnp.transpose(c.reshape(DEPTH, -1, N_KV_HEADS, HEAD_DIM, WINDOW), (0, 1, 4, 2, 3))

    xp = x_prompt
    nkp, nvp, ncp = [], [], []
    for i in range(DEPTH):
        xp, kt, vt, ut = _prompt_layer(i, xp, p_prompt, cos_p, sin_p, sinks, g_norm3, w_in_b, conv_w, w_out_b,
                                       w_pg_b, w_pp_b, g_fin2, i == DEPTH - 1)
        nkp.append(kt)
        nvp.append(vt)
        ncp.append(ut)

    xs, nc0, nc1, nks, nvs = _sample_trunk(
        x_sample.reshape(rows, D_MODEL), p_sample.reshape(DEPTH, rows, PLE_DIM), cos_s, sin_s, cos_st, sin_st,
        state_conv[:, :, 0], state_conv[:, :, 1], sinks, g_norm3, w_in_b, conv_w, w_out_b, w_pg_b, w_pp_b,
        g_fin2, to_t(cache_k), to_t(cache_v), t_dec)

    return (xp, xs.reshape(dec_batch, t_dec, D_MODEL), from_t(jnp.stack(nkp)), from_t(jnp.stack(nvp)),
            jnp.stack(ncp), from_t(nks), from_t(nvs), jnp.stack([nc0, nc1], axis=2))
```

```python
import functools

import jax
import jax.numpy as jnp
from jax import lax
from jax.experimental import pallas as pl
from jax.experimental.pallas import tpu as pltpu

D_MODEL = 1024
DEPTH = 4
N_HEADS = 8
HEAD_DIM = 64
N_KV_HEADS = 2
GROUP = N_HEADS // N_KV_HEADS
ATTN_W = N_HEADS * HEAD_DIM
KV_W = N_KV_HEADS * HEAD_DIM
CONV_W = D_MODEL - ATTN_W
CONV_K = 3
WINDOW = 128
BLOCK = 128
PLE_DIM = 256
PAST_LEN = 8192
ROPE_THETA = 10000.0
EPS = 1e-6
SCALE = HEAD_DIM ** -0.5
NEG = -1e30
IN_W = 2 * ATTN_W + 2 * KV_W + 4 * CONV_W

C_Q = 0
C_K = ATTN_W
C_V = C_K + KV_W
C_GA = C_V + KV_W
C_BG = C_GA + ATTN_W
C_CG = C_BG + CONV_W
C_HC = C_CG + CONV_W
C_GC = C_HC + CONV_W

LANES = 128
SUBLANES = 8
ROPE_HALF = HEAD_DIM // 2

MXU_N = 256
PROMPT_TILE = 512
SAMPLE_GROUP = 16
VMEM_LIMIT = 56 * 1024 * 1024
TRUNK_VMEM_LIMIT = 60 * 1024 * 1024

BF = jnp.bfloat16
F32 = jnp.float32
_NT = (((1,), (1,)), ((), ()))


def _mm(a, w):
    return jnp.dot(a, w, preferred_element_type=F32)


def _sigmoid(x):
    return 1.0 / (1.0 + jnp.exp(-x))


def _silu(x):
    return x * _sigmoid(x)


def _rmsnorm(x, g):
    r = lax.rsqrt(jnp.mean(x * x, axis=-1, keepdims=True) + EPS)
    return (x * r) * g


def _rope(x, cos, sin_signed):
    rows = x.shape[0]
    lane = lax.broadcasted_iota(jnp.int32, (rows, LANES), 1)
    first_half = (lane % HEAD_DIM) < ROPE_HALF
    outs = []
    for c in range(x.shape[1] // LANES):
        xc = x[:, c * LANES:(c + 1) * LANES]
        rot = jnp.where(first_half, pltpu.roll(xc, LANES - ROPE_HALF, 1), pltpu.roll(xc, ROPE_HALF, 1))
        outs.append(xc * cos + rot * sin_signed)
    return outs[0] if len(outs) == 1 else jnp.concatenate(outs, axis=1)


def _conv_gate(u, u1, u2, cw_ref, bg, gc):
    yc = cw_ref[0:1, :] * u2 + cw_ref[1:2, :] * u1 + cw_ref[2:3, :] * u
    return bg * yc * _silu(gc)


def _prompt_kernel(sink_ref, xa_ref, xb_ref, p_ref, cos_ref, sin_ref, gn_ref, win_ref, cw_ref, wout_ref, wpg_ref,
                   wpp_ref, gfin_ref, xo_ref, kt_ref, vt_ref, ut_ref, kbuf, vbuf, ubuf, mix_buf, xn_buf, zc_buf,
                   q_buf, sga_buf, x1_buf, x1b_buf, *, tile, n_seq, final):
    i = pl.program_id(0)
    n_tiles = pl.num_programs(0) - 1
    first_of_seq = jnp.minimum(i, n_tiles - 1) % n_seq == 0

    @pl.when(i == 0)
    def _():
        mix_buf[...] = jnp.zeros(mix_buf.shape, BF)

    @pl.when(first_of_seq)
    def _():
        kbuf[0:BLOCK, :] = jnp.zeros((BLOCK, KV_W), BF)
        vbuf[:, 0:BLOCK] = jnp.zeros((KV_W, BLOCK), BF)
        ubuf[0:SUBLANES, :] = jnp.zeros((SUBLANES, CONV_W), F32)

    n_out = D_MODEL // MXU_N
    rows_per = tile // n_out
    for c in range(n_out):
        cs = slice(c * MXU_N, (c + 1) * MXU_N)
        x1 = (xb_ref[0, :, cs] + _mm(mix_buf[:, 0:ATTN_W], wout_ref[0:ATTN_W, cs])
              + _mm(mix_buf[:, ATTN_W:D_MODEL], wout_ref[ATTN_W:D_MODEL, cs]))
        x1_buf[:, cs] = x1
        x1b_buf[:, cs] = x1.astype(BF)
        rs = slice(c * rows_per, (c + 1) * rows_per)
        xn_buf[rs, :] = _rmsnorm(xa_ref[0, rs, :], gn_ref[...]).astype(BF)
    pb = p_ref[0].astype(BF)

    def gate_unit(c):
        cs = slice(c * MXU_N, (c + 1) * MXU_N)
        gate = _sigmoid(_mm(x1b_buf[...], wpg_ref[:, cs]))
        xo_ref[0, :, cs] = x1_buf[:, cs] + gate * _mm(pb, wpp_ref[:, cs])

    cos = cos_ref[...]
    sin_s = sin_ref[...]

    def proj(c0, width=MXU_N):
        return _mm(xn_buf[...], win_ref[:, c0:c0 + width])

    kv = proj(C_K, 2 * KV_W)
    k = _rope(kv[:, 0:KV_W], cos, sin_s)
    kbuf[BLOCK:BLOCK + tile, :] = k.astype(BF)
    vt = kv[:, KV_W:2 * KV_W].T
    vbuf[:, BLOCK:BLOCK + tile] = vt.astype(BF)
    kt_ref[0] = k[tile - BLOCK:, :].T
    vt_ref[0] = vt[:, tile - BLOCK:]
    for c in range(ATTN_W // MXU_N):
        cs = slice(c * MXU_N, (c + 1) * MXU_N)
        q_buf[:, cs] = (_rope(proj(C_Q + c * MXU_N), cos, sin_s) * SCALE).astype(BF)

    conv_cols = (C_CG, C_HC, C_BG, C_GC)
    n_planes = len(conv_cols)
    n_slabs = CONV_W // MXU_N
    n_gate = ATTN_W // MXU_N
    n_dense = n_gate + n_slabs * n_planes

    def dense_unit(n):
        if n < n_gate:
            sga_buf[:, n * MXU_N:(n + 1) * MXU_N] = _silu(proj(C_GA + n * MXU_N))
        else:
            half, plane = divmod(n - n_gate, n_planes)
            zc_buf[plane, :, half * MXU_N:(half + 1) * MXU_N] = proj(conv_cols[plane] + half * MXU_N)

    def conv_unit(half):
        cs = slice(half * MXU_N, (half + 1) * MXU_N)
        u = zc_buf[0, :, cs] * zc_buf[1, :, cs]
        ubuf[SUBLANES:SUBLANES + tile, cs] = u
        u1 = ubuf[SUBLANES - 1:SUBLANES - 1 + tile, cs]
        u2 = ubuf[SUBLANES - 2:SUBLANES - 2 + tile, cs]
        yc = cw_ref[0:1, cs] * u2 + cw_ref[1:2, cs] * u1 + cw_ref[2:3, cs] * u
        mix_buf[:, ATTN_W + half * MXU_N:ATTN_W + (half + 1) * MXU_N] = (
            zc_buf[2, :, cs] * yc * _silu(zc_buf[3, :, cs])).astype(BF)
        ubuf[0:SUBLANES, cs] = ubuf[tile:tile + SUBLANES, cs]
        ut_ref[0, :, cs] = u[tile - (CONV_K - 1):, :]

    row = lax.broadcasted_iota(jnp.int32, (BLOCK, BLOCK), 0)
    col = lax.broadcasted_iota(jnp.int32, (BLOCK, BLOCK), 1)
    own = row <= col
    no_prev = jnp.where(first_of_seq, NEG, 0.0).astype(F32)

    def scores(n):
        j, g = divmod(n, N_KV_HEADS)
        kcat = kbuf[j * BLOCK:(j + 2) * BLOCK, g * HEAD_DIM:(g + 1) * HEAD_DIM]
        qst = jnp.concatenate(
            [q_buf[j * BLOCK:(j + 1) * BLOCK, (GROUP * g + h) * HEAD_DIM:(GROUP * g + h + 1) * HEAD_DIM]
             for h in range(GROUP)], axis=0)
        return lax.dot_general(kcat, qst, _NT, preferred_element_type=F32)

    def softmax(n, sc):
        j, g = divmod(n, N_KV_HEADS)
        probs, invs = [], []
        for h in range(GROUP):
            sh = sc[:, h * BLOCK:(h + 1) * BLOCK]
            s_prev = sh[:BLOCK]
            if j == 0:
                s_prev = s_prev + no_prev
            sm = jnp.where(own, sh[BLOCK:], s_prev)
            sink = sink_ref[GROUP * g + h]
            m = jnp.maximum(jnp.max(sm, axis=0, keepdims=True), sink)
            e = jnp.exp(sm - m)
            den = jnp.sum(e, axis=0, keepdims=True) + jnp.exp(sink - m)
            invs.append(1.0 / den)
            eb = e.astype(BF)
            zero = jnp.zeros_like(eb)
            probs.append(jnp.concatenate([jnp.where(own, zero, eb), jnp.where(own, eb, zero)], axis=0))
        return jnp.concatenate(probs, axis=1), jnp.concatenate(invs, axis=1)

    heads = {}

    def values(n, probs, inv):
        j, g = divmod(n, N_KV_HEADS)
        vcat = vbuf[g * HEAD_DIM:(g + 1) * HEAD_DIM, j * BLOCK:(j + 2) * BLOCK]
        o = _mm(vcat, probs) * inv
        heads.setdefault(j, []).extend(o[:, h * BLOCK:(h + 1) * BLOCK] for h in range(GROUP))
        if g == N_KV_HEADS - 1:
            rows = slice(j * BLOCK, (j + 1) * BLOCK)
            oa = jnp.concatenate(heads.pop(j), axis=0).T
            mix_buf[rows, 0:ATTN_W] = (oa * sga_buf[rows, :]).astype(BF)

    n_units = (tile // BLOCK) * N_KV_HEADS
    conv_at = {n_gate + (h + 1) * n_planes: h for h in range(n_slabs)}
    n_steps = max(n_units + 2, max(conv_at) + 1)
    gate_at = {2 * c + 1: c for c in range(n_out)}
    sc_q, sm_q = {}, {}
    for t in range(n_steps):
        if t < n_units:
            sc_q[t] = scores(t)
        if t < n_dense:
            dense_unit(t)
        if 0 <= t - 2 < n_units:
            values(t - 2, *sm_q.pop(t - 2))
        if t in gate_at:
            gate_unit(gate_at[t])
        if 0 <= t - 1 < n_units:
            sm_q[t - 1] = softmax(t - 1, sc_q.pop(t - 1))
        if t in conv_at:
            conv_unit(conv_at[t])

    kbuf[0:BLOCK, :] = kbuf[tile:tile + BLOCK, :]
    vbuf[:, 0:BLOCK] = vbuf[:, tile:tile + BLOCK]

    if final:
        xo_ref[0] = _rmsnorm(xo_ref[0], gfin_ref[...])


def _prompt_layer(i, x, p_all, cos, sin_s, sinks, g_norm, w_in, conv_w, w_out, w_pg, w_pp, g_final, final):
    batch, seq, _ = x.shape
    tile = PROMPT_TILE
    n_seq = seq // tile
    n_tiles = batch * n_seq
    in_tile = lambda t: jnp.minimum(t, n_tiles - 1)
    out_tile = lambda t: jnp.maximum(t - 1, 0)
    const = lambda t: (0, 0)
    layer = lambda t: (i, 0, 0)
    resident = dict(pipeline_mode=pl.Buffered(1))
    return pl.pallas_call(
        functools.partial(_prompt_kernel, tile=tile, n_seq=n_seq, final=final),
        name=f"prompt_layer{i}",
        grid=(n_tiles + 1,),
        in_specs=[
            pl.BlockSpec(memory_space=pltpu.SMEM),
            pl.BlockSpec((1, tile, D_MODEL), lambda t: (in_tile(t) // n_seq, in_tile(t) % n_seq, 0)),
            pl.BlockSpec((1, tile, D_MODEL), lambda t: (out_tile(t) // n_seq, out_tile(t) % n_seq, 0)),
            pl.BlockSpec((None, 1, tile, PLE_DIM),
                         lambda t: (i, out_tile(t) // n_seq, out_tile(t) % n_seq, 0)),
            pl.BlockSpec((tile, LANES), lambda t: (in_tile(t) % n_seq, 0)),
            pl.BlockSpec((tile, LANES), lambda t: (in_tile(t) % n_seq, 0)),
            pl.BlockSpec((None, 1, D_MODEL), layer, **resident),
            pl.BlockSpec((None, D_MODEL, IN_W), layer, **resident),
            pl.BlockSpec((None, CONV_K, CONV_W), layer, **resident),
            pl.BlockSpec((None, D_MODEL, D_MODEL), layer, **resident),
            pl.BlockSpec((None, D_MODEL, D_MODEL), layer, **resident),
            pl.BlockSpec((None, PLE_DIM, D_MODEL), layer, **resident),
            pl.BlockSpec((1, D_MODEL), const, **resident),
        ],
        out_specs=[
            pl.BlockSpec((1, tile, D_MODEL), lambda t: (out_tile(t) // n_seq, out_tile(t) % n_seq, 0)),
            pl.BlockSpec((1, BLOCK, KV_W), lambda t: (in_tile(t) // n_seq, 0, 0)),
            pl.BlockSpec((1, BLOCK, KV_W), lambda t: (in_tile(t) // n_seq, 0, 0)),
            pl.BlockSpec((1, CONV_K - 1, CONV_W), lambda t: (in_tile(t) // n_seq, 0, 0)),
        ],
        out_shape=[
            jax.ShapeDtypeStruct((batch, seq, D_MODEL), F32),
            jax.ShapeDtypeStruct((batch, BLOCK, KV_W), F32),
            jax.ShapeDtypeStruct((batch, BLOCK, KV_W), F32),
            jax.ShapeDtypeStruct((batch, CONV_K - 1, CONV_W), F32),
        ],
        scratch_shapes=[
            pltpu.VMEM((tile + BLOCK, KV_W), BF),
            pltpu.VMEM((KV_W, tile + BLOCK), BF),
            pltpu.VMEM((tile + SUBLANES, CONV_W), F32),
            pltpu.VMEM((tile, D_MODEL), BF),
            pltpu.VMEM((tile, D_MODEL), BF),
            pltpu.VMEM((4, tile, CONV_W), F32),
            pltpu.VMEM((tile, ATTN_W), BF),
            pltpu.VMEM((tile, ATTN_W), F32),
            pltpu.VMEM((tile, D_MODEL), F32),
            pltpu.VMEM((tile, D_MODEL), BF),
        ],
        compiler_params=pltpu.CompilerParams(dimension_semantics=("arbitrary",), vmem_limit_bytes=VMEM_LIMIT),
    )(sinks[i], x, x, p_all, cos, sin_s, g_norm, w_in, conv_w, w_out, w_pg, w_pp, g_final)


def _rope_t(x, cos_t, sin_t):
    parts = [x[r:r + ROPE_HALF] for r in range(0, x.shape[0], ROPE_HALF)]
    swapped = jnp.concatenate([parts[n ^ 1] for n in range(len(parts))], axis=0)
    return x * cos_t + swapped * sin_t


def _sample_kernel(sink_ref, x_ref, p_ref, cos_ref, sin_ref, cost_ref, sint_ref, st0_ref, st1_ref, gn_ref, win_ref,
                   cw_ref, wout_ref, wpg_ref, wpp_ref, gfin_ref, ck_ref, cv_ref,
                   xo_ref, nc0_ref, nc1_ref, nk_ref, nv_ref,
                   xs_buf, q_s, kt_s, vt_s, sga_s, mix_s, ubuf, p1_buf, p2_buf, *, rows, t_dec, chunk, nb):
    layer = pl.program_id(0)
    step = pl.program_id(1)
    last_layer = pl.num_programs(0) - 1
    last_step = pl.num_programs(1) - 1
    dec_batch = rows // t_dec
    gr = nb * t_dec
    cols = chunk * t_dec

    @pl.when((layer == 0) & (step == 0))
    def _():
        xs_buf[...] = x_ref[...]

    @pl.when(step == 0)
    def _():
        xn = _rmsnorm(xs_buf[...], gn_ref[...]).astype(BF)
        q_s[...] = (_rope(_mm(xn, win_ref[:, C_Q:C_K]), cos_ref[...], sin_ref[...]) * SCALE).astype(BF)
        kvt = lax.dot_general(win_ref[:, C_K:C_GA], xn, (((0,), (1,)), ((), ())),
                              preferred_element_type=F32)
        kt_s[...] = _rope_t(kvt[0:KV_W], cost_ref[...], sint_ref[...])
        vt_s[...] = kvt[KV_W:2 * KV_W]
        sga_s[...] = _silu(_mm(xn, win_ref[:, C_GA:C_BG]))
        u = _mm(xn, win_ref[:, C_CG:C_HC]) * _mm(xn, win_ref[:, C_HC:C_GC])
        per_batch = lambda first: pl.ds(first, dec_batch, stride=t_dec)
        for c in range(CONV_W // LANES):
            cs = slice(c * LANES, (c + 1) * LANES)
            ubuf[c, 0:SUBLANES, :] = jnp.zeros((SUBLANES, LANES), F32)
            ubuf[c, SUBLANES:SUBLANES + rows, :] = u[:, cs]
            p1_buf[c] = ubuf[c, SUBLANES - 1:SUBLANES - 1 + rows, :]
            p2_buf[c] = ubuf[c, SUBLANES - 2:SUBLANES - 2 + rows, :]
            p1_buf[c, per_batch(0), :] = st1_ref[:, cs]
            p2_buf[c, per_batch(0), :] = st0_ref[:, cs]
            p2_buf[c, per_batch(1), :] = st1_ref[:, cs]
            nc0_ref[:, cs] = ubuf[c, per_batch(SUBLANES + t_dec - 2), :]
            nc1_ref[:, cs] = ubuf[c, per_batch(SUBLANES + t_dec - 1), :]
        u1 = jnp.concatenate([p1_buf[c] for c in range(CONV_W // LANES)], axis=1)
        u2 = jnp.concatenate([p2_buf[c] for c in range(CONV_W // LANES)], axis=1)
        bg = _mm(xn, win_ref[:, C_BG:C_CG])
        gc = _mm(xn, win_ref[:, C_GC:IN_W])
        mix_s[:, ATTN_W:D_MODEL] = _conv_gate(u, u1, u2, cw_ref, bg, gc).astype(BF)

    c0 = pl.multiple_of(step * cols, cols)
    kt_new = kt_s[:, pl.ds(c0, cols)]
    vt_new = vt_s[:, pl.ds(c0, cols)]

    qrow = lax.broadcasted_iota(jnp.int32, (gr, nb * WINDOW), 0)
    ccol = lax.broadcasted_iota(jnp.int32, (gr, nb * WINDOW), 1)
    valid_c = ((ccol // WINDOW) == (qrow // t_dec)) & ((ccol % WINDOW) > (qrow % t_dec))
    qrow_n = lax.broadcasted_iota(jnp.int32, (gr, gr), 0)
    ncol = lax.broadcasted_iota(jnp.int32, (gr, gr), 1)
    valid_n = ((ncol // t_dec) == (qrow_n // t_dec)) & ((ncol % t_dec) <= (qrow_n % t_dec))
    lane = lax.broadcasted_iota(jnp.int32, (KV_W, WINDOW), 1)
    kept = lane < WINDOW - t_dec

    def update_window(b):
        place = (WINDOW - t_dec - t_dec * b) % cols
        k_in = pltpu.roll(kt_new, place, 1) if place else kt_new
        v_in = pltpu.roll(vt_new, place, 1) if place else vt_new
        nk_ref[b] = jnp.where(kept, pltpu.roll(ck_ref[b], WINDOW - t_dec, 1), k_in)
        nv_ref[b] = jnp.where(kept, pltpu.roll(cv_ref[b], WINDOW - t_dec, 1), v_in)

    def rows_of(sg):
        return pl.ds(pl.multiple_of(step * cols + sg * gr, gr), gr)

    def scores(n):
        sg, g = divmod(n, N_KV_HEADS)
        hd = slice(g * HEAD_DIM, (g + 1) * HEAD_DIM)
        qg = q_s[rows_of(sg), :]
        ktc = jnp.concatenate([ck_ref[sg * nb + b, hd, :] for b in range(nb)], axis=1).astype(BF)
        ktn = kt_new[hd, sg * gr:(sg + 1) * gr].astype(BF)
        qst = jnp.concatenate(
            [qg[:, (GROUP * g + h) * HEAD_DIM:(GROUP * g + h + 1) * HEAD_DIM] for h in range(GROUP)], axis=0)
        return _mm(qst, ktc), _mm(qst, ktn)

    def softmax(n, sc, sn):
        g = n % N_KV_HEADS
        ecs, ens, invs = [], [], []
        for h in range(GROUP):
            scm = jnp.where(valid_c, sc[h * gr:(h + 1) * gr], NEG)
            snm = jnp.where(valid_n, sn[h * gr:(h + 1) * gr], NEG)
            sink = sink_ref[layer, GROUP * g + h]
            m = jnp.maximum(jnp.maximum(jnp.max(scm, axis=-1, keepdims=True),
                                        jnp.max(snm, axis=-1, keepdims=True)), sink)
            ec = jnp.exp(scm - m)
            en = jnp.exp(snm - m)
            den = (jnp.sum(ec, axis=-1, keepdims=True) + jnp.sum(en, axis=-1, keepdims=True)
                   + jnp.exp(sink - m))
            invs.append(1.0 / den)
            ecs.append(ec.astype(BF))
            ens.append(en.astype(BF))
        return jnp.concatenate(ecs, axis=0), jnp.concatenate(ens, axis=0), invs

    heads = {}

    def values(n, ec, en, invs):
        sg, g = divmod(n, N_KV_HEADS)
        hd = slice(g * HEAD_DIM, (g + 1) * HEAD_DIM)
        vtc = jnp.concatenate([cv_ref[sg * nb + b, hd, :] for b in range(nb)], axis=1).astype(BF)
        vtn = vt_new[hd, sg * gr:(sg + 1) * gr].astype(BF)
        o = (lax.dot_general(ec, vtc, _NT, preferred_element_type=F32)
             + lax.dot_general(en, vtn, _NT, preferred_element_type=F32))
        heads.setdefault(sg, []).extend(o[h * gr:(h + 1) * gr] * invs[h] for h in range(GROUP))
        if g == N_KV_HEADS - 1:
            oa = jnp.concatenate(heads.pop(sg), axis=1)
            mix_s[rows_of(sg), 0:ATTN_W] = (oa * sga_s[rows_of(sg), :]).astype(BF)

    half = nb // N_KV_HEADS
    for n in range((chunk // nb) * N_KV_HEADS):
        sc = scores(n)
        for b in range(n * half, (n + 1) * half):
            update_window(b)
        values(n, *softmax(n, *sc))

    @pl.when(step == last_step)
    def _():
        x1 = xs_buf[...] + _mm(mix_s[...], wout_ref[...])
        gate = _sigmoid(_mm(x1.astype(BF), wpg_ref[...]))
        x2 = x1 + gate * _mm(p_ref[...].astype(BF), wpp_ref[...])
        xs_buf[...] = x2

        @pl.when(layer == last_layer)
        def _():
            xo_ref[...] = _rmsnorm(x2, gfin_ref[...])


def _sample_trunk(x, p_all, cos, sin_s, cos_t, sin_t, st0, st1, sinks, g_norm, w_in, conv_w, w_out, w_pg, w_pp,
                  g_final, cache_kt, cache_vt, t_dec):
    rows = x.shape[0]
    dec_batch = rows // t_dec
    chunk = LANES // t_dec
    nb = SAMPLE_GROUP
    const = lambda l, s: (0, 0)
    layer = lambda l, s: (l, 0, 0)
    once = dict(pipeline_mode=pl.Buffered(1))
    full = lambda shape: pl.BlockSpec(shape, const, **once)
    per_layer = lambda *shape: pl.BlockSpec((None,) + shape, layer)
    cache = pl.BlockSpec((None, chunk, KV_W, WINDOW), lambda l, s: (l, s, 0, 0))
    return pl.pallas_call(
        functools.partial(_sample_kernel, rows=rows, t_dec=t_dec, chunk=chunk, nb=nb),
        name="sample_trunk",
        grid=(DEPTH, dec_batch // chunk),
        in_specs=[
            pl.BlockSpec(memory_space=pltpu.SMEM),
            full((rows, D_MODEL)),
            per_layer(rows, PLE_DIM),
            full((rows, LANES)),
            full((rows, LANES)),
            full((KV_W, rows)),
            full((KV_W, rows)),
            per_layer(dec_batch, CONV_W),
            per_layer(dec_batch, CONV_W),
            per_layer(1, D_MODEL),
            per_layer(D_MODEL, IN_W),
            per_layer(CONV_K, CONV_W),
            per_layer(D_MODEL, D_MODEL),
            per_layer(D_MODEL, D_MODEL),
            per_layer(PLE_DIM, D_MODEL),
            full((1, D_MODEL)),
            cache,
            cache,
        ],
        out_specs=[
            pl.BlockSpec((rows, D_MODEL), const),
            pl.BlockSpec((None, dec_batch, CONV_W), layer),
            pl.BlockSpec((None, dec_batch, CONV_W), layer),
            cache,
            cache,
        ],
        out_shape=[
            jax.ShapeDtypeStruct((rows, D_MODEL), F32),
            jax.ShapeDtypeStruct((DEPTH, dec_batch, CONV_W), F32),
            jax.ShapeDtypeStruct((DEPTH, dec_batch, CONV_W), F32),
            jax.ShapeDtypeStruct((DEPTH, dec_batch, KV_W, WINDOW), F32),
            jax.ShapeDtypeStruct((DEPTH, dec_batch, KV_W, WINDOW), F32),
        ],
        scratch_shapes=[
            pltpu.VMEM((rows, D_MODEL), F32),
            pltpu.VMEM((rows, ATTN_W), BF),
            pltpu.VMEM((KV_W, rows), F32),
            pltpu.VMEM((KV_W, rows), F32),
            pltpu.VMEM((rows, ATTN_W), F32),
            pltpu.VMEM((rows, D_MODEL), BF),
            pltpu.VMEM((CONV_W // LANES, rows + SUBLANES, LANES), F32),
            pltpu.VMEM((CONV_W // LANES, rows, LANES), F32),
            pltpu.VMEM((CONV_W // LANES, rows, LANES), F32),
        ],
        compiler_params=pltpu.CompilerParams(
            dimension_semantics=("arbitrary", "arbitrary"), vmem_limit_bytes=TRUNK_VMEM_LIMIT),
    )(sinks, x, p_all, cos, sin_s, cos_t, sin_t, st0, st1, g_norm, w_in, conv_w, w_out, w_pg, w_pp, g_final,
      cache_kt, cache_vt)


def _rope_angles(pos):
    inv = ROPE_THETA ** (-jnp.arange(0, HEAD_DIM, 2, dtype=F32) / HEAD_DIM)
    return pos[:, None] * inv[None, :]


def _rope_tables(pos):
    ang = _rope_angles(pos)
    cos = jnp.tile(jnp.cos(ang), (1, 2 * LANES // HEAD_DIM))
    sin = jnp.sin(ang)
    sin_signed = jnp.tile(jnp.concatenate([-sin, sin], axis=-1), (1, LANES // HEAD_DIM))
    return cos, sin_signed


def _rope_tables_t(pos):
    ang = _rope_angles(pos).T
    cos = jnp.tile(jnp.cos(ang), (2 * KV_W // HEAD_DIM, 1))
    sin = jnp.sin(ang)
    sin_signed = jnp.tile(jnp.concatenate([-sin, sin], axis=0), (KV_W // HEAD_DIM, 1))
    return cos, sin_signed


def kernel(x_prompt, x_sample, cache_k, cache_v, state_conv, p_prompt, p_sample, g_norm, w_in, sinks, conv_w,
           w_out, w_pg, w_pp, g_final):
    batch, seq, _ = x_prompt.shape
    dec_batch, t_dec, _ = x_sample.shape
    rows = dec_batch * t_dec

    cos_p, sin_p = _rope_tables(jnp.arange(seq, dtype=F32))
    pos_s = jnp.tile(PAST_LEN + jnp.arange(t_dec, dtype=F32), dec_batch)
    cos_s, sin_s = _rope_tables(pos_s)
    cos_st, sin_st = _rope_tables_t(pos_s)

    w_in_b, w_out_b, w_pg_b, w_pp_b = (w.astype(BF) for w in (w_in, w_out, w_pg, w_pp))
    g_norm3 = g_norm.reshape(DEPTH, 1, D_MODEL)
    g_fin2 = g_final.reshape(1, D_MODEL)
    to_t = lambda c: jnp.transpose(c, (0, 1, 3, 4, 2)).reshape(DEPTH, -1, KV_W, WINDOW)
    from_t = lambda c: jnp.transpose(c.reshape(DEPTH, -1, N_KV_HEADS, HEAD_DIM, WINDOW), (0, 1, 4, 2, 3))

    xp = x_prompt
    nkp, nvp, ncp = [], [], []
    for i in range(DEPTH):
        xp, kt, vt, ut = _prompt_layer(i, xp, p_prompt, cos_p, sin_p, sinks, g_norm3, w_in_b, conv_w, w_out_b,
                                       w_pg_b, w_pp_b, g_fin2, i == DEPTH - 1)
        nkp.append(kt)
        nvp.append(vt)
        ncp.append(ut)

    xs, nc0, nc1, nks, nvs = _sample_trunk(
        x_sample.reshape(rows, D_MODEL), p_sample.reshape(DEPTH, rows, PLE_DIM), cos_s, sin_s, cos_st, sin_st,
        state_conv[:, :, 0], state_conv[:, :, 1], sinks, g_norm3, w_in_b, conv_w, w_out_b, w_pg_b, w_pp_b,
        g_fin2, to_t(cache_k), to_t(cache_v), t_dec)

    return (xp, xs.reshape(dec_batch, t_dec, D_MODEL), from_t(jnp.stack(nkp)), from_t(jnp.stack(nvp)),
            jnp.stack(ncp), from_t(nks), from_t(nvs), jnp.stack([nc0, nc1], axis=2))
```
